```python
import math
import jax, jax.numpy as jnp
from jax import lax
import numpy as np

D_MODEL = 1024
BATCH = 8
SEQ = 2048
DEPTH = 2

N_HEADS = 4
HEAD_DIM = 64
V_DIM = 2 * HEAD_DIM
ATTN_QK = N_HEADS * 2 * HEAD_DIM
ATTN_V = N_HEADS * V_DIM
ROPE_THETA = 10000.0
Q_BLOCK = 128
CONV_CH = 512
CONV_K = 31
CONV_PAD = (CONV_K - 1) // 2
IN_SPLITS = (ATTN_QK, 2 * ATTN_QK, 2 * ATTN_QK + ATTN_V,
             2 * ATTN_QK + ATTN_V + 2 * CONV_CH)
IN_COLS = 2 * ATTN_QK + ATTN_V + 2 * CONV_CH + 2 * D_MODEL
D_FF = 2816
N_EXPERTS = 8
TOP_K = 2
D_EXPERT = 3584
N_DENSE = (DEPTH + 1) // 2
N_MOE = DEPTH // 2
PLE_DIM = 256
EPS = 1e-6

kernel_name = "hybrid_diffattn_conformer_moe_encoder"


def rms_norm(x, g):
    xf = x.astype(jnp.float32)
    y = xf * lax.rsqrt(jnp.mean(xf * xf, axis=-1, keepdims=True) + EPS)
    return (y * g.astype(jnp.float32)).astype(x.dtype)


def layer_norm(x, g, b):
    xf = x.astype(jnp.float32)
    mu = jnp.mean(xf, axis=-1, keepdims=True)
    xc = xf - mu
    y = xc * lax.rsqrt(jnp.mean(xc * xc, axis=-1, keepdims=True) + EPS)
    return (y * g.astype(jnp.float32) + b.astype(jnp.float32)).astype(x.dtype)


def rope_tables(seq, dtype):
    inv_freq = ROPE_THETA ** (-jnp.arange(0, HEAD_DIM, 2, dtype=jnp.float32) / HEAD_DIM)
    ang = jnp.arange(seq, dtype=jnp.float32)[:, None] * inv_freq[None, :]
    return jnp.cos(ang).astype(dtype), jnp.sin(ang).astype(dtype)


def apply_rope(t, cos, sin):
    t1, t2 = jnp.split(t, 2, axis=-1)
    c = cos[:, None, None, :]
    s = sin[:, None, None, :]
    return jnp.concatenate([t1 * c - t2 * s, t2 * c + t1 * s], axis=-1)


def diff_attention(q, k, v, lam):
    b, s = q.shape[0], q.shape[1]
    nb = s // Q_BLOCK
    qb = q.reshape(b, nb, Q_BLOCK, N_HEADS, 2, HEAD_DIM).transpose(1, 0, 2, 3, 4, 5)
    scale = HEAD_DIM ** -0.5

    def one_block(qi):
        sc = jnp.einsum('bqhcd,bkhcd->bhcqk', qi, k).astype(jnp.float32) * scale
        pr = jax.nn.softmax(sc, axis=-1)
        a = pr[:, :, 0] - lam * pr[:, :, 1]
        return jnp.einsum('bhqk,bkhe->bqhe', a.astype(v.dtype), v)

    o = lax.map(one_block, qb)
    return o.transpose(1, 0, 2, 3, 4).reshape(b, s, N_HEADS, V_DIM)


def conformer_conv(u, conv_w, conv_b, ln_g, ln_b, w_out):
    a, g = jnp.split(u, 2, axis=-1)
    z = a * jax.nn.sigmoid(g)
    z = lax.conv_general_dilated(
        z, conv_w[:, None, :], window_strides=(1,), padding=[(CONV_PAD, CONV_PAD)],
        dimension_numbers=('NWC', 'WIO', 'NWC'), feature_group_count=CONV_CH) + conv_b
    z = jax.nn.silu(layer_norm(z, ln_g, ln_b))
    return z @ w_out


def swiglu(h, w_gu, w_down):
    g, u = jnp.split(h @ w_gu, 2, axis=-1)
    return (jax.nn.silu(g) * u) @ w_down


def moe_swiglu(h, w_router, b_router, we_gu, we_down):
    b, s, d = h.shape
    t = h.reshape(b * s, d)
    logits = (t @ w_router).astype(jnp.float32) + b_router.astype(jnp.float32)
    top_v, top_i = lax.top_k(logits, TOP_K)
    top_w = jax.nn.softmax(top_v, axis=-1)
    combine = jnp.sum(jax.nn.one_hot(top_i, N_EXPERTS, dtype=jnp.float32) * top_w[..., None], axis=1)
    combine = combine.astype(h.dtype)
    out = jnp.zeros_like(t)
    for e in range(N_EXPERTS):
        out = out + combine[:, e:e + 1] * swiglu(t, we_gu[e], we_down[e])
    return out.reshape(b, s, d)


def setup_inputs(seed: int = 0) -> dict:
    key = jax.random.key(seed)
    ks = iter(jax.random.split(key, 40))

    def w(shape, fan_in):
        return jax.random.normal(next(ks), shape, jnp.float32) * (fan_in ** -0.5)

    def gain(shape):
        return 1.0 + 0.02 * jax.random.normal(next(ks), shape, jnp.float32)

    def bias(shape, scale=0.02):
        return scale * jax.random.normal(next(ks), shape, jnp.float32)

    return {
        "x": jax.random.normal(next(ks), (BATCH, SEQ, D_MODEL), jnp.float32),
        "p": jax.random.normal(next(ks), (DEPTH, BATCH, SEQ, PLE_DIM), jnp.float32),
        "g_mix": gain((DEPTH, D_MODEL)),
        "w_in": w((DEPTH, D_MODEL, IN_COLS), D_MODEL),
        "lam": 0.1 * jax.random.normal(next(ks), (DEPTH, 4, HEAD_DIM), jnp.float32),
        "g_subln": gain((DEPTH, V_DIM)),
        "w_attn_out": w((DEPTH, ATTN_V, D_MODEL), ATTN_V),
        "conv_w": w((DEPTH, CONV_K, CONV_CH), CONV_K),
        "conv_b": bias((DEPTH, CONV_CH)),
        "conv_ln_g": gain((DEPTH, CONV_CH)),
        "conv_ln_b": bias((DEPTH, CONV_CH)),
        "w_conv_out": w((DEPTH, CONV_CH, D_MODEL), CONV_CH),
        "w_o": w((DEPTH, D_MODEL, D_MODEL), D_MODEL),
        "g_ffn": gain((DEPTH, D_MODEL)),
        "w_ff_gu": w((N_DENSE, D_MODEL, 2 * D_FF), D_MODEL),
        "w_ff_down": w((N_DENSE, D_FF, D_MODEL), D_FF),
        "w_router": w((N_MOE, D_MODEL, N_EXPERTS), D_MODEL),
        "b_router": bias((N_MOE, N_EXPERTS), 0.01),
        "we_gu": w((N_MOE, N_EXPERTS, D_MODEL, 2 * D_EXPERT), D_MODEL),
        "we_down": w((N_MOE, N_EXPERTS, D_EXPERT, D_MODEL), D_EXPERT),
        "g_ple": gain((DEPTH, D_MODEL)),
        "w_ple_gate": w((DEPTH, D_MODEL, D_MODEL), D_MODEL),
        "w_ple_proj": w((DEPTH, PLE_DIM, D_MODEL), PLE_DIM),
        "g_final": gain((D_MODEL,)),
    }


def reference(x, p, g_mix, w_in, lam, g_subln, w_attn_out, conv_w, conv_b, conv_ln_g,
              conv_ln_b, w_conv_out, w_o, g_ffn, w_ff_gu, w_ff_down, w_router, b_router,
              we_gu, we_down, g_ple, w_ple_gate, w_ple_proj, g_final):
    b, s, _ = x.shape
    cos, sin = rope_tables(s, x.dtype)
    for i in range(DEPTH):
        h = rms_norm(x, g_mix[i])
        q, k, v, u, gl = jnp.split(h @ w_in[i], IN_SPLITS, axis=-1)
        q = apply_rope(q.reshape(b, s, N_HEADS, 2, HEAD_DIM), cos, sin)
        k = apply_rope(k.reshape(b, s, N_HEADS, 2, HEAD_DIM), cos, sin)
        v = v.reshape(b, s, N_HEADS, V_DIM)
        lam_init = 0.8 - 0.6 * math.exp(-0.3 * i)
        lf = lam[i].astype(jnp.float32)
        lam_full = (jnp.exp(jnp.sum(lf[0] * lf[1])) - jnp.exp(jnp.sum(lf[2] * lf[3]))
                    + lam_init)
        o = diff_attention(q, k, v, lam_full)
        o = rms_norm(o, g_subln[i]) * (1.0 - lam_init)
        attn_d = o.reshape(b, s, ATTN_V) @ w_attn_out[i]
        conv_d = conformer_conv(u, conv_w[i], conv_b[i], conv_ln_g[i], conv_ln_b[i],
                                w_conv_out[i])
        ga, gc = jnp.split(gl, 2, axis=-1)
        x = x + (jax.nn.sigmoid(ga) * attn_d + jax.nn.sigmoid(gc) * conv_d) @ w_o[i]
        h = rms_norm(x, g_ffn[i])
        if i % 2 == 0:
            x = x + swiglu(h, w_ff_gu[i // 2], w_ff_down[i // 2])
        else:
            m = i // 2
            x = x + moe_swiglu(h, w_router[m], b_router[m], we_gu[m], we_down[m])
        h = rms_norm(x, g_ple[i])
        x = x + jax.nn.sigmoid(h @ w_ple_gate[i]) * (p[i] @ w_ple_proj[i])
    return rms_norm(x, g_final)
```

```python
import functools
import math

import jax
import jax.numpy as jnp
from jax import lax
from jax.experimental import pallas as pl
from jax.experimental.pallas import tpu as pltpu

D_MODEL = 1024
N_HEADS = 4
HEAD_DIM = 64
V_DIM = 2 * HEAD_DIM
ATTN_QK = N_HEADS * 2 * HEAD_DIM
ATTN_V = N_HEADS * V_DIM
ROPE_THETA = 10000.0
CONV_CH = 512
CONV_K = 31
CONV_PAD = (CONV_K - 1) // 2
N_EXPERTS = 8
EPS = 1e-6

LANES = 128
SUBLANES = 8
VMEM_LIMIT = 56 * 1024 * 1024

F32 = jnp.float32
BF16 = jnp.bfloat16


def _cparams(n_axes):
    return pltpu.CompilerParams(
        dimension_semantics=("arbitrary",) * n_axes, vmem_limit_bytes=VMEM_LIMIT)


def _resident(shape):
    nd = len(shape)
    return pl.BlockSpec(shape, lambda *_: (0,) * nd, pipeline_mode=pl.Buffered(1))


def _rms(x, g):
    return x * lax.rsqrt(jnp.mean(x * x, axis=-1, keepdims=True) + EPS) * g


def _dot(a, b):
    return jnp.dot(a, b, preferred_element_type=F32)


def _rope(t, cos, sin_signed):
    lane = lax.broadcasted_iota(jnp.int32, t.shape, 1)
    first_half = (lane & (HEAD_DIM // 2)) == 0
    width = t.shape[-1]
    partner = jnp.where(first_half,
                        pltpu.roll(t, width - HEAD_DIM // 2, axis=1),
                        pltpu.roll(t, HEAD_DIM // 2, axis=1))
    reps = width // LANES
    c = jnp.concatenate([cos] * reps, axis=-1)
    s = jnp.concatenate([sin_signed] * reps, axis=-1)
    return t * c + partner * s


def _in_proj_kernel(x_ref, g_ref, w_ref, cos_ref, sin_ref,
                    q_ref, k_ref, v_ref, z_ref, gate_ref):
    h = _rms(x_ref[...], g_ref[...]).astype(BF16)
    cos = cos_ref[...]
    sin = sin_ref[...]

    def proj(c0, n):
        return _dot(h, w_ref[:, c0:c0 + n])

    q = _rope(proj(0, ATTN_QK), cos, sin) * (HEAD_DIM ** -0.5)
    q_ref[...] = q.astype(BF16)
    k_ref[...] = _rope(proj(ATTN_QK, ATTN_QK), cos, sin).astype(BF16)
    v_ref[...] = proj(2 * ATTN_QK, ATTN_V).astype(BF16)
    u0 = 2 * ATTN_QK + ATTN_V
    a = proj(u0, CONV_CH)
    g = proj(u0 + CONV_CH, CONV_CH)
    z_ref[...] = a * jax.nn.sigmoid(g)
    g0 = u0 + 2 * CONV_CH
    step = 512
    for c in range(0, 2 * D_MODEL, step):
        gate_ref[:, c:c + step] = jax.nn.sigmoid(proj(g0 + c, step)).astype(BF16)


def _in_proj(x, g, w_bf, cos_t, sin_t, seq, tm=512):
    t = x.shape[0]
    n_in = w_bf.shape[1]
    seq_blocks = seq // tm
    row = lambda i: (i, 0)
    tab = lambda i: (i % seq_blocks, 0)
    return pl.pallas_call(
        _in_proj_kernel,
        grid=(t // tm,),
        in_specs=[
            pl.BlockSpec((tm, D_MODEL), row),
            _resident((1, D_MODEL)),
            _resident((D_MODEL, n_in)),
            pl.BlockSpec((tm, LANES), tab),
            pl.BlockSpec((tm, LANES), tab),
        ],
        out_specs=[
            pl.BlockSpec((tm, ATTN_QK), row),
            pl.BlockSpec((tm, ATTN_QK), row),
            pl.BlockSpec((tm, ATTN_V), row),
            pl.BlockSpec((tm, CONV_CH), row),
            pl.BlockSpec((tm, 2 * D_MODEL), row),
        ],
        out_shape=[
            jax.ShapeDtypeStruct((t, ATTN_QK), BF16),
            jax.ShapeDtypeStruct((t, ATTN_QK), BF16),
            jax.ShapeDtypeStruct((t, ATTN_V), BF16),
            jax.ShapeDtypeStruct((t, CONV_CH), F32),
            jax.ShapeDtypeStruct((t, 2 * D_MODEL), BF16),
        ],
        compiler_params=_cparams(1),
        name="in_proj",
    )(x, g, w_bf, cos_t, sin_t)


def _attn_kernel(lam_ref, q_ref, k_ref, v_ref, g_ref, o_ref, *, out_scale):
    q = q_ref[...]
    k = k_ref[...]
    v = v_ref[...]
    lane = lax.broadcasted_iota(jnp.int32, q.shape, 1)
    zero = jnp.zeros_like(q)
    contract_last = (((1,), (1,)), ((), ()))

    def softmax_pv(qc):
        s = lax.dot_general(qc, k, contract_last, preferred_element_type=F32)
        p = jnp.exp(s - jnp.max(s, axis=-1, keepdims=True))
        denom = jnp.sum(p, axis=-1, keepdims=True)
        return _dot(p.astype(BF16), v) / denom

    o0 = softmax_pv(jnp.where(lane < HEAD_DIM, q, zero))
    o1 = softmax_pv(jnp.where(lane >= HEAD_DIM, q, zero))
    o = o0 - lam_ref[0] * o1
    o_ref[...] = (_rms(o, g_ref[...]) * out_scale).astype(BF16)


def _attention(q, k, v, lam_full, g_subln, batch, seq, out_scale, tq=256):
    t = q.shape[0]
    nq = seq // tq
    kernel = functools.partial(_attn_kernel, out_scale=out_scale)
    grid_spec = pltpu.PrefetchScalarGridSpec(
        num_scalar_prefetch=1,
        grid=(batch, N_HEADS, nq),
        in_specs=[
            pl.BlockSpec((tq, V_DIM), lambda b, h, i, lam: (b * nq + i, h)),
            pl.BlockSpec((seq, V_DIM), lambda b, h, i, lam: (b, h)),
            pl.BlockSpec((seq, V_DIM), lambda b, h, i, lam: (b, h)),
            pl.BlockSpec((1, V_DIM), lambda b, h, i, lam: (0, 0)),
        ],
        out_specs=pl.BlockSpec((tq, V_DIM), lambda b, h, i, lam: (b * nq + i, h)),
    )
    return pl.pallas_call(
        kernel,
        grid_spec=grid_spec,
        out_shape=jax.ShapeDtypeStruct((t, ATTN_V), BF16),
        compiler_params=_cparams(3),
        name="diff_attn",
    )(lam_full, q, k, v, g_subln)


_CONV_HALO = 16
_CONV_ROWS = 128


def _conv_kernel(z_ref, w_ref, b_ref, lg_ref, lb_ref, o_ref, zp_ref):
    seq = z_ref.shape[0]
    halo = jnp.zeros((_CONV_HALO, CONV_CH), F32)
    zp_ref[0:_CONV_HALO, :] = halo
    zp_ref[_CONV_HALO + seq:_CONV_HALO + seq + _CONV_HALO, :] = halo
    zp_ref[_CONV_HALO:_CONV_HALO + seq, :] = z_ref[...]
    bias = b_ref[...]
    lg = lg_ref[...]
    lb = lb_ref[...]
    shift = _CONV_HALO - CONV_PAD
    for r0 in range(0, seq, _CONV_ROWS):
        cols = []
        for c0 in range(0, CONV_CH, LANES):
            acc = jnp.zeros((_CONV_ROWS, LANES), F32)
            for tap in range(CONV_K):
                lo = r0 + tap + shift
                acc = acc + zp_ref[lo:lo + _CONV_ROWS, c0:c0 + LANES] * w_ref[tap:tap + 1, c0:c0 + LANES]
            cols.append(acc)
        y = jnp.concatenate(cols, axis=-1) + bias
        mu = jnp.mean(y, axis=-1, keepdims=True)
        yc = y - mu
        yn = yc * lax.rsqrt(jnp.mean(yc * yc, axis=-1, keepdims=True) + EPS) * lg + lb
        o_ref[r0:r0 + _CONV_ROWS, :] = (yn * jax.nn.sigmoid(yn)).astype(BF16)


def _conv_module(z, conv_w, conv_b, ln_g, ln_b, batch, seq):
    t = z.shape[0]
    return pl.pallas_call(
        _conv_kernel,
        grid=(batch,),
        in_specs=[
            pl.BlockSpec((seq, CONV_CH), lambda b: (b, 0)),
            _resident((CONV_K, CONV_CH)),
            _resident((1, CONV_CH)),
            _resident((1, CONV_CH)),
            _resident((1, CONV_CH)),
        ],
        out_specs=pl.BlockSpec((seq, CONV_CH), lambda b: (b, 0)),
        out_shape=jax.ShapeDtypeStruct((t, CONV_CH), BF16),
        scratch_shapes=[pltpu.VMEM((seq + 2 * _CONV_HALO, CONV_CH), F32)],
        compiler_params=_cparams(1),
        name="conv_module",
    )(z, conv_w, conv_b, ln_g, ln_b)


def _mix_kernel(x_ref, o_ref, c_ref, gate_ref, wa_ref, wc_ref, wo_ref, out_ref):
    attn_d = _dot(o_ref[...], wa_ref[...])
    conv_d = _dot(c_ref[...], wc_ref[...])
    ga = gate_ref[:, 0:D_MODEL].astype(F32)
    gc = gate_ref[:, D_MODEL:2 * D_MODEL].astype(F32)
    mix = (ga * attn_d + gc * conv_d).astype(BF16)
    out_ref[...] = x_ref[...] + _dot(mix, wo_ref[...])


def _mix(x, o, c, gates, wa, wc, wo, tm=512):
    t = x.shape[0]
    row = lambda i: (i, 0)
    return pl.pallas_call(
        _mix_kernel,
        grid=(t // tm,),
        in_specs=[
            pl.BlockSpec((tm, D_MODEL), row),
            pl.BlockSpec((tm, ATTN_V), row),
            pl.BlockSpec((tm, CONV_CH), row),
            pl.BlockSpec((tm, 2 * D_MODEL), row),
            _resident(wa.shape),
            _resident(wc.shape),
            _resident(wo.shape),
        ],
        out_specs=pl.BlockSpec((tm, D_MODEL), row),
        out_shape=jax.ShapeDtypeStruct((t, D_MODEL), F32),
        compiler_params=_cparams(1),
        name="branch_mix",
    )(x, o, c, gates, wa, wc, wo)


def _swiglu(h, wg_ref, wu_ref, wd_ref):
    g = _dot(h, wg_ref[...])
    u = _dot(h, wu_ref[...])
    a = (g * jax.nn.sigmoid(g) * u).astype(BF16)
    return _dot(a, wd_ref[...])


def _ffn_kernel(x_ref, g_ref, wg_ref, wu_ref, wd_ref, out_ref):
    x = x_ref[...]
    h = _rms(x, g_ref[...]).astype(BF16)
    out_ref[...] = x + _swiglu(h, wg_ref, wu_ref, wd_ref)


def _ffn(x, g, w_gu, w_down, tm=256):
    t = x.shape[0]
    d_ff = w_down.shape[0]
    row = lambda i: (i, 0)
    return pl.pallas_call(
        _ffn_kernel,
        grid=(t // tm,),
        in_specs=[
            pl.BlockSpec((tm, D_MODEL), row),
            _resident((1, D_MODEL)),
            pl.BlockSpec((D_MODEL, d_ff), lambda i: (0, 0), pipeline_mode=pl.Buffered(1)),
            pl.BlockSpec((D_MODEL, d_ff), lambda i: (0, 1), pipeline_mode=pl.Buffered(1)),
            _resident((d_ff, D_MODEL)),
        ],
        out_specs=pl.BlockSpec((tm, D_MODEL), row),
        out_shape=jax.ShapeDtypeStruct((t, D_MODEL), F32),
        compiler_params=_cparams(1),
        name="dense_ffn",
    )(x, g, w_gu, w_gu, w_down)


def _router_kernel(x_ref, g_ref, wr_hi_ref, wr_lo_ref, br_ref, h_ref, comb_ref):
    hf = _rms(x_ref[...], g_ref[...])
    h_hi = hf.astype(BF16)
    h_lo = (hf - h_hi.astype(F32)).astype(BF16)
    h_ref[...] = h_hi
    logits = (_dot(h_hi, wr_hi_ref[...]) + _dot(h_lo, wr_hi_ref[...])
              + _dot(h_hi, wr_lo_ref[...])) + br_ref[...]
    lane = lax.broadcasted_iota(jnp.int32, logits.shape, 1)
    neg = jnp.full_like(logits, -jnp.inf)
    logits = jnp.where(lane < N_EXPERTS, logits, neg)
    m1 = jnp.max(logits, axis=-1, keepdims=True)
    i1 = jnp.min(jnp.where(logits == m1, lane, LANES), axis=-1, keepdims=True)
    rest = jnp.where(lane == i1, neg, logits)
    m2 = jnp.max(rest, axis=-1, keepdims=True)
    i2 = jnp.min(jnp.where(rest == m2, lane, LANES), axis=-1, keepdims=True)
    e2 = jnp.exp(m2 - m1)
    denom = 1.0 + e2
    comb_ref[...] = (jnp.where(lane == i1, 1.0 / denom, 0.0)
                     + jnp.where(lane == i2, e2 / denom, 0.0))


def _router(x, g, wr_hi, wr_lo, br, tm=512):
    t = x.shape[0]
    row = lambda i: (i, 0)
    return pl.pallas_call(
        _router_kernel,
        grid=(t // tm,),
        in_specs=[
            pl.BlockSpec((tm, D_MODEL), row),
            _resident((1, D_MODEL)),
            _resident((D_MODEL, LANES)),
            _resident((D_MODEL, LANES)),
            _resident((1, LANES)),
        ],
        out_specs=[pl.BlockSpec((tm, D_MODEL), row), pl.BlockSpec((tm, LANES), row)],
        out_shape=[jax.ShapeDtypeStruct((t, D_MODEL), BF16),
                   jax.ShapeDtypeStruct((t, LANES), F32)],
        compiler_params=_cparams(1),
        name="moe_router",
    )(x, g, wr_hi, wr_lo, br)


def _expert_kernel(acc_ref, h_ref, comb_ref, wg_ref, wu_ref, wd_ref, out_ref, *, expert):
    y = _swiglu(h_ref[...], wg_ref, wu_ref, wd_ref)
    out_ref[...] = acc_ref[...] + comb_ref[:, expert:expert + 1] * y


def _expert(acc, h, comb, w_gu, w_down, expert, tm=256):
    t = acc.shape[0]
    d_e = w_down.shape[0]
    row = lambda i: (i, 0)
    return pl.pallas_call(
        functools.partial(_expert_kernel, expert=expert),
        grid=(t // tm,),
        in_specs=[
            pl.BlockSpec((tm, D_MODEL), row),
            pl.BlockSpec((tm, D_MODEL), row),
            pl.BlockSpec((tm, LANES), row),
            pl.BlockSpec((D_MODEL, d_e), lambda i: (0, 0), pipeline_mode=pl.Buffered(1)),
            pl.BlockSpec((D_MODEL, d_e), lambda i: (0, 1), pipeline_mode=pl.Buffered(1)),
            _resident((d_e, D_MODEL)),
        ],
        out_specs=pl.BlockSpec((tm, D_MODEL), row),
        out_shape=jax.ShapeDtypeStruct((t, D_MODEL), F32),
        input_output_aliases={0: 0},
        compiler_params=_cparams(1),
        name="moe_expert",
    )(acc, h, comb, w_gu, w_gu, w_down)


def _ple_kernel(x_ref, p_ref, g_ref, wg_ref, wp_ref, gf_ref, out_ref, *, final_norm):
    x = x_ref[...]
    h = _rms(x, g_ref[...]).astype(BF16)
    gate = jax.nn.sigmoid(_dot(h, wg_ref[...]))
    y = x + gate * _dot(p_ref[...].astype(BF16), wp_ref[...])
    if final_norm:
        y = _rms(y, gf_ref[...])
    out_ref[...] = y


def _ple(x, p, g, w_gate, w_proj, g_final, final_norm, tm=512):
    t = x.shape[0]
    ple_dim = p.shape[1]
    row = lambda i: (i, 0)
    return pl.pallas_call(
        functools.partial(_ple_kernel, final_norm=final_norm),
        grid=(t // tm,),
        in_specs=[
            pl.BlockSpec((tm, D_MODEL), row),
            pl.BlockSpec((tm, ple_dim), row),
            _resident((1, D_MODEL)),
            _resident(w_gate.shape),
            _resident(w_proj.shape),
            _resident((1, D_MODEL)),
        ],
        out_specs=pl.BlockSpec((tm, D_MODEL), row),
        out_shape=jax.ShapeDtypeStruct((t, D_MODEL), F32),
        compiler_params=_cparams(1),
        name="ple",
    )(x, p, g, w_gate, w_proj, g_final)


def _rope_tables(seq):
    inv_freq = ROPE_THETA ** (-jnp.arange(0, HEAD_DIM, 2, dtype=F32) / HEAD_DIM)
    ang = jnp.arange(seq, dtype=F32)[:, None] * inv_freq[None, :]
    cos, sin = jnp.cos(ang), jnp.sin(ang)
    cos_t = jnp.concatenate([cos] * (LANES // (HEAD_DIM // 2)), axis=-1)
    sin_t = jnp.concatenate([-sin, sin] * (LANES // HEAD_DIM), axis=-1)
    return cos_t, sin_t


def kernel(x, p, g_mix, w_in, lam, g_subln, w_attn_out, conv_w, conv_b, conv_ln_g, conv_ln_b, w_conv_out, w_o, g_ffn, w_ff_gu, w_ff_down, w_router, b_router, we_gu, we_down, g_ple, w_ple_gate, w_ple_proj, g_final):
    batch, seq, _ = x.shape
    depth = w_in.shape[0]
    t = batch * seq
    cos_t, sin_t = _rope_tables(seq)
    xs = x.reshape(t, D_MODEL)
    row2 = lambda a: a.reshape(1, -1)
    for i in range(depth):
        q, k, v, z, gates = _in_proj(xs, row2(g_mix[i]), w_in[i].astype(BF16), cos_t, sin_t, seq)
        lam_init = 0.8 - 0.6 * math.exp(-0.3 * i)
        lf = lam[i].astype(F32)
        lam_full = (jnp.exp(jnp.sum(lf[0] * lf[1])) - jnp.exp(jnp.sum(lf[2] * lf[3]))
                    + lam_init).reshape(1)
        o = _attention(q, k, v, lam_full, row2(g_subln[i]), batch, seq, 1.0 - lam_init)
        c = _conv_module(z, conv_w[i], row2(conv_b[i]), row2(conv_ln_g[i]), row2(conv_ln_b[i]),
                         batch, seq)
        xs = _mix(xs, o, c, gates, w_attn_out[i].astype(BF16), w_conv_out[i].astype(BF16),
                  w_o[i].astype(BF16))
        if i % 2 == 0:
            xs = _ffn(xs, row2(g_ffn[i]), w_ff_gu[i // 2].astype(BF16),
                      w_ff_down[i // 2].astype(BF16))
        else:
            m = i // 2
            wr = jnp.pad(w_router[m], ((0, 0), (0, LANES - N_EXPERTS)))
            wr_hi = wr.astype(BF16)
            wr_lo = (wr - wr_hi.astype(F32)).astype(BF16)
            br = jnp.pad(b_router[m], (0, LANES - N_EXPERTS)).reshape(1, LANES)
            h, comb = _router(xs, row2(g_ffn[i]), wr_hi, wr_lo, br)
            for e in range(N_EXPERTS):
                xs = _expert(xs, h, comb, we_gu[m, e].astype(BF16), we_down[m, e].astype(BF16), e)
        xs = _ple(xs, p[i].reshape(t, -1), row2(g_ple[i]), w_ple_gate[i].astype(BF16),
                  w_ple_proj[i].astype(BF16), row2(g_final), final_norm=(i == depth - 1))
    return xs.reshape(batch, seq, D_MODEL)
```

```python
import functools
import math

import jax
import jax.numpy as jnp
from jax import lax
from jax.experimental import pallas as pl
from jax.experimental.pallas import tpu as pltpu

D_MODEL = 1024
N_HEADS = 4
HEAD_DIM = 64
V_DIM = 2 * HEAD_DIM
ATTN_QK = N_HEADS * 2 * HEAD_DIM
ATTN_V = N_HEADS * V_DIM
ROPE_THETA = 10000.0
CONV_CH = 512
CONV_K = 31
CONV_PAD = (CONV_K - 1) // 2
N_EXPERTS = 8
EPS = 1e-6

LANES = 128
SUBLANES = 8
VMEM_LIMIT = 56 * 1024 * 1024
VMEM_LIMIT_EXPERT = 60 * 1024 * 1024

F32 = jnp.float32
BF16 = jnp.bfloat16


def _cparams(n_axes):
    return pltpu.CompilerParams(
        dimension_semantics=("arbitrary",) * n_axes, vmem_limit_bytes=VMEM_LIMIT)


def _resident(shape):
    nd = len(shape)
    return pl.BlockSpec(shape, lambda *_: (0,) * nd, pipeline_mode=pl.Buffered(1))


def _rms(x, g):
    return x * lax.rsqrt(jnp.mean(x * x, axis=-1, keepdims=True) + EPS) * g


def _dot(a, b):
    return jnp.dot(a, b, preferred_element_type=F32)


def _rope(t, cos, sin_signed):
    lane = lax.broadcasted_iota(jnp.int32, t.shape, 1)
    first_half = (lane & (HEAD_DIM // 2)) == 0
    width = t.shape[-1]
    partner = jnp.where(first_half,
                        pltpu.roll(t, width - HEAD_DIM // 2, axis=1),
                        pltpu.roll(t, HEAD_DIM // 2, axis=1))
    reps = width // LANES
    c = jnp.concatenate([cos] * reps, axis=-1)
    s = jnp.concatenate([sin_signed] * reps, axis=-1)
    return t * c + partner * s


def _in_proj_kernel(x_ref, g_ref, w_ref, cos_ref, sin_ref,
                    q_ref, k_ref, v_ref, z_ref, gate_ref):
    h = _rms(x_ref[...], g_ref[...]).astype(BF16)
    cos = cos_ref[...]
    sin = sin_ref[...]

    def proj(c0, n):
        return _dot(h, w_ref[:, c0:c0 + n])

    q = _rope(proj(0, ATTN_QK), cos, sin) * (HEAD_DIM ** -0.5)
    q_ref[...] = q.astype(BF16)
    k_ref[...] = _rope(proj(ATTN_QK, ATTN_QK), cos, sin).astype(BF16)
    v_ref[...] = proj(2 * ATTN_QK, ATTN_V).astype(BF16)
    u0 = 2 * ATTN_QK + ATTN_V
    a = proj(u0, CONV_CH)
    g = proj(u0 + CONV_CH, CONV_CH)
    z_ref[...] = a * jax.nn.sigmoid(g)
    g0 = u0 + 2 * CONV_CH
    step = 512
    for c in range(0, 2 * D_MODEL, step):
        gate_ref[:, c:c + step] = jax.nn.sigmoid(proj(g0 + c, step)).astype(BF16)


def _in_proj(x, g, w_bf, cos_t, sin_t, seq, tm=512):
    t = x.shape[0]
    n_in = w_bf.shape[1]
    seq_blocks = seq // tm
    row = lambda i: (i, 0)
    tab = lambda i: (i % seq_blocks, 0)
    return pl.pallas_call(
        _in_proj_kernel,
        grid=(t // tm,),
        in_specs=[
            pl.BlockSpec((tm, D_MODEL), row),
            _resident((1, D_MODEL)),
            _resident((D_MODEL, n_in)),
            pl.BlockSpec((tm, LANES), tab),
            pl.BlockSpec((tm, LANES), tab),
        ],
        out_specs=[
            pl.BlockSpec((tm, ATTN_QK), row),
            pl.BlockSpec((tm, ATTN_QK), row),
            pl.BlockSpec((tm, ATTN_V), row),
            pl.BlockSpec((tm, CONV_CH), row),
            pl.BlockSpec((tm, 2 * D_MODEL), row),
        ],
        out_shape=[
            jax.ShapeDtypeStruct((t, ATTN_QK), BF16),
            jax.ShapeDtypeStruct((t, ATTN_QK), BF16),
            jax.ShapeDtypeStruct((t, ATTN_V), BF16),
            jax.ShapeDtypeStruct((t, CONV_CH), F32),
            jax.ShapeDtypeStruct((t, 2 * D_MODEL), BF16),
        ],
        compiler_params=_cparams(1),
        name="in_proj",
    )(x, g, w_bf, cos_t, sin_t)


_ATTN_SUB = 128


def _attn_kernel(lam_ref, q_ref, k_ref, v_ref, g_ref, o_ref, v1_ref, *, out_scale):
    @pl.when(pl.program_id(2) == 0)
    def _():
        v1_ref[:, 0:V_DIM] = v_ref[...]
        v1_ref[:, V_DIM:2 * V_DIM] = jnp.ones(v_ref.shape, BF16)

    k = k_ref[...]
    v1 = v1_ref[...]
    contract_last = (((1,), (1,)), ((), ()))

    def softmax_pv(qc):
        s = lax.dot_general(qc, k, contract_last, preferred_element_type=F32)
        p = jnp.exp(s - jnp.max(s, axis=-1, keepdims=True)).astype(BF16)
        ov = _dot(p, v1)
        return ov[:, 0:V_DIM] / ov[:, V_DIM:V_DIM + 1]

    for r0 in range(0, q_ref.shape[0], _ATTN_SUB):
        q = q_ref[r0:r0 + _ATTN_SUB, :]
        lane = lax.broadcasted_iota(jnp.int32, q.shape, 1)
        zero = jnp.zeros_like(q)
        o0 = softmax_pv(jnp.where(lane < HEAD_DIM, q, zero))
        o1 = softmax_pv(jnp.where(lane >= HEAD_DIM, q, zero))
        o = o0 - lam_ref[0] * o1
        o_ref[r0:r0 + _ATTN_SUB, :] = (_rms(o, g_ref[...]) * out_scale).astype(BF16)


def _attention(q, k, v, lam_full, g_subln, batch, seq, out_scale, tq=1024):
    t = q.shape[0]
    nq = seq // tq
    kernel = functools.partial(_attn_kernel, out_scale=out_scale)
    grid_spec = pltpu.PrefetchScalarGridSpec(
        num_scalar_prefetch=1,
        grid=(batch, N_HEADS, nq),
        in_specs=[
            pl.BlockSpec((tq, V_DIM), lambda b, h, i, lam: (b * nq + i, h)),
            pl.BlockSpec((seq, V_DIM), lambda b, h, i, lam: (b, h)),
            pl.BlockSpec((seq, V_DIM), lambda b, h, i, lam: (b, h)),
            pl.BlockSpec((1, V_DIM), lambda b, h, i, lam: (0, 0)),
        ],
        out_specs=pl.BlockSpec((tq, V_DIM), lambda b, h, i, lam: (b * nq + i, h)),
        scratch_shapes=[pltpu.VMEM((seq, 2 * V_DIM), BF16)],
    )
    return pl.pallas_call(
        kernel,
        grid_spec=grid_spec,
        out_shape=jax.ShapeDtypeStruct((t, ATTN_V), BF16),
        compiler_params=_cparams(3),
        name="diff_attn",
    )(lam_full, q, k, v, g_subln)


_CONV_HALO = 16
_CONV_ROWS = 128


def _conv_kernel(z_ref, w_ref, b_ref, lg_ref, lb_ref, o_ref, zp_ref):
    seq = z_ref.shape[0]
    halo = jnp.zeros((_CONV_HALO, CONV_CH), F32)
    zp_ref[0:_CONV_HALO, :] = halo
    zp_ref[_CONV_HALO + seq:_CONV_HALO + seq + _CONV_HALO, :] = halo
    zp_ref[_CONV_HALO:_CONV_HALO + seq, :] = z_ref[...]
    bias = b_ref[...]
    lg = lg_ref[...]
    lb = lb_ref[...]
    shift = _CONV_HALO - CONV_PAD
    for r0 in range(0, seq, _CONV_ROWS):
        cols = []
        for c0 in range(0, CONV_CH, LANES):
            acc = jnp.zeros((_CONV_ROWS, LANES), F32)
            for tap in range(CONV_K):
                lo = r0 + tap + shift
                acc = acc + zp_ref[lo:lo + _CONV_ROWS, c0:c0 + LANES] * w_ref[tap:tap + 1, c0:c0 + LANES]
            cols.append(acc)
        y = jnp.concatenate(cols, axis=-1) + bias
        mu = jnp.mean(y, axis=-1, keepdims=True)
        yc = y - mu
        yn = yc * lax.rsqrt(jnp.mean(yc * yc, axis=-1, keepdims=True) + EPS) * lg + lb
        o_ref[r0:r0 + _CONV_ROWS, :] = (yn * jax.nn.sigmoid(yn)).astype(BF16)


def _conv_module(z, conv_w, conv_b, ln_g, ln_b, batch, seq):
    t = z.shape[0]
    return pl.pallas_call(
        _conv_kernel,
        grid=(batch,),
        in_specs=[
            pl.BlockSpec((seq, CONV_CH), lambda b: (b, 0)),
            _resident((CONV_K, CONV_CH)),
            _resident((1, CONV_CH)),
            _resident((1, CONV_CH)),
            _resident((1, CONV_CH)),
        ],
        out_specs=pl.BlockSpec((seq, CONV_CH), lambda b: (b, 0)),
        out_shape=jax.ShapeDtypeStruct((t, CONV_CH), BF16),
        scratch_shapes=[pltpu.VMEM((seq + 2 * _CONV_HALO, CONV_CH), F32)],
        compiler_params=_cparams(1),
        name="conv_module",
    )(z, conv_w, conv_b, ln_g, ln_b)


def _mix_kernel(x_ref, o_ref, c_ref, gate_ref, wa_ref, wc_ref, wo_ref, out_ref):
    attn_d = _dot(o_ref[...], wa_ref[...])
    conv_d = _dot(c_ref[...], wc_ref[...])
    ga = gate_ref[:, 0:D_MODEL].astype(F32)
    gc = gate_ref[:, D_MODEL:2 * D_MODEL].astype(F32)
    mix = (ga * attn_d + gc * conv_d).astype(BF16)
    out_ref[...] = x_ref[...] + _dot(mix, wo_ref[...])


def _mix(x, o, c, gates, wa, wc, wo, tm=512):
    t = x.shape[0]
    row = lambda i: (i, 0)
    return pl.pallas_call(
        _mix_kernel,
        grid=(t // tm,),
        in_specs=[
            pl.BlockSpec((tm, D_MODEL), row),
            pl.BlockSpec((tm, ATTN_V), row),
            pl.BlockSpec((tm, CONV_CH), row),
            pl.BlockSpec((tm, 2 * D_MODEL), row),
            _resident(wa.shape),
            _resident(wc.shape),
            _resident(wo.shape),
        ],
        out_specs=pl.BlockSpec((tm, D_MODEL), row),
        out_shape=jax.ShapeDtypeStruct((t, D_MODEL), F32),
        compiler_params=_cparams(1),
        name="branch_mix",
    )(x, o, c, gates, wa, wc, wo)


def _swiglu(h, wg_ref, wu_ref, wd_ref):
    g = _dot(h, wg_ref[...])
    u = _dot(h, wu_ref[...])
    a = (g * jax.nn.sigmoid(g) * u).astype(BF16)
    return _dot(a, wd_ref[...])


def _ffn_kernel(x_ref, g_ref, wg_ref, wu_ref, wd_ref, out_ref):
    x = x_ref[...]
    h = _rms(x, g_ref[...]).astype(BF16)
    out_ref[...] = x + _swiglu(h, wg_ref, wu_ref, wd_ref)


def _ffn(x, g, w_gu, w_down, tm=256):
    t = x.shape[0]
    d_ff = w_down.shape[0]
    row = lambda i: (i, 0)
    return pl.pallas_call(
        _ffn_kernel,
        grid=(t // tm,),
        in_specs=[
            pl.BlockSpec((tm, D_MODEL), row),
            _resident((1, D_MODEL)),
            pl.BlockSpec((D_MODEL, d_ff), lambda i: (0, 0), pipeline_mode=pl.Buffered(1)),
            pl.BlockSpec((D_MODEL, d_ff), lambda i: (0, 1), pipeline_mode=pl.Buffered(1)),
            _resident((d_ff, D_MODEL)),
        ],
        out_specs=pl.BlockSpec((tm, D_MODEL), row),
        out_shape=jax.ShapeDtypeStruct((t, D_MODEL), F32),
        compiler_params=_cparams(1),
        name="dense_ffn",
    )(x, g, w_gu, w_gu, w_down)


TM_EXPERT = 256
META_E1, META_E2, META_R1, META_R2, META_W1, META_W2 = range(6)


def _row_copy(src_ref, src_row, dst_ref, dst_row, sem):
    return pltpu.make_async_copy(src_ref.at[pl.ds(src_row, 1)], dst_ref.at[pl.ds(dst_row, 1)], sem)


def _router_kernel(x_ref, g_ref, wr_hi_ref, wr_lo_ref, br_ref, h_ref, meta_ref, cnt_ref, carry_ref):
    @pl.when(pl.program_id(0) == 0)
    def _():
        carry_ref[...] = jnp.zeros_like(carry_ref)

    hf = _rms(x_ref[...], g_ref[...])
    h_ref[...] = hf
    h_hi = hf.astype(BF16)
    h_lo = (hf - h_hi.astype(F32)).astype(BF16)
    logits = (_dot(h_hi, wr_hi_ref[...]) + _dot(h_lo, wr_hi_ref[...])
              + _dot(h_hi, wr_lo_ref[...])) + br_ref[...]
    tm = logits.shape[0]
    lane = lax.broadcasted_iota(jnp.int32, logits.shape, 1)
    neg = jnp.full_like(logits, -jnp.inf)
    logits = jnp.where(lane < N_EXPERTS, logits, neg)
    m1 = jnp.max(logits, axis=-1, keepdims=True)
    i1 = jnp.min(jnp.where(logits == m1, lane, LANES), axis=-1, keepdims=True)
    rest = jnp.where(lane == i1, neg, logits)
    m2 = jnp.max(rest, axis=-1, keepdims=True)
    i2 = jnp.min(jnp.where(rest == m2, lane, LANES), axis=-1, keepdims=True)
    e2 = jnp.exp(m2 - m1)
    denom = 1.0 + e2
    pick1 = lane == i1
    pick2 = lane == i2
    onehot = jnp.where(pick1 | pick2, 1.0, 0.0)
    r = lax.broadcasted_iota(jnp.int32, (tm, tm), 0)
    c = lax.broadcasted_iota(jnp.int32, (tm, tm), 1)
    strict_lower = jnp.where(r > c, 1.0, 0.0).astype(BF16)
    before = _dot(strict_lower, onehot.astype(BF16)) + carry_ref[...]
    rank1 = jnp.sum(jnp.where(pick1, before, 0.0), axis=-1, keepdims=True)
    rank2 = jnp.sum(jnp.where(pick2, before, 0.0), axis=-1, keepdims=True)
    record = jnp.zeros_like(logits)
    for slot, val in ((META_E1, i1.astype(F32)), (META_E2, i2.astype(F32)),
                      (META_R1, rank1), (META_R2, rank2),
                      (META_W1, 1.0 / denom), (META_W2, e2 / denom)):
        record = jnp.where(lane == slot, val, record)
    meta_ref[...] = record
    carry_ref[...] += jnp.sum(onehot, axis=0, keepdims=True)
    cnt_ref[...] = carry_ref[...]


def _router(x, g, wr_hi, wr_lo, br, tm=512):
    t = x.shape[0]
    row = lambda i: (i, 0)
    return pl.pallas_call(
        _router_kernel,
        grid=(t // tm,),
        in_specs=[
            pl.BlockSpec((tm, D_MODEL), row),
            _resident((1, D_MODEL)),
            _resident((D_MODEL, LANES)),
            _resident((D_MODEL, LANES)),
            _resident((1, LANES)),
        ],
        out_specs=[pl.BlockSpec((tm, D_MODEL), row), pl.BlockSpec((tm, LANES), row),
                   pl.BlockSpec((1, LANES), lambda i: (0, 0))],
        out_shape=[jax.ShapeDtypeStruct((t, D_MODEL), F32),
                   jax.ShapeDtypeStruct((t, LANES), F32),
                   jax.ShapeDtypeStruct((1, LANES), F32)],
        scratch_shapes=[pltpu.VMEM((1, LANES), F32)],
        compiler_params=_cparams(1),
        name="moe_router",
    )(x, g, wr_hi, wr_lo, br)


def _dispatch_kernel(ends_ref, sizes_ref, h_ref, p1_ref, p2_ref, xs_ref, zero_ref, row_sem, zero_sem):
    tm = h_ref.shape[0]

    @pl.when(pl.program_id(0) == 0)
    def _():
        zero_ref[...] = jnp.zeros_like(zero_ref)
        n_tiles = xs_ref.shape[0] // TM_EXPERT
        total = ends_ref[N_EXPERTS - 1]

        def zero_tile(row):
            start = pl.multiple_of(row, TM_EXPERT)
            return pltpu.make_async_copy(zero_ref, xs_ref.at[pl.ds(start, TM_EXPERT)], zero_sem)

        fills = [(sizes_ref[e] > 0, ends_ref[e] - TM_EXPERT) for e in range(N_EXPERTS)]
        fills += [(j * TM_EXPERT >= total, j * TM_EXPERT) for j in range(n_tiles - N_EXPERTS, n_tiles)]
        for needed, row in fills:
            @pl.when(needed)
            def _():
                zero_tile(row).start()
        for needed, row in fills:
            @pl.when(needed)
            def _():
                zero_tile(row).wait()

    def scatter_row(r, carry):
        _row_copy(h_ref, r, xs_ref, p1_ref[0, 0, r], row_sem).start()
        _row_copy(h_ref, r, xs_ref, p2_ref[0, 0, r], row_sem).start()
        return carry

    lax.fori_loop(0, tm, scatter_row, 0, unroll=8)
    for _ in range(2):
        pltpu.make_async_copy(h_ref, xs_ref.at[pl.ds(0, tm)], row_sem).wait()


def _dispatch(h, pos1, pos2, ends, sizes, n_rows, tm=512):
    t = h.shape[0]
    idx = lambda a: a.reshape(t // tm, 1, tm)
    smem_block = pl.BlockSpec((1, 1, tm), lambda i, *_: (i, 0, 0), memory_space=pltpu.SMEM)
    grid_spec = pltpu.PrefetchScalarGridSpec(
        num_scalar_prefetch=2,
        grid=(t // tm,),
        in_specs=[pl.BlockSpec((tm, D_MODEL), lambda i, *_: (i, 0)), smem_block, smem_block],
        out_specs=pl.BlockSpec(memory_space=pl.ANY),
        scratch_shapes=[pltpu.VMEM((TM_EXPERT, D_MODEL), F32),
                        pltpu.SemaphoreType.DMA(()), pltpu.SemaphoreType.DMA(())],
    )
    return pl.pallas_call(
        _dispatch_kernel,
        grid_spec=grid_spec,
        out_shape=jax.ShapeDtypeStruct((n_rows, D_MODEL), F32),
        compiler_params=_cparams(1),
        name="moe_dispatch",
    )(ends, sizes, h, idx(pos1), idx(pos2))


def _expert_kernel(tile_expert_ref, tile_src_ref, x_ref, wg_ref, wu_ref, wd_ref, y_ref):
    j = pl.program_id(0)

    @pl.when(tile_src_ref[j] != j)
    def _():
        y_ref[...] = jnp.zeros_like(y_ref)

    @pl.when(tile_src_ref[j] == j)
    def _():
        xb = x_ref[...].astype(BF16)
        half = wd_ref.shape[0] // 2
        y = None
        for c0 in (0, half):
            g = _dot(xb, wg_ref[:, c0:c0 + half])
            u = _dot(xb, wu_ref[:, c0:c0 + half])
            a = (g * jax.nn.sigmoid(g) * u).astype(BF16)
            part = _dot(a, wd_ref[c0:c0 + half, :])
            y = part if y is None else y + part
        y_ref[...] = y


def _experts(xs, tile_expert, tile_src, we_gu, we_down):
    n_rows = xs.shape[0]
    d_e = we_down.shape[1]
    tile = lambda j, te, ts: (ts[j], 0)
    grid_spec = pltpu.PrefetchScalarGridSpec(
        num_scalar_prefetch=2,
        grid=(n_rows // TM_EXPERT,),
        in_specs=[
            pl.BlockSpec((TM_EXPERT, D_MODEL), tile),
            pl.BlockSpec((None, D_MODEL, d_e), lambda j, te, ts: (te[j], 0, 0)),
            pl.BlockSpec((None, D_MODEL, d_e), lambda j, te, ts: (te[j], 0, 1)),
            pl.BlockSpec((None, d_e, D_MODEL), lambda j, te, ts: (te[j], 0, 0)),
        ],
        out_specs=pl.BlockSpec((TM_EXPERT, D_MODEL), lambda j, te, ts: (j, 0)),
    )
    return pl.pallas_call(
        _expert_kernel,
        grid_spec=grid_spec,
        out_shape=jax.ShapeDtypeStruct((n_rows, D_MODEL), F32),
        compiler_params=pltpu.CompilerParams(
            dimension_semantics=("arbitrary",), vmem_limit_bytes=VMEM_LIMIT_EXPERT),
        name="moe_experts",
    )(tile_expert, tile_src, xs, we_gu, we_gu, we_down)


def _ple_tail(x, p_ref, g_ref, wg_ref, wp_ref, gf_ref, out_ref, final_norm):
    h = _rms(x, g_ref[...]).astype(BF16)
    gate = jax.nn.sigmoid(_dot(h, wg_ref[...]))
    y = x + gate * _dot(p_ref[...].astype(BF16), wp_ref[...])
    if final_norm:
        y = _rms(y, gf_ref[...])
    out_ref[...] = y


def _ple_kernel(x_ref, p_ref, g_ref, wg_ref, wp_ref, gf_ref, out_ref, *, final_norm):
    _ple_tail(x_ref[...], p_ref, g_ref, wg_ref, wp_ref, gf_ref, out_ref, final_norm)


def _combine_ple_kernel(x_ref, meta_ref, p1_ref, p2_ref, p1_next_ref, p2_next_ref, y_ref,
                        p_ref, g_ref, wg_ref, wp_ref, gf_ref, out_ref, rows_ref, sems, *, final_norm):
    i = pl.program_id(0)
    n = pl.num_programs(0)
    tm = x_ref.shape[0]
    slot = i % 2

    def gather(p1, p2, s):
        def gather_row(r, carry):
            _row_copy(y_ref, p1[0, 0, r], rows_ref.at[s, 0], r, sems.at[s]).start()
            _row_copy(y_ref, p2[0, 0, r], rows_ref.at[s, 1], r, sems.at[s]).start()
            return carry
        lax.fori_loop(0, tm, gather_row, 0, unroll=8)

    @pl.when(i == 0)
    def _():
        gather(p1_ref, p2_ref, 0)

    @pl.when(i + 1 < n)
    def _():
        gather(p1_next_ref, p2_next_ref, 1 - slot)

    for k in range(2):
        pltpu.make_async_copy(y_ref.at[pl.ds(0, tm)], rows_ref.at[slot, k], sems.at[slot]).wait()
    meta = meta_ref[...]
    x = (x_ref[...] + meta[:, META_W1:META_W1 + 1] * rows_ref[slot, 0]
         + meta[:, META_W2:META_W2 + 1] * rows_ref[slot, 1])
    _ple_tail(x, p_ref, g_ref, wg_ref, wp_ref, gf_ref, out_ref, final_norm)


def _combine_ple(x, meta, pos1, pos2, y, p, g, w_gate, w_proj, g_final, final_norm, tm=256):
    t = x.shape[0]
    nb = t // tm
    ple_dim = p.shape[1]
    row = lambda i: (i, 0)
    idx = lambda a: a.reshape(nb, 1, tm)
    cur = pl.BlockSpec((1, 1, tm), lambda i: (i, 0, 0), memory_space=pltpu.SMEM)
    nxt = pl.BlockSpec((1, 1, tm), lambda i: (jnp.minimum(i + 1, nb - 1), 0, 0),
                       memory_space=pltpu.SMEM)
    return pl.pallas_call(
        functools.partial(_combine_ple_kernel, final_norm=final_norm),
        grid=(nb,),
        in_specs=[
            pl.BlockSpec((tm, D_MODEL), row),
            pl.BlockSpec((tm, LANES), row),
            cur, cur, nxt, nxt,
            pl.BlockSpec(memory_space=pl.ANY),
            pl.BlockSpec((tm, ple_dim), row),
            _resident((1, D_MODEL)),
            _resident(w_gate.shape),
            _resident(w_proj.shape),
            _resident((1, D_MODEL)),
        ],
        out_specs=pl.BlockSpec((tm, D_MODEL), row),
        out_shape=jax.ShapeDtypeStruct((t, D_MODEL), F32),
        scratch_shapes=[pltpu.VMEM((2, 2, tm, D_MODEL), F32), pltpu.SemaphoreType.DMA((2,))],
        compiler_params=_cparams(1),
        name="moe_combine_ple",
    )(x, meta, idx(pos1), idx(pos2), idx(pos1), idx(pos2), y, p, g, w_gate, w_proj, g_final)


def _moe_plan(meta, counts):
    cnt = counts[0, :N_EXPERTS].astype(jnp.int32)
    sizes = (cnt + TM_EXPERT - 1) // TM_EXPERT * TM_EXPERT
    ends = jnp.cumsum(sizes)
    starts = ends - sizes
    e1 = meta[:, META_E1].astype(jnp.int32)
    e2 = meta[:, META_E2].astype(jnp.int32)
    start_of = lambda e: jnp.sum(jnp.where(e[:, None] == jnp.arange(N_EXPERTS)[None, :],
                                           starts[None, :], 0), axis=1)
    pos1 = start_of(e1) + meta[:, META_R1].astype(jnp.int32)
    pos2 = start_of(e2) + meta[:, META_R2].astype(jnp.int32)
    n_rows = 2 * meta.shape[0] + N_EXPERTS * TM_EXPERT
    n_valid = ends[-1] // TM_EXPERT
    tile_src = jnp.minimum(jnp.arange(n_rows // TM_EXPERT, dtype=jnp.int32), n_valid - 1)
    tile_expert = jnp.sum((tile_src * TM_EXPERT)[:, None] >= ends[None, :], axis=1).astype(jnp.int32)
    return pos1, pos2, ends.astype(jnp.int32), sizes, tile_expert, tile_src, n_rows


def _ple(x, p, g, w_gate, w_proj, g_final, final_norm, tm=512):
    t = x.shape[0]
    ple_dim = p.shape[1]
    row = lambda i: (i, 0)
    return pl.pallas_call(
        functools.partial(_ple_kernel, final_norm=final_norm),
        grid=(t // tm,),
        in_specs=[
            pl.BlockSpec((tm, D_MODEL), row),
            pl.BlockSpec((tm, ple_dim), row),
            _resident((1, D_MODEL)),
            _resident(w_gate.shape),
            _resident(w_proj.shape),
            _resident((1, D_MODEL)),
        ],
        out_specs=pl.BlockSpec((tm, D_MODEL), row),
        out_shape=jax.ShapeDtypeStruct((t, D_MODEL), F32),
        compiler_params=_cparams(1),
        name="ple",
    )(x, p, g, w_gate, w_proj, g_final)


def _rope_tables(seq):
    inv_freq = ROPE_THETA ** (-jnp.arange(0, HEAD_DIM, 2, dtype=F32) / HEAD_DIM)
    ang = jnp.arange(seq, dtype=F32)[:, None] * inv_freq[None, :]
    cos, sin = jnp.cos(ang), jnp.sin(ang)
    cos_t = jnp.concatenate([cos] * (LANES // (HEAD_DIM // 2)), axis=-1)
    sin_t = jnp.concatenate([-sin, sin] * (LANES // HEAD_DIM), axis=-1)
    return cos_t, sin_t


def kernel(x, p, g_mix, w_in, lam, g_subln, w_attn_out, conv_w, conv_b, conv_ln_g, conv_ln_b, w_conv_out, w_o, g_ffn, w_ff_gu, w_ff_down, w_router, b_router, we_gu, we_down, g_ple, w_ple_gate, w_ple_proj, g_final):
    batch, seq, _ = x.shape
    depth = w_in.shape[0]
    t = batch * seq
    cos_t, sin_t = _rope_tables(seq)
    xs = x.reshape(t, D_MODEL)
    row2 = lambda a: a.reshape(1, -1)
    for i in range(depth):
        q, k, v, z, gates = _in_proj(xs, row2(g_mix[i]), w_in[i].astype(BF16), cos_t, sin_t, seq)
        lam_init = 0.8 - 0.6 * math.exp(-0.3 * i)
        lf = lam[i].astype(F32)
        lam_full = (jnp.exp(jnp.sum(lf[0] * lf[1])) - jnp.exp(jnp.sum(lf[2] * lf[3]))
                    + lam_init).reshape(1)
        o = _attention(q, k, v, lam_full, row2(g_subln[i]), batch, seq, 1.0 - lam_init)
        c = _conv_module(z, conv_w[i], row2(conv_b[i]), row2(conv_ln_g[i]), row2(conv_ln_b[i]),
                         batch, seq)
        xs = _mix(xs, o, c, gates, w_attn_out[i].astype(BF16), w_conv_out[i].astype(BF16),
                  w_o[i].astype(BF16))
        if i % 2 == 0:
            xs = _ffn(xs, row2(g_ffn[i]), w_ff_gu[i // 2].astype(BF16),
                      w_ff_down[i // 2].astype(BF16))
        else:
            m = i // 2
            wr = jnp.pad(w_router[m], ((0, 0), (0, LANES - N_EXPERTS)))
            wr_hi = wr.astype(BF16)
            wr_lo = (wr - wr_hi.astype(F32)).astype(BF16)
            br = jnp.pad(b_router[m], (0, LANES - N_EXPERTS)).reshape(1, LANES)
            h, meta, counts = _router(xs, row2(g_ffn[i]), wr_hi, wr_lo, br)
            pos1, pos2, ends, sizes, tile_expert, tile_src, n_rows = _moe_plan(meta, counts)
            routed = _dispatch(h, pos1, pos2, ends, sizes, n_rows)
            y = _experts(routed, tile_expert, tile_src, we_gu[m].astype(BF16), we_down[m].astype(BF16))
        ple_args = (p[i].reshape(t, -1), row2(g_ple[i]), w_ple_gate[i].astype(BF16),
                    w_ple_proj[i].astype(BF16), row2(g_final))
        final_norm = i == depth - 1
        if i % 2 == 0:
            xs = _ple(xs, *ple_args, final_norm=final_norm)
        else:
            xs = _combine_ple(xs, meta, pos1, pos2, y, *ple_args, final_norm=final_norm)
    return xs.reshape(batch, seq, D_MODEL)
```

```python
import functools
import math

import jax
import jax.numpy as jnp
from jax import lax
from jax.experimental import pallas as pl
from jax.experimental.pallas import tpu as pltpu

D_MODEL = 1024
N_HEADS = 4
HEAD_DIM = 64
V_DIM = 2 * HEAD_DIM
ATTN_QK = N_HEADS * 2 * HEAD_DIM
ATTN_V = N_HEADS * V_DIM
ROPE_THETA = 10000.0
CONV_CH = 512
CONV_K = 31
CONV_PAD = (CONV_K - 1) // 2
N_EXPERTS = 8
EPS = 1e-6

LANES = 128
SUBLANES = 8
VMEM_LIMIT = 56 * 1024 * 1024
VMEM_LIMIT_EXPERT = 60 * 1024 * 1024

F32 = jnp.float32
BF16 = jnp.bfloat16


def _cparams(n_axes):
    return pltpu.CompilerParams(
        dimension_semantics=("arbitrary",) * n_axes, vmem_limit_bytes=VMEM_LIMIT)


def _resident(shape):
    nd = len(shape)
    return pl.BlockSpec(shape, lambda *_: (0,) * nd, pipeline_mode=pl.Buffered(1))


def _rms(x, g):
    return x * lax.rsqrt(jnp.mean(x * x, axis=-1, keepdims=True) + EPS) * g


def _dot(a, b):
    return jnp.dot(a, b, preferred_element_type=F32)


def _rope(t, cos, sin_signed):
    lane = lax.broadcasted_iota(jnp.int32, t.shape, 1)
    first_half = (lane & (HEAD_DIM // 2)) == 0
    width = t.shape[-1]
    partner = jnp.where(first_half,
                        pltpu.roll(t, width - HEAD_DIM // 2, axis=1),
                        pltpu.roll(t, HEAD_DIM // 2, axis=1))
    reps = width // LANES
    c = jnp.concatenate([cos] * reps, axis=-1)
    s = jnp.concatenate([sin_signed] * reps, axis=-1)
    return t * c + partner * s


def _in_proj_kernel(x_ref, g_ref, w_ref, cos_ref, sin_ref,
                    q_ref, k_ref, v_ref, z_ref, gate_ref):
    h = _rms(x_ref[...], g_ref[...]).astype(BF16)
    cos = cos_ref[...]
    sin = sin_ref[...]

    def proj(c0, n):
        return _dot(h, w_ref[:, c0:c0 + n])

    q = _rope(proj(0, ATTN_QK), cos, sin) * (HEAD_DIM ** -0.5)
    q_ref[...] = q.astype(BF16)
    k_ref[...] = _rope(proj(ATTN_QK, ATTN_QK), cos, sin).astype(BF16)
    v_ref[...] = proj(2 * ATTN_QK, ATTN_V).astype(BF16)
    u0 = 2 * ATTN_QK + ATTN_V
    a = proj(u0, CONV_CH)
    g = proj(u0 + CONV_CH, CONV_CH)
    z_ref[...] = a * jax.nn.sigmoid(g)
    g0 = u0 + 2 * CONV_CH
    step = 512
    for c in range(0, 2 * D_MODEL, step):
        gate_ref[:, c:c + step] = jax.nn.sigmoid(proj(g0 + c, step)).astype(BF16)


def _in_proj(x, g, w_bf, cos_t, sin_t, seq, tm=512):
    t = x.shape[0]
    n_in = w_bf.shape[1]
    seq_blocks = seq // tm
    row = lambda i: (i, 0)
    tab = lambda i: (i % seq_blocks, 0)
    return pl.pallas_call(
        _in_proj_kernel,
        grid=(t // tm,),
        in_specs=[
            pl.BlockSpec((tm, D_MODEL), row),
            _resident((1, D_MODEL)),
            _resident((D_MODEL, n_in)),
            pl.BlockSpec((tm, LANES), tab),
            pl.BlockSpec((tm, LANES), tab),
        ],
        out_specs=[
            pl.BlockSpec((tm, ATTN_QK), row),
            pl.BlockSpec((tm, ATTN_QK), row),
            pl.BlockSpec((tm, ATTN_V), row),
            pl.BlockSpec((tm, CONV_CH), row),
            pl.BlockSpec((tm, 2 * D_MODEL), row),
        ],
        out_shape=[
            jax.ShapeDtypeStruct((t, ATTN_QK), BF16),
            jax.ShapeDtypeStruct((t, ATTN_QK), BF16),
            jax.ShapeDtypeStruct((t, ATTN_V), BF16),
            jax.ShapeDtypeStruct((t, CONV_CH), F32),
            jax.ShapeDtypeStruct((t, 2 * D_MODEL), BF16),
        ],
        compiler_params=_cparams(1),
        name="in_proj",
    )(x, g, w_bf, cos_t, sin_t)


_ATTN_SUB = 128


def _attn_kernel(lam_ref, q_ref, k_ref, v_ref, g_ref, o_ref, v1_ref, *, out_scale):
    @pl.when(pl.program_id(2) == 0)
    def _():
        v1_ref[:, 0:V_DIM] = v_ref[...]
        v1_ref[:, V_DIM:2 * V_DIM] = jnp.ones(v_ref.shape, BF16)

    k = k_ref[...]
    v1 = v1_ref[...]
    contract_last = (((1,), (1,)), ((), ()))

    def softmax_pv(qc):
        s = lax.dot_general(qc, k, contract_last, preferred_element_type=F32)
        p = jnp.exp(s - jnp.max(s, axis=-1, keepdims=True)).astype(BF16)
        ov = _dot(p, v1)
        return ov[:, 0:V_DIM] / ov[:, V_DIM:V_DIM + 1]

    for r0 in range(0, q_ref.shape[0], _ATTN_SUB):
        q = q_ref[r0:r0 + _ATTN_SUB, :]
        lane = lax.broadcasted_iota(jnp.int32, q.shape, 1)
        zero = jnp.zeros_like(q)
        o0 = softmax_pv(jnp.where(lane < HEAD_DIM, q, zero))
        o1 = softmax_pv(jnp.where(lane >= HEAD_DIM, q, zero))
        o = o0 - lam_ref[0] * o1
        o_ref[r0:r0 + _ATTN_SUB, :] = (_rms(o, g_ref[...]) * out_scale).astype(BF16)


def _attention(q, k, v, lam_full, g_subln, batch, seq, out_scale):
    t = q.shape[0]
    tq = seq
    nq = seq // tq
    kernel = functools.partial(_attn_kernel, out_scale=out_scale)
    grid_spec = pltpu.PrefetchScalarGridSpec(
        num_scalar_prefetch=1,
        grid=(batch, N_HEADS, nq),
        in_specs=[
            pl.BlockSpec((tq, V_DIM), lambda b, h, i, lam: (b * nq + i, h)),
            pl.BlockSpec((seq, V_DIM), lambda b, h, i, lam: (b, h)),
            pl.BlockSpec((seq, V_DIM), lambda b, h, i, lam: (b, h)),
            pl.BlockSpec((1, V_DIM), lambda b, h, i, lam: (0, 0)),
        ],
        out_specs=pl.BlockSpec((tq, V_DIM), lambda b, h, i, lam: (b * nq + i, h)),
        scratch_shapes=[pltpu.VMEM((seq, 2 * V_DIM), BF16)],
    )
    return pl.pallas_call(
        kernel,
        grid_spec=grid_spec,
        out_shape=jax.ShapeDtypeStruct((t, ATTN_V), BF16),
        compiler_params=_cparams(3),
        name="diff_attn",
    )(lam_full, q, k, v, g_subln)


_CONV_HALO = 16
_CONV_ROWS = 128


def _conv_kernel(z_ref, w_ref, b_ref, lg_ref, lb_ref, o_ref, zp_ref):
    seq = z_ref.shape[0]
    halo = jnp.zeros((_CONV_HALO, CONV_CH), F32)
    zp_ref[0:_CONV_HALO, :] = halo
    zp_ref[_CONV_HALO + seq:_CONV_HALO + seq + _CONV_HALO, :] = halo
    zp_ref[_CONV_HALO:_CONV_HALO + seq, :] = z_ref[...]
    bias = b_ref[...]
    lg = lg_ref[...]
    lb = lb_ref[...]
    first = _CONV_HALO - CONV_PAD
    window = _CONV_ROWS + SUBLANES
    n_aligned = (CONV_K + first + SUBLANES - 1) // SUBLANES

    def conv_block(r0, c0):
        acc = None
        for r in reversed(range(SUBLANES)):
            y_r = None
            for a in range(n_aligned):
                tap = SUBLANES * a + r - first
                if 0 <= tap < CONV_K:
                    term = (zp_ref[pl.ds(r0 + SUBLANES * a, window), c0:c0 + LANES]
                            * w_ref[tap:tap + 1, c0:c0 + LANES])
                    y_r = term if y_r is None else y_r + term
            if acc is None:
                acc = y_r
            else:
                shifted = pltpu.roll(acc, window - 1, axis=0)
                acc = shifted if y_r is None else y_r + shifted
        return acc[0:_CONV_ROWS, :]

    def row_chunk(j, carry):
        r0 = pl.multiple_of(j * _CONV_ROWS, _CONV_ROWS)
        y = jnp.concatenate([conv_block(r0, c0) for c0 in range(0, CONV_CH, LANES)], axis=-1) + bias
        mu = jnp.mean(y, axis=-1, keepdims=True)
        yc = y - mu
        yn = yc * lax.rsqrt(jnp.mean(yc * yc, axis=-1, keepdims=True) + EPS) * lg + lb
        o_ref[pl.ds(r0, _CONV_ROWS), :] = (yn * jax.nn.sigmoid(yn)).astype(BF16)
        return carry

    lax.fori_loop(0, seq // _CONV_ROWS, row_chunk, 0)


def _conv_module(z, conv_w, conv_b, ln_g, ln_b, batch, seq):
    t = z.shape[0]
    return pl.pallas_call(
        _conv_kernel,
        grid=(batch,),
        in_specs=[
            pl.BlockSpec((seq, CONV_CH), lambda b: (b, 0)),
            _resident((CONV_K, CONV_CH)),
            _resident((1, CONV_CH)),
            _resident((1, CONV_CH)),
            _resident((1, CONV_CH)),
        ],
        out_specs=pl.BlockSpec((seq, CONV_CH), lambda b: (b, 0)),
        out_shape=jax.ShapeDtypeStruct((t, CONV_CH), BF16),
        scratch_shapes=[pltpu.VMEM((seq + 2 * _CONV_HALO, CONV_CH), F32)],
        compiler_params=_cparams(1),
        name="conv_module",
    )(z, conv_w, conv_b, ln_g, ln_b)


def _mix_kernel(x_ref, o_ref, c_ref, gate_ref, wa_ref, wc_ref, wo_ref, out_ref):
    attn_d = _dot(o_ref[...], wa_ref[...])
    conv_d = _dot(c_ref[...], wc_ref[...])
    ga = gate_ref[:, 0:D_MODEL].astype(F32)
    gc = gate_ref[:, D_MODEL:2 * D_MODEL].astype(F32)
    mix = (ga * attn_d + gc * conv_d).astype(BF16)
    out_ref[...] = x_ref[...] + _dot(mix, wo_ref[...])


def _mix(x, o, c, gates, wa, wc, wo, tm=512):
    t = x.shape[0]
    row = lambda i: (i, 0)
    return pl.pallas_call(
        _mix_kernel,
        grid=(t // tm,),
        in_specs=[
            pl.BlockSpec((tm, D_MODEL), row),
            pl.BlockSpec((tm, ATTN_V), row),
            pl.BlockSpec((tm, CONV_CH), row),
            pl.BlockSpec((tm, 2 * D_MODEL), row),
            _resident(wa.shape),
            _resident(wc.shape),
            _resident(wo.shape),
        ],
        out_specs=pl.BlockSpec((tm, D_MODEL), row),
        out_shape=jax.ShapeDtypeStruct((t, D_MODEL), F32),
        compiler_params=_cparams(1),
        name="branch_mix",
    )(x, o, c, gates, wa, wc, wo)


def _swiglu(h, wg_ref, wu_ref, wd_ref):
    g = _dot(h, wg_ref[...])
    u = _dot(h, wu_ref[...])
    a = (g * jax.nn.sigmoid(g) * u).astype(BF16)
    return _dot(a, wd_ref[...])


def _ffn_kernel(x_ref, g_ref, wg_ref, wu_ref, wd_ref, *rest, n_cast):
    cast_in, out_ref, cast_out = rest[:n_cast], rest[n_cast], rest[n_cast + 1:]
    x = x_ref[...]
    h = _rms(x, g_ref[...]).astype(BF16)
    out_ref[...] = x + _swiglu(h, wg_ref, wu_ref, wd_ref)
    for src, dst in zip(cast_in, cast_out):
        dst[...] = src[...].astype(BF16)


def _ffn(x, g, w_gu, w_down, to_cast=(), tm=256):
    t = x.shape[0]
    d_ff = w_down.shape[0]
    steps = t // tm
    row = lambda i: (i, 0)
    slabs = [pl.BlockSpec((w.shape[0] // steps, w.shape[1]), row) for w in to_cast]
    for w in to_cast:
        assert w.shape[0] % (steps * 2 * SUBLANES) == 0, w.shape
    outs = pl.pallas_call(
        functools.partial(_ffn_kernel, n_cast=len(to_cast)),
        grid=(steps,),
        in_specs=[
            pl.BlockSpec((tm, D_MODEL), row),
            _resident((1, D_MODEL)),
            pl.BlockSpec((D_MODEL, d_ff), lambda i: (0, 0), pipeline_mode=pl.Buffered(1)),
            pl.BlockSpec((D_MODEL, d_ff), lambda i: (0, 1), pipeline_mode=pl.Buffered(1)),
            _resident((d_ff, D_MODEL)),
            *slabs,
        ],
        out_specs=[pl.BlockSpec((tm, D_MODEL), row), *slabs],
        out_shape=[jax.ShapeDtypeStruct((t, D_MODEL), F32),
                   *[jax.ShapeDtypeStruct(w.shape, BF16) for w in to_cast]],
        compiler_params=_cparams(1),
        name="dense_ffn",
    )(x, g, w_gu, w_gu, w_down, *to_cast)
    return outs[0], outs[1:]


TM_EXPERT = 256
META_E1, META_E2, META_R1, META_R2, META_W1, META_W2 = range(6)


def _row_copy(src_ref, src_row, dst_ref, dst_row, sem):
    return pltpu.make_async_copy(src_ref.at[pl.ds(src_row, 1)], dst_ref.at[pl.ds(dst_row, 1)], sem)


def _router_kernel(x_ref, g_ref, wr_hi_ref, wr_lo_ref, br_ref, h_ref, meta_ref, cnt_ref, carry_ref):
    @pl.when(pl.program_id(0) == 0)
    def _():
        carry_ref[...] = jnp.zeros_like(carry_ref)

    hf = _rms(x_ref[...], g_ref[...])
    h_ref[...] = hf
    h_hi = hf.astype(BF16)
    h_lo = (hf - h_hi.astype(F32)).astype(BF16)
    logits = (_dot(h_hi, wr_hi_ref[...]) + _dot(h_lo, wr_hi_ref[...])
              + _dot(h_hi, wr_lo_ref[...])) + br_ref[...]
    tm = logits.shape[0]
    lane = lax.broadcasted_iota(jnp.int32, logits.shape, 1)
    neg = jnp.full_like(logits, -jnp.inf)
    logits = jnp.where(lane < N_EXPERTS, logits, neg)
    m1 = jnp.max(logits, axis=-1, keepdims=True)
    i1 = jnp.min(jnp.where(logits == m1, lane, LANES), axis=-1, keepdims=True)
    rest = jnp.where(lane == i1, neg, logits)
    m2 = jnp.max(rest, axis=-1, keepdims=True)
    i2 = jnp.min(jnp.where(rest == m2, lane, LANES), axis=-1, keepdims=True)
    e2 = jnp.exp(m2 - m1)
    denom = 1.0 + e2
    pick1 = lane == i1
    pick2 = lane == i2
    onehot = jnp.where(pick1 | pick2, 1.0, 0.0)
    r = lax.broadcasted_iota(jnp.int32, (tm, tm), 0)
    c = lax.broadcasted_iota(jnp.int32, (tm, tm), 1)
    strict_lower = jnp.where(r > c, 1.0, 0.0).astype(BF16)
    before = _dot(strict_lower, onehot.astype(BF16)) + carry_ref[...]
    rank1 = jnp.sum(jnp.where(pick1, before, 0.0), axis=-1, keepdims=True)
    rank2 = jnp.sum(jnp.where(pick2, before, 0.0), axis=-1, keepdims=True)
    record = jnp.zeros_like(logits)
    for slot, val in ((META_E1, i1.astype(F32)), (META_E2, i2.astype(F32)),
                      (META_R1, rank1), (META_R2, rank2),
                      (META_W1, 1.0 / denom), (META_W2, e2 / denom)):
        record = jnp.where(lane == slot, val, record)
    meta_ref[...] = record
    carry_ref[...] += jnp.sum(onehot, axis=0, keepdims=True)
    cnt_ref[...] = carry_ref[...]


def _router(x, g, wr_hi, wr_lo, br, tm=512):
    t = x.shape[0]
    row = lambda i: (i, 0)
    return pl.pallas_call(
        _router_kernel,
        grid=(t // tm,),
        in_specs=[
            pl.BlockSpec((tm, D_MODEL), row),
            _resident((1, D_MODEL)),
            _resident((D_MODEL, LANES)),
            _resident((D_MODEL, LANES)),
            _resident((1, LANES)),
        ],
        out_specs=[pl.BlockSpec((tm, D_MODEL), row), pl.BlockSpec((tm, LANES), row),
                   pl.BlockSpec((1, LANES), lambda i: (0, 0))],
        out_shape=[jax.ShapeDtypeStruct((t, D_MODEL), F32),
                   jax.ShapeDtypeStruct((t, LANES), F32),
                   jax.ShapeDtypeStruct((1, LANES), F32)],
        scratch_shapes=[pltpu.VMEM((1, LANES), F32)],
        compiler_params=_cparams(1),
        name="moe_router",
    )(x, g, wr_hi, wr_lo, br)


def _dispatch_kernel(ends_ref, sizes_ref, h_ref, p1_ref, p2_ref, xs_ref, zero_ref, row_sem, zero_sem):
    tm = h_ref.shape[0]

    @pl.when(pl.program_id(0) == 0)
    def _():
        zero_ref[...] = jnp.zeros_like(zero_ref)
        n_tiles = xs_ref.shape[0] // TM_EXPERT
        total = ends_ref[N_EXPERTS - 1]

        def zero_tile(row):
            start = pl.multiple_of(row, TM_EXPERT)
            return pltpu.make_async_copy(zero_ref, xs_ref.at[pl.ds(start, TM_EXPERT)], zero_sem)

        fills = [(sizes_ref[e] > 0, ends_ref[e] - TM_EXPERT) for e in range(N_EXPERTS)]
        fills += [(j * TM_EXPERT >= total, j * TM_EXPERT) for j in range(n_tiles - N_EXPERTS, n_tiles)]
        for needed, row in fills:
            @pl.when(needed)
            def _():
                zero_tile(row).start()
        for needed, row in fills:
            @pl.when(needed)
            def _():
                zero_tile(row).wait()

    def scatter_row(r, carry):
        _row_copy(h_ref, r, xs_ref, p1_ref[0, 0, r], row_sem).start()
        _row_copy(h_ref, r, xs_ref, p2_ref[0, 0, r], row_sem).start()
        return carry

    lax.fori_loop(0, tm, scatter_row, 0, unroll=8)
    for _ in range(2):
        pltpu.make_async_copy(h_ref, xs_ref.at[pl.ds(0, tm)], row_sem).wait()


def _dispatch(h, pos1, pos2, ends, sizes, n_rows, tm=512):
    t = h.shape[0]
    idx = lambda a: a.reshape(t // tm, 1, tm)
    smem_block = pl.BlockSpec((1, 1, tm), lambda i, *_: (i, 0, 0), memory_space=pltpu.SMEM)
    grid_spec = pltpu.PrefetchScalarGridSpec(
        num_scalar_prefetch=2,
        grid=(t // tm,),
        in_specs=[pl.BlockSpec((tm, D_MODEL), lambda i, *_: (i, 0)), smem_block, smem_block],
        out_specs=pl.BlockSpec(memory_space=pl.ANY),
        scratch_shapes=[pltpu.VMEM((TM_EXPERT, D_MODEL), F32),
                        pltpu.SemaphoreType.DMA(()), pltpu.SemaphoreType.DMA(())],
    )
    return pl.pallas_call(
        _dispatch_kernel,
        grid_spec=grid_spec,
        out_shape=jax.ShapeDtypeStruct((n_rows, D_MODEL), F32),
        compiler_params=_cparams(1),
        name="moe_dispatch",
    )(ends, sizes, h, idx(pos1), idx(pos2))


def _expert_kernel(tile_expert_ref, tile_src_ref, x_ref, wg_ref, wu_ref, wd_ref, y_ref):
    j = pl.program_id(0)

    @pl.when(tile_src_ref[j] != j)
    def _():
        y_ref[...] = jnp.zeros_like(y_ref)

    @pl.when(tile_src_ref[j] == j)
    def _():
        xb = x_ref[...].astype(BF16)
        half = wd_ref.shape[0] // 2
        y = None
        for c0 in (0, half):
            g = _dot(xb, wg_ref[:, c0:c0 + half])
            u = _dot(xb, wu_ref[:, c0:c0 + half])
            a = (g * jax.nn.sigmoid(g) * u).astype(BF16)
            part = _dot(a, wd_ref[c0:c0 + half, :])
            y = part if y is None else y + part
        y_ref[...] = y


def _experts(xs, tile_expert, tile_src, we_gu, we_down):
    n_rows = xs.shape[0]
    d_e = we_down.shape[1]
    tile = lambda j, te, ts: (ts[j], 0)
    grid_spec = pltpu.PrefetchScalarGridSpec(
        num_scalar_prefetch=2,
        grid=(n_rows // TM_EXPERT,),
        in_specs=[
            pl.BlockSpec((TM_EXPERT, D_MODEL), tile),
            pl.BlockSpec((None, D_MODEL, d_e), lambda j, te, ts: (te[j], 0, 0)),
            pl.BlockSpec((None, D_MODEL, d_e), lambda j, te, ts: (te[j], 0, 1)),
            pl.BlockSpec((None, d_e, D_MODEL), lambda j, te, ts: (te[j], 0, 0)),
        ],
        out_specs=pl.BlockSpec((TM_EXPERT, D_MODEL), lambda j, te, ts: (j, 0)),
    )
    return pl.pallas_call(
        _expert_kernel,
        grid_spec=grid_spec,
        out_shape=jax.ShapeDtypeStruct((n_rows, D_MODEL), F32),
        compiler_params=pltpu.CompilerParams(
            dimension_semantics=("arbitrary",), vmem_limit_bytes=VMEM_LIMIT_EXPERT),
        name="moe_experts",
    )(tile_expert, tile_src, xs, we_gu, we_gu, we_down)


def _ple_tail(x, p_ref, g_ref, wg_ref, wp_ref, gf_ref, out_ref, final_norm):
    h = _rms(x, g_ref[...]).astype(BF16)
    gate = jax.nn.sigmoid(_dot(h, wg_ref[...]))
    y = x + gate * _dot(p_ref[...].astype(BF16), wp_ref[...])
    if final_norm:
        y = _rms(y, gf_ref[...])
    out_ref[...] = y


def _ple_kernel(x_ref, p_ref, g_ref, wg_ref, wp_ref, gf_ref, out_ref, *, final_norm):
    _ple_tail(x_ref[...], p_ref, g_ref, wg_ref, wp_ref, gf_ref, out_ref, final_norm)


def _combine_ple_kernel(x_ref, meta_ref, p1_ref, p2_ref, p1_next_ref, p2_next_ref, y_ref,
                        p_ref, g_ref, wg_ref, wp_ref, gf_ref, out_ref, rows_ref, sems, *, final_norm):
    i = pl.program_id(0)
    n = pl.num_programs(0)
    tm = x_ref.shape[0]
    slot = i % 2

    def gather(p1, p2, s):
        def gather_row(r, carry):
            _row_copy(y_ref, p1[0, 0, r], rows_ref.at[s, 0], r, sems.at[s]).start()
            _row_copy(y_ref, p2[0, 0, r], rows_ref.at[s, 1], r, sems.at[s]).start()
            return carry
        lax.fori_loop(0, tm, gather_row, 0, unroll=8)

    @pl.when(i == 0)
    def _():
        gather(p1_ref, p2_ref, 0)

    @pl.when(i + 1 < n)
    def _():
        gather(p1_next_ref, p2_next_ref, 1 - slot)

    for k in range(2):
        pltpu.make_async_copy(y_ref.at[pl.ds(0, tm)], rows_ref.at[slot, k], sems.at[slot]).wait()
    meta = meta_ref[...]
    x = (x_ref[...] + meta[:, META_W1:META_W1 + 1] * rows_ref[slot, 0]
         + meta[:, META_W2:META_W2 + 1] * rows_ref[slot, 1])
    _ple_tail(x, p_ref, g_ref, wg_ref, wp_ref, gf_ref, out_ref, final_norm)


def _combine_ple(x, meta, pos1, pos2, y, p, g, w_gate, w_proj, g_final, final_norm, tm=256):
    t = x.shape[0]
    nb = t // tm
    ple_dim = p.shape[1]
    row = lambda i: (i, 0)
    idx = lambda a: a.reshape(nb, 1, tm)
    cur = pl.BlockSpec((1, 1, tm), lambda i: (i, 0, 0), memory_space=pltpu.SMEM)
    nxt = pl.BlockSpec((1, 1, tm), lambda i: (jnp.minimum(i + 1, nb - 1), 0, 0),
                       memory_space=pltpu.SMEM)
    return pl.pallas_call(
        functools.partial(_combine_ple_kernel, final_norm=final_norm),
        grid=(nb,),
        in_specs=[
            pl.BlockSpec((tm, D_MODEL), row),
            pl.BlockSpec((tm, LANES), row),
            cur, cur, nxt, nxt,
            pl.BlockSpec(memory_space=pl.ANY),
            pl.BlockSpec((tm, ple_dim), row),
            _resident((1, D_MODEL)),
            _resident(w_gate.shape),
            _resident(w_proj.shape),
            _resident((1, D_MODEL)),
        ],
        out_specs=pl.BlockSpec((tm, D_MODEL), row),
        out_shape=jax.ShapeDtypeStruct((t, D_MODEL), F32),
        scratch_shapes=[pltpu.VMEM((2, 2, tm, D_MODEL), F32), pltpu.SemaphoreType.DMA((2,))],
        compiler_params=_cparams(1),
        name="moe_combine_ple",
    )(x, meta, idx(pos1), idx(pos2), idx(pos1), idx(pos2), y, p, g, w_gate, w_proj, g_final)


def _moe_plan(meta, counts):
    cnt = counts[0, :N_EXPERTS].astype(jnp.int32)
    sizes = (cnt + TM_EXPERT - 1) // TM_EXPERT * TM_EXPERT
    ends = jnp.cumsum(sizes)
    starts = ends - sizes
    e1 = meta[:, META_E1].astype(jnp.int32)
    e2 = meta[:, META_E2].astype(jnp.int32)
    start_of = lambda e: jnp.sum(jnp.where(e[:, None] == jnp.arange(N_EXPERTS)[None, :],
                                           starts[None, :], 0), axis=1)
    pos1 = start_of(e1) + meta[:, META_R1].astype(jnp.int32)
    pos2 = start_of(e2) + meta[:, META_R2].astype(jnp.int32)
    n_rows = 2 * meta.shape[0] + N_EXPERTS * TM_EXPERT
    n_valid = ends[-1] // TM_EXPERT
    tile_src = jnp.minimum(jnp.arange(n_rows // TM_EXPERT, dtype=jnp.int32), n_valid - 1)
    tile_expert = jnp.sum((tile_src * TM_EXPERT)[:, None] >= ends[None, :], axis=1).astype(jnp.int32)
    return pos1, pos2, ends.astype(jnp.int32), sizes, tile_expert, tile_src, n_rows


def _ple(x, p, g, w_gate, w_proj, g_final, final_norm, tm=512):
    t = x.shape[0]
    ple_dim = p.shape[1]
    row = lambda i: (i, 0)
    return pl.pallas_call(
        functools.partial(_ple_kernel, final_norm=final_norm),
        grid=(t // tm,),
        in_specs=[
            pl.BlockSpec((tm, D_MODEL), row),
            pl.BlockSpec((tm, ple_dim), row),
            _resident((1, D_MODEL)),
            _resident(w_gate.shape),
            _resident(w_proj.shape),
            _resident((1, D_MODEL)),
        ],
        out_specs=pl.BlockSpec((tm, D_MODEL), row),
        out_shape=jax.ShapeDtypeStruct((t, D_MODEL), F32),
        compiler_params=_cparams(1),
        name="ple",
    )(x, p, g, w_gate, w_proj, g_final)


def _rope_tables(seq):
    inv_freq = ROPE_THETA ** (-jnp.arange(0, HEAD_DIM, 2, dtype=F32) / HEAD_DIM)
    ang = jnp.arange(seq, dtype=F32)[:, None] * inv_freq[None, :]
    cos, sin = jnp.cos(ang), jnp.sin(ang)
    cos_t = jnp.concatenate([cos] * (LANES // (HEAD_DIM // 2)), axis=-1)
    sin_t = jnp.concatenate([-sin, sin] * (LANES // HEAD_DIM), axis=-1)
    return cos_t, sin_t


def kernel(x, p, g_mix, w_in, lam, g_subln, w_attn_out, conv_w, conv_b, conv_ln_g, conv_ln_b, w_conv_out, w_o, g_ffn, w_ff_gu, w_ff_down, w_router, b_router, we_gu, we_down, g_ple, w_ple_gate, w_ple_proj, g_final):
    batch, seq, _ = x.shape
    depth = w_in.shape[0]
    t = batch * seq
    cos_t, sin_t = _rope_tables(seq)
    xs = x.reshape(t, D_MODEL)
    row2 = lambda a: a.reshape(1, -1)
    expert_bf16 = {}
    for i in range(depth):
        q, k, v, z, gates = _in_proj(xs, row2(g_mix[i]), w_in[i].astype(BF16), cos_t, sin_t, seq)
        lam_init = 0.8 - 0.6 * math.exp(-0.3 * i)
        lf = lam[i].astype(F32)
        lam_full = (jnp.exp(jnp.sum(lf[0] * lf[1])) - jnp.exp(jnp.sum(lf[2] * lf[3]))
                    + lam_init).reshape(1)
        o = _attention(q, k, v, lam_full, row2(g_subln[i]), batch, seq, 1.0 - lam_init)
        c = _conv_module(z, conv_w[i], row2(conv_b[i]), row2(conv_ln_g[i]), row2(conv_ln_b[i]),
                         batch, seq)
        xs = _mix(xs, o, c, gates, w_attn_out[i].astype(BF16), w_conv_out[i].astype(BF16),
                  w_o[i].astype(BF16))
        if i % 2 == 0:
            m_next = i // 2
            to_cast = ()
            if i + 1 < depth:
                to_cast = (we_gu[m_next].reshape(-1, we_gu.shape[-1]),
                           we_down[m_next].reshape(-1, we_down.shape[-1]))
            xs, cast = _ffn(xs, row2(g_ffn[i]), w_ff_gu[i // 2].astype(BF16),
                            w_ff_down[i // 2].astype(BF16), to_cast)
            if cast:
                expert_bf16[m_next] = (cast[0].reshape(we_gu.shape[1:]), cast[1].reshape(we_down.shape[1:]))
        else:
            m = i // 2
            wr = jnp.pad(w_router[m], ((0, 0), (0, LANES - N_EXPERTS)))
            wr_hi = wr.astype(BF16)
            wr_lo = (wr - wr_hi.astype(F32)).astype(BF16)
            br = jnp.pad(b_router[m], (0, LANES - N_EXPERTS)).reshape(1, LANES)
            h, meta, counts = _router(xs, row2(g_ffn[i]), wr_hi, wr_lo, br)
            pos1, pos2, ends, sizes, tile_expert, tile_src, n_rows = _moe_plan(meta, counts)
            routed = _dispatch(h, pos1, pos2, ends, sizes, n_rows)
            if m not in expert_bf16:
                expert_bf16[m] = (we_gu[m].astype(BF16), we_down[m].astype(BF16))
            y = _experts(routed, tile_expert, tile_src, *expert_bf16[m])
        ple_args = (p[i].reshape(t, -1), row2(g_ple[i]), w_ple_gate[i].astype(BF16),
                    w_ple_proj[i].astype(BF16), row2(g_final))
        final_norm = i == depth - 1
        if i % 2 == 0:
            xs = _ple(xs, *ple_args, final_norm=final_norm)
        else:
            xs = _combine_ple(xs, meta, pos1, pos2, y, *ple_args, final_norm=final_norm)
    return xs.reshape(batch, seq, D_MODEL)
```

```python
import functools
import math

import jax
import jax.numpy as jnp
from jax import lax
from jax.experimental import pallas as pl
from jax.experimental.pallas import tpu as pltpu

D_MODEL = 1024
N_HEADS = 4
HEAD_DIM = 64
V_DIM = 2 * HEAD_DIM
ATTN_QK = N_HEADS * 2 * HEAD_DIM
ATTN_V = N_HEADS * V_DIM
ROPE_THETA = 10000.0
CONV_CH = 512
CONV_K = 31
CONV_PAD = (CONV_K - 1) // 2
N_EXPERTS = 8
EPS = 1e-6

LANES = 128
SUBLANES = 8
VMEM_LIMIT = 56 * 1024 * 1024
VMEM_LIMIT_EXPERT = 60 * 1024 * 1024

F32 = jnp.float32
BF16 = jnp.bfloat16


def _cparams(n_axes):
    return pltpu.CompilerParams(
        dimension_semantics=("arbitrary",) * n_axes, vmem_limit_bytes=VMEM_LIMIT)


def _resident(shape):
    nd = len(shape)
    return pl.BlockSpec(shape, lambda *_: (0,) * nd, pipeline_mode=pl.Buffered(1))


def _rms(x, g):
    return x * lax.rsqrt(jnp.mean(x * x, axis=-1, keepdims=True) + EPS) * g


def _dot(a, b):
    return jnp.dot(a, b, preferred_element_type=F32)


def _rope(t, cos, sin_signed):
    lane = lax.broadcasted_iota(jnp.int32, t.shape, 1)
    first_half = (lane & (HEAD_DIM // 2)) == 0
    width = t.shape[-1]
    partner = jnp.where(first_half,
                        pltpu.roll(t, width - HEAD_DIM // 2, axis=1),
                        pltpu.roll(t, HEAD_DIM // 2, axis=1))
    reps = width // LANES
    c = jnp.concatenate([cos] * reps, axis=-1)
    s = jnp.concatenate([sin_signed] * reps, axis=-1)
    return t * c + partner * s


def _in_proj_kernel(x_ref, g_ref, w_ref, cos_ref, sin_ref,
                    q_ref, k_ref, v_ref, z_ref, gate_ref):
    h = _rms(x_ref[...], g_ref[...]).astype(BF16)
    cos = cos_ref[...]
    sin = sin_ref[...]

    def proj(c0, n):
        return _dot(h, w_ref[:, c0:c0 + n])

    q = _rope(proj(0, ATTN_QK), cos, sin) * (HEAD_DIM ** -0.5)
    q_ref[...] = q.astype(BF16)
    k_ref[...] = _rope(proj(ATTN_QK, ATTN_QK), cos, sin).astype(BF16)
    v_ref[...] = proj(2 * ATTN_QK, ATTN_V).astype(BF16)
    u0 = 2 * ATTN_QK + ATTN_V
    a = proj(u0, CONV_CH)
    g = proj(u0 + CONV_CH, CONV_CH)
    z_ref[...] = a * jax.nn.sigmoid(g)
    g0 = u0 + 2 * CONV_CH
    step = 512
    for c in range(0, 2 * D_MODEL, step):
        gate_ref[:, c:c + step] = jax.nn.sigmoid(proj(g0 + c, step)).astype(BF16)


def _in_proj(x, g, w_bf, cos_t, sin_t, seq, tm=512):
    t = x.shape[0]
    n_in = w_bf.shape[1]
    seq_blocks = seq // tm
    row = lambda i: (i, 0)
    tab = lambda i: (i % seq_blocks, 0)
    return pl.pallas_call(
        _in_proj_kernel,
        grid=(t // tm,),
        in_specs=[
            pl.BlockSpec((tm, D_MODEL), row),
            _resident((1, D_MODEL)),
            _resident((D_MODEL, n_in)),
            pl.BlockSpec((tm, LANES), tab),
            pl.BlockSpec((tm, LANES), tab),
        ],
        out_specs=[
            pl.BlockSpec((tm, ATTN_QK), row),
            pl.BlockSpec((tm, ATTN_QK), row),
            pl.BlockSpec((tm, ATTN_V), row),
            pl.BlockSpec((tm, CONV_CH), row),
            pl.BlockSpec((tm, 2 * D_MODEL), row),
        ],
        out_shape=[
            jax.ShapeDtypeStruct((t, ATTN_QK), BF16),
            jax.ShapeDtypeStruct((t, ATTN_QK), BF16),
            jax.ShapeDtypeStruct((t, ATTN_V), BF16),
            jax.ShapeDtypeStruct((t, CONV_CH), F32),
            jax.ShapeDtypeStruct((t, 2 * D_MODEL), BF16),
        ],
        compiler_params=_cparams(1),
        name="in_proj",
    )(x, g, w_bf, cos_t, sin_t)


_ATTN_SUB = 256


def _attn_kernel(lam_ref, q_ref, k_ref, v_ref, g_ref, o_ref, v1_ref, *, out_scale):
    @pl.when(pl.program_id(2) == 0)
    def _():
        v1_ref[:, 0:V_DIM] = v_ref[...]
        v1_ref[:, V_DIM:2 * V_DIM] = jnp.ones(v_ref.shape, BF16)

    k = k_ref[...]
    v1 = v1_ref[...]
    contract_last = (((1,), (1,)), ((), ()))

    def softmax_pv(qc):
        s = lax.dot_general(qc, k, contract_last, preferred_element_type=F32)
        p = jnp.exp(s - jnp.max(s, axis=-1, keepdims=True)).astype(BF16)
        ov = _dot(p, v1)
        return ov[:, 0:V_DIM] / ov[:, V_DIM:V_DIM + 1]

    for r0 in range(0, q_ref.shape[0], _ATTN_SUB):
        q = q_ref[r0:r0 + _ATTN_SUB, :]
        lane = lax.broadcasted_iota(jnp.int32, q.shape, 1)
        zero = jnp.zeros_like(q)
        o0 = softmax_pv(jnp.where(lane < HEAD_DIM, q, zero))
        o1 = softmax_pv(jnp.where(lane >= HEAD_DIM, q, zero))
        o = o0 - lam_ref[0] * o1
        o_ref[r0:r0 + _ATTN_SUB, :] = (_rms(o, g_ref[...]) * out_scale).astype(BF16)


def _attention(q, k, v, lam_full, g_subln, batch, seq, out_scale):
    t = q.shape[0]
    tq = seq
    nq = seq // tq
    kernel = functools.partial(_attn_kernel, out_scale=out_scale)
    grid_spec = pltpu.PrefetchScalarGridSpec(
        num_scalar_prefetch=1,
        grid=(batch, N_HEADS, nq),
        in_specs=[
            pl.BlockSpec((tq, V_DIM), lambda b, h, i, lam: (b * nq + i, h)),
            pl.BlockSpec((seq, V_DIM), lambda b, h, i, lam: (b, h)),
            pl.BlockSpec((seq, V_DIM), lambda b, h, i, lam: (b, h)),
            pl.BlockSpec((1, V_DIM), lambda b, h, i, lam: (0, 0)),
        ],
        out_specs=pl.BlockSpec((tq, V_DIM), lambda b, h, i, lam: (b * nq + i, h)),
        scratch_shapes=[pltpu.VMEM((seq, 2 * V_DIM), BF16)],
    )
    return pl.pallas_call(
        kernel,
        grid_spec=grid_spec,
        out_shape=jax.ShapeDtypeStruct((t, ATTN_V), BF16),
        compiler_params=_cparams(3),
        name="diff_attn",
    )(lam_full, q, k, v, g_subln)


_CONV_HALO = 16
_CONV_ROWS = 128


def _conv_kernel(z_ref, w_ref, b_ref, lg_ref, lb_ref, o_ref, zp_ref):
    seq = z_ref.shape[0]
    halo = jnp.zeros((_CONV_HALO, CONV_CH), F32)
    zp_ref[0:_CONV_HALO, :] = halo
    zp_ref[_CONV_HALO + seq:_CONV_HALO + seq + _CONV_HALO, :] = halo
    zp_ref[_CONV_HALO:_CONV_HALO + seq, :] = z_ref[...]
    bias = b_ref[...]
    lg = lg_ref[...]
    lb = lb_ref[...]
    first = _CONV_HALO - CONV_PAD
    window = _CONV_ROWS + SUBLANES
    n_aligned = (CONV_K + first + SUBLANES - 1) // SUBLANES

    def conv_block(r0, c0):
        acc = None
        for r in reversed(range(SUBLANES)):
            y_r = None
            for a in range(n_aligned):
                tap = SUBLANES * a + r - first
                if 0 <= tap < CONV_K:
                    term = (zp_ref[pl.ds(r0 + SUBLANES * a, window), c0:c0 + LANES]
                            * w_ref[tap:tap + 1, c0:c0 + LANES])
                    y_r = term if y_r is None else y_r + term
            if acc is None:
                acc = y_r
            else:
                shifted = pltpu.roll(acc, window - 1, axis=0)
                acc = shifted if y_r is None else y_r + shifted
        return acc[0:_CONV_ROWS, :]

    def row_chunk(j, carry):
        r0 = pl.multiple_of(j * _CONV_ROWS, _CONV_ROWS)
        y = jnp.concatenate([conv_block(r0, c0) for c0 in range(0, CONV_CH, LANES)], axis=-1) + bias
        mu = jnp.mean(y, axis=-1, keepdims=True)
        yc = y - mu
        yn = yc * lax.rsqrt(jnp.mean(yc * yc, axis=-1, keepdims=True) + EPS) * lg + lb
        o_ref[pl.ds(r0, _CONV_ROWS), :] = (yn * jax.nn.sigmoid(yn)).astype(BF16)
        return carry

    lax.fori_loop(0, seq // _CONV_ROWS, row_chunk, 0)


def _conv_module(z, conv_w, conv_b, ln_g, ln_b, batch, seq):
    t = z.shape[0]
    return pl.pallas_call(
        _conv_kernel,
        grid=(batch,),
        in_specs=[
            pl.BlockSpec((seq, CONV_CH), lambda b: (b, 0)),
            _resident((CONV_K, CONV_CH)),
            _resident((1, CONV_CH)),
            _resident((1, CONV_CH)),
            _resident((1, CONV_CH)),
        ],
        out_specs=pl.BlockSpec((seq, CONV_CH), lambda b: (b, 0)),
        out_shape=jax.ShapeDtypeStruct((t, CONV_CH), BF16),
        scratch_shapes=[pltpu.VMEM((seq + 2 * _CONV_HALO, CONV_CH), F32)],
        compiler_params=_cparams(1),
        name="conv_module",
    )(z, conv_w, conv_b, ln_g, ln_b)


def _mixed_residual(x_ref, o_ref, c_ref, gate_ref, wa_ref, wc_ref, wo_ref):
    attn_d = _dot(o_ref[...], wa_ref[...])
    conv_d = _dot(c_ref[...], wc_ref[...])
    ga = gate_ref[:, 0:D_MODEL].astype(F32)
    gc = gate_ref[:, D_MODEL:2 * D_MODEL].astype(F32)
    mix = (ga * attn_d + gc * conv_d).astype(BF16)
    return x_ref[...] + _dot(mix, wo_ref[...])


def _mix_kernel(x_ref, o_ref, c_ref, gate_ref, wa_ref, wc_ref, wo_ref, out_ref):
    out_ref[...] = _mixed_residual(x_ref, o_ref, c_ref, gate_ref, wa_ref, wc_ref, wo_ref)


def _mix_route_kernel(x_ref, o_ref, c_ref, gate_ref, wa_ref, wc_ref, wo_ref,
                      g_ref, wr_hi_ref, wr_lo_ref, br_ref, out_ref, meta_ref, cnt_ref, carry_ref):
    x1 = _mixed_residual(x_ref, o_ref, c_ref, gate_ref, wa_ref, wc_ref, wo_ref)
    out_ref[...] = x1
    _route(_rms(x1, g_ref[...]), wr_hi_ref, wr_lo_ref, br_ref, meta_ref, cnt_ref, carry_ref)


def _mix(x, o, c, gates, wa, wc, wo, router=None, tm=512):
    t = x.shape[0]
    row = lambda i: (i, 0)
    in_specs = [
        pl.BlockSpec((tm, D_MODEL), row),
        pl.BlockSpec((tm, ATTN_V), row),
        pl.BlockSpec((tm, CONV_CH), row),
        pl.BlockSpec((tm, 2 * D_MODEL), row),
        _resident(wa.shape),
        _resident(wc.shape),
        _resident(wo.shape),
    ]
    x_spec = pl.BlockSpec((tm, D_MODEL), row)
    x_shape = jax.ShapeDtypeStruct((t, D_MODEL), F32)
    if router is None:
        return pl.pallas_call(
            _mix_kernel, grid=(t // tm,), in_specs=in_specs, out_specs=x_spec, out_shape=x_shape,
            compiler_params=_cparams(1), name="branch_mix",
        )(x, o, c, gates, wa, wc, wo)
    return pl.pallas_call(
        _mix_route_kernel,
        grid=(t // tm,),
        in_specs=in_specs + [_resident(a.shape) for a in router],
        out_specs=[x_spec, pl.BlockSpec((tm, LANES), row), pl.BlockSpec((1, LANES), lambda i: (0, 0))],
        out_shape=[x_shape, jax.ShapeDtypeStruct((t, LANES), F32), jax.ShapeDtypeStruct((1, LANES), F32)],
        scratch_shapes=[pltpu.VMEM((1, LANES), F32)],
        compiler_params=_cparams(1),
        name="branch_mix_route",
    )(x, o, c, gates, wa, wc, wo, *router)


def _swiglu(h, wg_ref, wu_ref, wd_ref):
    g = _dot(h, wg_ref[...])
    u = _dot(h, wu_ref[...])
    a = (g * jax.nn.sigmoid(g) * u).astype(BF16)
    return _dot(a, wd_ref[...])


def _ffn_kernel(x_ref, g_ref, wg_ref, wu_ref, wd_ref, p_ref, gp_ref, wpg_ref, wpp_ref, gf_ref, *rest,
                n_cast, final_norm):
    cast_in, out_ref, cast_out = rest[:n_cast], rest[n_cast], rest[n_cast + 1:]
    x = x_ref[...]
    h = _rms(x, g_ref[...]).astype(BF16)
    x = x + _swiglu(h, wg_ref, wu_ref, wd_ref)
    _ple_tail(x, p_ref, gp_ref, wpg_ref, wpp_ref, gf_ref, out_ref, final_norm)
    for src, dst in zip(cast_in, cast_out):
        dst[...] = src[...].astype(BF16)


def _ffn_ple(x, g, w_gu, w_down, p, g_ple, w_gate, w_proj, g_final, final_norm, to_cast=(), tm=256):
    t = x.shape[0]
    d_ff = w_down.shape[0]
    steps = t // tm
    row = lambda i: (i, 0)
    slabs = [pl.BlockSpec((w.shape[0] // steps, w.shape[1]), row) for w in to_cast]
    for w in to_cast:
        assert w.shape[0] % (steps * 2 * SUBLANES) == 0, w.shape
    outs = pl.pallas_call(
        functools.partial(_ffn_kernel, n_cast=len(to_cast), final_norm=final_norm),
        grid=(steps,),
        in_specs=[
            pl.BlockSpec((tm, D_MODEL), row),
            _resident((1, D_MODEL)),
            pl.BlockSpec((D_MODEL, d_ff), lambda i: (0, 0), pipeline_mode=pl.Buffered(1)),
            pl.BlockSpec((D_MODEL, d_ff), lambda i: (0, 1), pipeline_mode=pl.Buffered(1)),
            _resident((d_ff, D_MODEL)),
            pl.BlockSpec((tm, p.shape[1]), row),
            _resident((1, D_MODEL)),
            _resident(w_gate.shape),
            _resident(w_proj.shape),
            _resident((1, D_MODEL)),
            *slabs,
        ],
        out_specs=[pl.BlockSpec((tm, D_MODEL), row), *slabs],
        out_shape=[jax.ShapeDtypeStruct((t, D_MODEL), F32),
                   *[jax.ShapeDtypeStruct(w.shape, BF16) for w in to_cast]],
        compiler_params=_cparams(1),
        name="dense_ffn_ple",
    )(x, g, w_gu, w_gu, w_down, p, g_ple, w_gate, w_proj, g_final, *to_cast)
    return outs[0], outs[1:]


TM_EXPERT = 256
META_E1, META_E2, META_R1, META_R2, META_W1, META_W2 = range(6)


def _row_copy(src_ref, src_row, dst_ref, dst_row, sem):
    return pltpu.make_async_copy(src_ref.at[pl.ds(src_row, 1)], dst_ref.at[pl.ds(dst_row, 1)], sem)


def _route(hf, wr_hi_ref, wr_lo_ref, br_ref, meta_ref, cnt_ref, carry_ref):
    @pl.when(pl.program_id(0) == 0)
    def _():
        carry_ref[...] = jnp.zeros_like(carry_ref)

    h_hi = hf.astype(BF16)
    h_lo = (hf - h_hi.astype(F32)).astype(BF16)
    logits = (_dot(h_hi, wr_hi_ref[...]) + _dot(h_lo, wr_hi_ref[...])
              + _dot(h_hi, wr_lo_ref[...])) + br_ref[...]
    tm = logits.shape[0]
    lane = lax.broadcasted_iota(jnp.int32, logits.shape, 1)
    neg = jnp.full_like(logits, -jnp.inf)
    logits = jnp.where(lane < N_EXPERTS, logits, neg)
    m1 = jnp.max(logits, axis=-1, keepdims=True)
    i1 = jnp.min(jnp.where(logits == m1, lane, LANES), axis=-1, keepdims=True)
    rest = jnp.where(lane == i1, neg, logits)
    m2 = jnp.max(rest, axis=-1, keepdims=True)
    i2 = jnp.min(jnp.where(rest == m2, lane, LANES), axis=-1, keepdims=True)
    e2 = jnp.exp(m2 - m1)
    denom = 1.0 + e2
    pick1 = lane == i1
    pick2 = lane == i2
    onehot = jnp.where(pick1 | pick2, 1.0, 0.0)
    r = lax.broadcasted_iota(jnp.int32, (tm, tm), 0)
    c = lax.broadcasted_iota(jnp.int32, (tm, tm), 1)
    strict_lower = jnp.where(r > c, 1.0, 0.0).astype(BF16)
    before = _dot(strict_lower, onehot.astype(BF16)) + carry_ref[...]
    rank1 = jnp.sum(jnp.where(pick1, before, 0.0), axis=-1, keepdims=True)
    rank2 = jnp.sum(jnp.where(pick2, before, 0.0), axis=-1, keepdims=True)
    record = jnp.zeros_like(logits)
    for slot, val in ((META_E1, i1.astype(F32)), (META_E2, i2.astype(F32)),
                      (META_R1, rank1), (META_R2, rank2),
                      (META_W1, 1.0 / denom), (META_W2, e2 / denom)):
        record = jnp.where(lane == slot, val, record)
    meta_ref[...] = record
    carry_ref[...] += jnp.sum(onehot, axis=0, keepdims=True)
    cnt_ref[...] = carry_ref[...]


def _dispatch_kernel(ends_ref, sizes_ref, h_ref, p1_ref, p2_ref, xs_ref, zero_ref, row_sem, zero_sem):
    tm = h_ref.shape[0]

    @pl.when(pl.program_id(0) == 0)
    def _():
        zero_ref[...] = jnp.zeros_like(zero_ref)
        n_tiles = xs_ref.shape[0] // TM_EXPERT
        total = ends_ref[N_EXPERTS - 1]

        def zero_tile(row):
            start = pl.multiple_of(row, TM_EXPERT)
            return pltpu.make_async_copy(zero_ref, xs_ref.at[pl.ds(start, TM_EXPERT)], zero_sem)

        fills = [(sizes_ref[e] > 0, ends_ref[e] - TM_EXPERT) for e in range(N_EXPERTS)]
        fills += [(j * TM_EXPERT >= total, j * TM_EXPERT) for j in range(n_tiles - N_EXPERTS, n_tiles)]
        for needed, row in fills:
            @pl.when(needed)
            def _():
                zero_tile(row).start()
        for needed, row in fills:
            @pl.when(needed)
            def _():
                zero_tile(row).wait()

    def scatter_row(r, carry):
        _row_copy(h_ref, r, xs_ref, p1_ref[0, 0, r], row_sem).start(priority=0)
        _row_copy(h_ref, r, xs_ref, p2_ref[0, 0, r], row_sem).start(priority=1)
        return carry

    lax.fori_loop(0, tm, scatter_row, 0, unroll=8)
    for _ in range(2):
        pltpu.make_async_copy(h_ref, xs_ref.at[pl.ds(0, tm)], row_sem).wait()


def _dispatch(h, pos1, pos2, ends, sizes, n_rows, tm=512):
    t = h.shape[0]
    idx = lambda a: a.reshape(t // tm, 1, tm)
    smem_block = pl.BlockSpec((1, 1, tm), lambda i, *_: (i, 0, 0), memory_space=pltpu.SMEM)
    grid_spec = pltpu.PrefetchScalarGridSpec(
        num_scalar_prefetch=2,
        grid=(t // tm,),
        in_specs=[pl.BlockSpec((tm, D_MODEL), lambda i, *_: (i, 0)), smem_block, smem_block],
        out_specs=pl.BlockSpec(memory_space=pl.ANY),
        scratch_shapes=[pltpu.VMEM((TM_EXPERT, D_MODEL), F32),
                        pltpu.SemaphoreType.DMA(()), pltpu.SemaphoreType.DMA(())],
    )
    return pl.pallas_call(
        _dispatch_kernel,
        grid_spec=grid_spec,
        out_shape=jax.ShapeDtypeStruct((n_rows, D_MODEL), F32),
        compiler_params=_cparams(1),
        name="moe_dispatch",
    )(ends, sizes, h, idx(pos1), idx(pos2))


def _expert_kernel(tile_expert_ref, tile_src_ref, x_ref, g_ref, wg_ref, wu_ref, wd_ref, y_ref):
    j = pl.program_id(0)

    @pl.when(tile_src_ref[j] != j)
    def _():
        y_ref[...] = jnp.zeros_like(y_ref)

    @pl.when(tile_src_ref[j] == j)
    def _():
        xb = _rms(x_ref[...], g_ref[...]).astype(BF16)
        half = wd_ref.shape[0] // 2
        y = None
        for c0 in (0, half):
            g = _dot(xb, wg_ref[:, c0:c0 + half])
            u = _dot(xb, wu_ref[:, c0:c0 + half])
            a = (g * jax.nn.sigmoid(g) * u).astype(BF16)
            part = _dot(a, wd_ref[c0:c0 + half, :])
            y = part if y is None else y + part
        y_ref[...] = y


def _experts(xs, g, tile_expert, tile_src, we_gu, we_down):
    n_rows = xs.shape[0]
    d_e = we_down.shape[1]
    tile = lambda j, te, ts: (ts[j], 0)
    grid_spec = pltpu.PrefetchScalarGridSpec(
        num_scalar_prefetch=2,
        grid=(n_rows // TM_EXPERT,),
        in_specs=[
            pl.BlockSpec((TM_EXPERT, D_MODEL), tile),
            pl.BlockSpec((1, D_MODEL), lambda j, te, ts: (0, 0)),
            pl.BlockSpec((None, D_MODEL, d_e), lambda j, te, ts: (te[j], 0, 0)),
            pl.BlockSpec((None, D_MODEL, d_e), lambda j, te, ts: (te[j], 0, 1)),
            pl.BlockSpec((None, d_e, D_MODEL), lambda j, te, ts: (te[j], 0, 0)),
        ],
        out_specs=pl.BlockSpec((TM_EXPERT, D_MODEL), lambda j, te, ts: (j, 0)),
    )
    return pl.pallas_call(
        _expert_kernel,
        grid_spec=grid_spec,
        out_shape=jax.ShapeDtypeStruct((n_rows, D_MODEL), F32),
        compiler_params=pltpu.CompilerParams(
            dimension_semantics=("arbitrary",), vmem_limit_bytes=VMEM_LIMIT_EXPERT),
        name="moe_experts",
    )(tile_expert, tile_src, xs, g, we_gu, we_gu, we_down)


def _ple_tail(x, p_ref, g_ref, wg_ref, wp_ref, gf_ref, out_ref, final_norm):
    h = _rms(x, g_ref[...]).astype(BF16)
    gate = jax.nn.sigmoid(_dot(h, wg_ref[...]))
    y = x + gate * _dot(p_ref[...].astype(BF16), wp_ref[...])
    if final_norm:
        y = _rms(y, gf_ref[...])
    out_ref[...] = y


def _combine_ple_kernel(x_ref, meta_ref, p1_ref, p2_ref, p1_next_ref, p2_next_ref, y_ref,
                        p_ref, g_ref, wg_ref, wp_ref, gf_ref, out_ref, rows_ref, sems, *, final_norm):
    i = pl.program_id(0)
    n = pl.num_programs(0)
    tm = x_ref.shape[0]
    slot = i % 2

    def gather(p1, p2, s):
        def gather_row(r, carry):
            _row_copy(y_ref, p1[0, 0, r], rows_ref.at[s, 0], r, sems.at[s]).start(priority=0)
            _row_copy(y_ref, p2[0, 0, r], rows_ref.at[s, 1], r, sems.at[s]).start(priority=1)
            return carry
        lax.fori_loop(0, tm, gather_row, 0, unroll=8)

    @pl.when(i == 0)
    def _():
        gather(p1_ref, p2_ref, 0)

    @pl.when(i + 1 < n)
    def _():
        gather(p1_next_ref, p2_next_ref, 1 - slot)

    for k in range(2):
        pltpu.make_async_copy(y_ref.at[pl.ds(0, tm)], rows_ref.at[slot, k], sems.at[slot]).wait()
    meta = meta_ref[...]
    x = (x_ref[...] + meta[:, META_W1:META_W1 + 1] * rows_ref[slot, 0]
         + meta[:, META_W2:META_W2 + 1] * rows_ref[slot, 1])
    _ple_tail(x, p_ref, g_ref, wg_ref, wp_ref, gf_ref, out_ref, final_norm)


def _combine_ple(x, meta, pos1, pos2, y, p, g, w_gate, w_proj, g_final, final_norm, tm=256):
    t = x.shape[0]
    nb = t // tm
    ple_dim = p.shape[1]
    row = lambda i: (i, 0)
    idx = lambda a: a.reshape(nb, 1, tm)
    cur = pl.BlockSpec((1, 1, tm), lambda i: (i, 0, 0), memory_space=pltpu.SMEM)
    nxt = pl.BlockSpec((1, 1, tm), lambda i: (jnp.minimum(i + 1, nb - 1), 0, 0),
                       memory_space=pltpu.SMEM)
    return pl.pallas_call(
        functools.partial(_combine_ple_kernel, final_norm=final_norm),
        grid=(nb,),
        in_specs=[
            pl.BlockSpec((tm, D_MODEL), row),
            pl.BlockSpec((tm, LANES), row),
            cur, cur, nxt, nxt,
            pl.BlockSpec(memory_space=pl.ANY),
            pl.BlockSpec((tm, ple_dim), row),
            _resident((1, D_MODEL)),
            _resident(w_gate.shape),
            _resident(w_proj.shape),
            _resident((1, D_MODEL)),
        ],
        out_specs=pl.BlockSpec((tm, D_MODEL), row),
        out_shape=jax.ShapeDtypeStruct((t, D_MODEL), F32),
        scratch_shapes=[pltpu.VMEM((2, 2, tm, D_MODEL), F32), pltpu.SemaphoreType.DMA((2,))],
        compiler_params=_cparams(1),
        name="moe_combine_ple",
    )(x, meta, idx(pos1), idx(pos2), idx(pos1), idx(pos2), y, p, g, w_gate, w_proj, g_final)


def _moe_plan(meta, counts):
    cnt = counts[0, :N_EXPERTS].astype(jnp.int32)
    sizes = (cnt + TM_EXPERT - 1) // TM_EXPERT * TM_EXPERT
    ends = jnp.cumsum(sizes)
    starts = ends - sizes
    e1 = meta[:, META_E1].astype(jnp.int32)
    e2 = meta[:, META_E2].astype(jnp.int32)
    start_of = lambda e: jnp.sum(jnp.where(e[:, None] == jnp.arange(N_EXPERTS)[None, :],
                                           starts[None, :], 0), axis=1)
    pos1 = start_of(e1) + meta[:, META_R1].astype(jnp.int32)
    pos2 = start_of(e2) + meta[:, META_R2].astype(jnp.int32)
    n_rows = 2 * meta.shape[0] + N_EXPERTS * TM_EXPERT
    n_valid = ends[-1] // TM_EXPERT
    tile_src = jnp.minimum(jnp.arange(n_rows // TM_EXPERT, dtype=jnp.int32), n_valid - 1)
    tile_expert = jnp.sum((tile_src * TM_EXPERT)[:, None] >= ends[None, :], axis=1).astype(jnp.int32)
    return pos1, pos2, ends.astype(jnp.int32), sizes, tile_expert, tile_src, n_rows


def _rope_tables(seq):
    inv_freq = ROPE_THETA ** (-jnp.arange(0, HEAD_DIM, 2, dtype=F32) / HEAD_DIM)
    ang = jnp.arange(seq, dtype=F32)[:, None] * inv_freq[None, :]
    cos, sin = jnp.cos(ang), jnp.sin(ang)
    cos_t = jnp.concatenate([cos] * (LANES // (HEAD_DIM // 2)), axis=-1)
    sin_t = jnp.concatenate([-sin, sin] * (LANES // HEAD_DIM), axis=-1)
    return cos_t, sin_t


def kernel(x, p, g_mix, w_in, lam, g_subln, w_attn_out, conv_w, conv_b, conv_ln_g, conv_ln_b, w_conv_out, w_o, g_ffn, w_ff_gu, w_ff_down, w_router, b_router, we_gu, we_down, g_ple, w_ple_gate, w_ple_proj, g_final):
    batch, seq, _ = x.shape
    depth = w_in.shape[0]
    t = batch * seq
    cos_t, sin_t = _rope_tables(seq)
    xs = x.reshape(t, D_MODEL)
    row2 = lambda a: a.reshape(1, -1)
    expert_bf16 = {}
    for i in range(depth):
        q, k, v, z, gates = _in_proj(xs, row2(g_mix[i]), w_in[i].astype(BF16), cos_t, sin_t, seq)
        lam_init = 0.8 - 0.6 * math.exp(-0.3 * i)
        lf = lam[i].astype(F32)
        lam_full = (jnp.exp(jnp.sum(lf[0] * lf[1])) - jnp.exp(jnp.sum(lf[2] * lf[3]))
                    + lam_init).reshape(1)
        o = _attention(q, k, v, lam_full, row2(g_subln[i]), batch, seq, 1.0 - lam_init)
        c = _conv_module(z, conv_w[i], row2(conv_b[i]), row2(conv_ln_g[i]), row2(conv_ln_b[i]),
                         batch, seq)
        mix_args = (xs, o, c, gates, w_attn_out[i].astype(BF16), w_conv_out[i].astype(BF16),
                    w_o[i].astype(BF16))
        ple_args = (p[i].reshape(t, -1), row2(g_ple[i]), w_ple_gate[i].astype(BF16),
                    w_ple_proj[i].astype(BF16), row2(g_final))
        final_norm = i == depth - 1
        if i % 2 == 0:
            xs = _mix(*mix_args)
            m_next = i // 2
            to_cast = ()
            if i + 1 < depth:
                to_cast = (we_gu[m_next].reshape(-1, we_gu.shape[-1]),
                           we_down[m_next].reshape(-1, we_down.shape[-1]))
            xs, cast = _ffn_ple(xs, row2(g_ffn[i]), w_ff_gu[i // 2].astype(BF16),
                                w_ff_down[i // 2].astype(BF16), *ple_args, final_norm, to_cast)
            if cast:
                expert_bf16[m_next] = (cast[0].reshape(we_gu.shape[1:]), cast[1].reshape(we_down.shape[1:]))
        else:
            m = i // 2
            wr = jnp.pad(w_router[m], ((0, 0), (0, LANES - N_EXPERTS)))
            wr_hi = wr.astype(BF16)
            wr_lo = (wr - wr_hi.astype(F32)).astype(BF16)
            br = jnp.pad(b_router[m], (0, LANES - N_EXPERTS)).reshape(1, LANES)
            xs, meta, counts = _mix(*mix_args, router=(row2(g_ffn[i]), wr_hi, wr_lo, br))
            pos1, pos2, ends, sizes, tile_expert, tile_src, n_rows = _moe_plan(meta, counts)
            routed = _dispatch(xs, pos1, pos2, ends, sizes, n_rows)
            if m not in expert_bf16:
                expert_bf16[m] = (we_gu[m].astype(BF16), we_down[m].astype(BF16))
            y = _experts(routed, row2(g_ffn[i]), tile_expert, tile_src, *expert_bf16[m])
            xs = _combine_ple(xs, meta, pos1, pos2, y, *ple_args, final_norm=final_norm)
    return xs.reshape(batch, seq, D_MODEL)
```

```python
import functools
import math

import jax
import jax.numpy as jnp
from jax import lax
from jax.experimental import pallas as pl
from jax.experimental.pallas import tpu as pltpu

D_MODEL = 1024
N_HEADS = 4
HEAD_DIM = 64
V_DIM = 2 * HEAD_DIM
ATTN_QK = N_HEADS * 2 * HEAD_DIM
ATTN_V = N_HEADS * V_DIM
ROPE_THETA = 10000.0
CONV_CH = 512
CONV_K = 31
CONV_PAD = (CONV_K - 1) // 2
N_EXPERTS = 8
EPS = 1e-6

LANES = 128
SUBLANES = 8
VMEM_LIMIT = 56 * 1024 * 1024
VMEM_LIMIT_EXPERT = 60 * 1024 * 1024

F32 = jnp.float32
BF16 = jnp.bfloat16


def _cparams(n_axes):
    return pltpu.CompilerParams(
        dimension_semantics=("arbitrary",) * n_axes, vmem_limit_bytes=VMEM_LIMIT)


def _resident(shape):
    nd = len(shape)
    return pl.BlockSpec(shape, lambda *_: (0,) * nd, pipeline_mode=pl.Buffered(1))


def _rms(x, g):
    return x * lax.rsqrt(jnp.mean(x * x, axis=-1, keepdims=True) + EPS) * g


def _dot(a, b):
    return jnp.dot(a, b, preferred_element_type=F32)


def _rope(t, cos, sin_signed):
    lane = lax.broadcasted_iota(jnp.int32, t.shape, 1)
    first_half = (lane & (HEAD_DIM // 2)) == 0
    width = t.shape[-1]
    partner = jnp.where(first_half,
                        pltpu.roll(t, width - HEAD_DIM // 2, axis=1),
                        pltpu.roll(t, HEAD_DIM // 2, axis=1))
    reps = width // LANES
    c = jnp.concatenate([cos] * reps, axis=-1)
    s = jnp.concatenate([sin_signed] * reps, axis=-1)
    return t * c + partner * s


def _in_proj_kernel(x_ref, g_ref, w_ref, cos_ref, sin_ref,
                    q_ref, k_ref, v_ref, z_ref, gate_ref):
    h = _rms(x_ref[...], g_ref[...]).astype(BF16)
    cos = cos_ref[...]
    sin = sin_ref[...]

    def proj(c0, n):
        return _dot(h, w_ref[:, c0:c0 + n])

    q = _rope(proj(0, ATTN_QK), cos, sin) * (HEAD_DIM ** -0.5)
    q_ref[...] = q.astype(BF16)
    k_ref[...] = _rope(proj(ATTN_QK, ATTN_QK), cos, sin).astype(BF16)
    v_ref[...] = proj(2 * ATTN_QK, ATTN_V).astype(BF16)
    u0 = 2 * ATTN_QK + ATTN_V
    a = proj(u0, CONV_CH)
    g = proj(u0 + CONV_CH, CONV_CH)
    z_ref[...] = a * jax.nn.sigmoid(g)
    g0 = u0 + 2 * CONV_CH
    step = 512
    for c in range(0, 2 * D_MODEL, step):
        gate_ref[:, c:c + step] = jax.nn.sigmoid(proj(g0 + c, step)).astype(BF16)


def _in_proj(x, g, w_bf, cos_t, sin_t, seq, tm=512):
    t = x.shape[0]
    n_in = w_bf.shape[1]
    seq_blocks = seq // tm
    row = lambda i: (i, 0)
    tab = lambda i: (i % seq_blocks, 0)
    return pl.pallas_call(
        _in_proj_kernel,
        grid=(t // tm,),
        in_specs=[
            pl.BlockSpec((tm, D_MODEL), row),
            _resident((1, D_MODEL)),
            _resident((D_MODEL, n_in)),
            pl.BlockSpec((tm, LANES), tab),
            pl.BlockSpec((tm, LANES), tab),
        ],
        out_specs=[
            pl.BlockSpec((tm, ATTN_QK), row),
            pl.BlockSpec((tm, ATTN_QK), row),
            pl.BlockSpec((tm, ATTN_V), row),
            pl.BlockSpec((tm, CONV_CH), row),
            pl.BlockSpec((tm, 2 * D_MODEL), row),
        ],
        out_shape=[
            jax.ShapeDtypeStruct((t, ATTN_QK), BF16),
            jax.ShapeDtypeStruct((t, ATTN_QK), BF16),
            jax.ShapeDtypeStruct((t, ATTN_V), BF16),
            jax.ShapeDtypeStruct((t, CONV_CH), F32),
            jax.ShapeDtypeStruct((t, 2 * D_MODEL), BF16),
        ],
        compiler_params=_cparams(1),
        name="in_proj",
    )(x, g, w_bf, cos_t, sin_t)


_ATTN_SUB = 256


def _attn_kernel(lam_ref, q_ref, k_ref, v_ref, g_ref, o_ref, v1_ref, *, out_scale):
    @pl.when(pl.program_id(2) == 0)
    def _():
        v1_ref[:, 0:V_DIM] = v_ref[...]
        v1_ref[:, V_DIM:2 * V_DIM] = jnp.ones(v_ref.shape, BF16)

    k = k_ref[...]
    v1 = v1_ref[...]
    contract_last = (((1,), (1,)), ((), ()))

    def softmax_pv(qc):
        s = lax.dot_general(qc, k, contract_last, preferred_element_type=F32)
        p = jnp.exp(s - jnp.max(s, axis=-1, keepdims=True)).astype(BF16)
        ov = _dot(p, v1)
        return ov[:, 0:V_DIM] / ov[:, V_DIM:V_DIM + 1]

    for r0 in range(0, q_ref.shape[0], _ATTN_SUB):
        q = q_ref[r0:r0 + _ATTN_SUB, :]
        lane = lax.broadcasted_iota(jnp.int32, q.shape, 1)
        zero = jnp.zeros_like(q)
        o0 = softmax_pv(jnp.where(lane < HEAD_DIM, q, zero))
        o1 = softmax_pv(jnp.where(lane >= HEAD_DIM, q, zero))
        o = o0 - lam_ref[0] * o1
        o_ref[r0:r0 + _ATTN_SUB, :] = (_rms(o, g_ref[...]) * out_scale).astype(BF16)


def _attention(q, k, v, lam_full, g_subln, batch, seq, out_scale):
    t = q.shape[0]
    tq = seq
    nq = seq // tq
    kernel = functools.partial(_attn_kernel, out_scale=out_scale)
    grid_spec = pltpu.PrefetchScalarGridSpec(
        num_scalar_prefetch=1,
        grid=(batch, N_HEADS, nq),
        in_specs=[
            pl.BlockSpec((tq, V_DIM), lambda b, h, i, lam: (b * nq + i, h)),
            pl.BlockSpec((seq, V_DIM), lambda b, h, i, lam: (b, h)),
            pl.BlockSpec((seq, V_DIM), lambda b, h, i, lam: (b, h)),
            pl.BlockSpec((1, V_DIM), lambda b, h, i, lam: (0, 0)),
        ],
        out_specs=pl.BlockSpec((tq, V_DIM), lambda b, h, i, lam: (b * nq + i, h)),
        scratch_shapes=[pltpu.VMEM((seq, 2 * V_DIM), BF16)],
    )
    return pl.pallas_call(
        kernel,
        grid_spec=grid_spec,
        out_shape=jax.ShapeDtypeStruct((t, ATTN_V), BF16),
        compiler_params=_cparams(3),
        name="diff_attn",
    )(lam_full, q, k, v, g_subln)


_CONV_HALO = 16
_CONV_ROWS = 128


def _conv_block(zp_ref, w_ref, r0, c0):
    first = _CONV_HALO - CONV_PAD
    window = _CONV_ROWS + SUBLANES
    n_aligned = (CONV_K + first + SUBLANES - 1) // SUBLANES
    acc = None
    for r in reversed(range(SUBLANES)):
        y_r = None
        for a in range(n_aligned):
            tap = SUBLANES * a + r - first
            if 0 <= tap < CONV_K:
                lo = r0 + SUBLANES * a
                term = zp_ref[lo:lo + window, c0:c0 + LANES] * w_ref[tap:tap + 1, c0:c0 + LANES]
                y_r = term if y_r is None else y_r + term
        if acc is None:
            acc = y_r
        else:
            shifted = pltpu.roll(acc, window - 1, axis=0)
            acc = shifted if y_r is None else y_r + shifted
    return acc[0:_CONV_ROWS, :]


def _conv_tail(y, b_ref, lg_ref, lb_ref):
    y = y + b_ref[...]
    mu = jnp.mean(y, axis=-1, keepdims=True)
    yc = y - mu
    yn = yc * lax.rsqrt(jnp.mean(yc * yc, axis=-1, keepdims=True) + EPS) * lg_ref[...] + lb_ref[...]
    return (yn * jax.nn.sigmoid(yn)).astype(BF16)


_MIX_SUB = 256


def _mixed_residual(x_ref, o_ref, zprev_ref, z_ref, znext_ref, gate_ref, cw_ref, cb_ref, lg_ref, lb_ref,
                    wa_ref, wc_ref, wo_ref, out_ref, zp_ref, *, tiles_per_seq):
    tm = x_ref.shape[0]
    pos = pl.program_id(0) % tiles_per_seq
    zp_ref[0:_CONV_HALO, :] = jnp.where(pos > 0, zprev_ref[0], 0.0)
    zp_ref[_CONV_HALO:_CONV_HALO + tm, :] = z_ref[...]
    zp_ref[_CONV_HALO + tm:_CONV_HALO + tm + _CONV_HALO, :] = jnp.where(
        pos < tiles_per_seq - 1, znext_ref[0], 0.0)
    for r0 in range(0, tm, _MIX_SUB):
        rows = slice(r0, r0 + _MIX_SUB)
        conv = jnp.concatenate(
            [jnp.concatenate([_conv_block(zp_ref, cw_ref, r, c0) for c0 in range(0, CONV_CH, LANES)], axis=-1)
             for r in range(r0, r0 + _MIX_SUB, _CONV_ROWS)], axis=0)
        c = _conv_tail(conv, cb_ref, lg_ref, lb_ref)
        attn_d = _dot(o_ref[rows, :], wa_ref[...])
        conv_d = _dot(c, wc_ref[...])
        ga = gate_ref[rows, 0:D_MODEL].astype(F32)
        gc = gate_ref[rows, D_MODEL:2 * D_MODEL].astype(F32)
        mix = (ga * attn_d + gc * conv_d).astype(BF16)
        out_ref[rows, :] = x_ref[rows, :] + _dot(mix, wo_ref[...])


def _mix_kernel(*refs, tiles_per_seq):
    _mixed_residual(*refs, tiles_per_seq=tiles_per_seq)


def _mix_route_kernel(*refs, tiles_per_seq):
    mix_refs, (g_ref, wr_hi_ref, wr_lo_ref, br_ref) = refs[:13], refs[13:17]
    out_ref, meta_ref, cnt_ref, zp_ref, carry_ref = refs[17:]
    _mixed_residual(*mix_refs, out_ref, zp_ref, tiles_per_seq=tiles_per_seq)
    _route(_rms(out_ref[...], g_ref[...]), wr_hi_ref, wr_lo_ref, br_ref, meta_ref, cnt_ref, carry_ref)


def _mix(x, o, z, gates, conv_w, conv_b, ln_g, ln_b, wa, wc, wo, seq, router=None, tm=512):
    t = x.shape[0]
    row = lambda i: (i, 0)
    halo_per_tile = tm // _CONV_HALO
    n_halo = t // _CONV_HALO
    z_halo = z.reshape(n_halo, _CONV_HALO, CONV_CH)
    in_specs = [
        pl.BlockSpec((tm, D_MODEL), row),
        pl.BlockSpec((tm, ATTN_V), row),
        pl.BlockSpec((1, _CONV_HALO, CONV_CH), lambda i: (jnp.maximum(i * halo_per_tile - 1, 0), 0, 0)),
        pl.BlockSpec((tm, CONV_CH), row),
        pl.BlockSpec((1, _CONV_HALO, CONV_CH),
                     lambda i: (jnp.minimum((i + 1) * halo_per_tile, n_halo - 1), 0, 0)),
        pl.BlockSpec((tm, 2 * D_MODEL), row),
        _resident(conv_w.shape), _resident(conv_b.shape), _resident(ln_g.shape), _resident(ln_b.shape),
        _resident(wa.shape), _resident(wc.shape), _resident(wo.shape),
    ]
    args = (x, o, z_halo, z, z_halo, gates, conv_w, conv_b, ln_g, ln_b, wa, wc, wo)
    x_spec = pl.BlockSpec((tm, D_MODEL), row)
    x_shape = jax.ShapeDtypeStruct((t, D_MODEL), F32)
    zp_scratch = pltpu.VMEM((tm + 2 * _CONV_HALO, CONV_CH), F32)
    tiles_per_seq = seq // tm
    if router is None:
        return pl.pallas_call(
            functools.partial(_mix_kernel, tiles_per_seq=tiles_per_seq),
            grid=(t // tm,), in_specs=in_specs, out_specs=x_spec, out_shape=x_shape,
            scratch_shapes=[zp_scratch], compiler_params=_cparams(1), name="branch_mix",
        )(*args)
    return pl.pallas_call(
        functools.partial(_mix_route_kernel, tiles_per_seq=tiles_per_seq),
        grid=(t // tm,),
        in_specs=in_specs + [_resident(a.shape) for a in router],
        out_specs=[x_spec, pl.BlockSpec((tm, LANES), row), pl.BlockSpec((1, LANES), lambda i: (0, 0))],
        out_shape=[x_shape, jax.ShapeDtypeStruct((t, LANES), F32), jax.ShapeDtypeStruct((1, LANES), F32)],
        scratch_shapes=[zp_scratch, pltpu.VMEM((1, LANES), F32)],
        compiler_params=_cparams(1),
        name="branch_mix_route",
    )(*args, *router)


def _swiglu(h, wg_ref, wu_ref, wd_ref):
    g = _dot(h, wg_ref[...])
    u = _dot(h, wu_ref[...])
    a = (g * jax.nn.sigmoid(g) * u).astype(BF16)
    return _dot(a, wd_ref[...])


def _ffn_kernel(x_ref, g_ref, wg_ref, wu_ref, wd_ref, p_ref, gp_ref, wpg_ref, wpp_ref, gf_ref, *rest,
                n_cast, final_norm):
    cast_in, out_ref, cast_out = rest[:n_cast], rest[n_cast], rest[n_cast + 1:]
    x = x_ref[...]
    h = _rms(x, g_ref[...]).astype(BF16)
    x = x + _swiglu(h, wg_ref, wu_ref, wd_ref)
    _ple_tail(x, p_ref, gp_ref, wpg_ref, wpp_ref, gf_ref, out_ref, final_norm)
    for src, dst in zip(cast_in, cast_out):
        dst[...] = src[...].astype(BF16)


def _ffn_ple(x, g, w_gu, w_down, p, g_ple, w_gate, w_proj, g_final, final_norm, to_cast=(), tm=256):
    t = x.shape[0]
    d_ff = w_down.shape[0]
    steps = t // tm
    row = lambda i: (i, 0)
    slabs = [pl.BlockSpec((w.shape[0] // steps, w.shape[1]), row) for w in to_cast]
    for w in to_cast:
        assert w.shape[0] % (steps * 2 * SUBLANES) == 0, w.shape
    outs = pl.pallas_call(
        functools.partial(_ffn_kernel, n_cast=len(to_cast), final_norm=final_norm),
        grid=(steps,),
        in_specs=[
            pl.BlockSpec((tm, D_MODEL), row),
            _resident((1, D_MODEL)),
            pl.BlockSpec((D_MODEL, d_ff), lambda i: (0, 0), pipeline_mode=pl.Buffered(1)),
            pl.BlockSpec((D_MODEL, d_ff), lambda i: (0, 1), pipeline_mode=pl.Buffered(1)),
            _resident((d_ff, D_MODEL)),
            pl.BlockSpec((tm, p.shape[1]), row),
            _resident((1, D_MODEL)),
            _resident(w_gate.shape),
            _resident(w_proj.shape),
            _resident((1, D_MODEL)),
            *slabs,
        ],
        out_specs=[pl.BlockSpec((tm, D_MODEL), row), *slabs],
        out_shape=[jax.ShapeDtypeStruct((t, D_MODEL), F32),
                   *[jax.ShapeDtypeStruct(w.shape, BF16) for w in to_cast]],
        compiler_params=_cparams(1),
        name="dense_ffn_ple",
    )(x, g, w_gu, w_gu, w_down, p, g_ple, w_gate, w_proj, g_final, *to_cast)
    return outs[0], outs[1:]


TM_EXPERT = 256
META_E1, META_E2, META_R1, META_R2, META_W1, META_W2 = range(6)


def _row_copy(src_ref, src_row, dst_ref, dst_row, sem):
    return pltpu.make_async_copy(src_ref.at[pl.ds(src_row, 1)], dst_ref.at[pl.ds(dst_row, 1)], sem)


def _route(hf, wr_hi_ref, wr_lo_ref, br_ref, meta_ref, cnt_ref, carry_ref):
    @pl.when(pl.program_id(0) == 0)
    def _():
        carry_ref[...] = jnp.zeros_like(carry_ref)

    h_hi = hf.astype(BF16)
    h_lo = (hf - h_hi.astype(F32)).astype(BF16)
    logits = (_dot(h_hi, wr_hi_ref[...]) + _dot(h_lo, wr_hi_ref[...])
              + _dot(h_hi, wr_lo_ref[...])) + br_ref[...]
    tm = logits.shape[0]
    lane = lax.broadcasted_iota(jnp.int32, logits.shape, 1)
    neg = jnp.full_like(logits, -jnp.inf)
    logits = jnp.where(lane < N_EXPERTS, logits, neg)
    m1 = jnp.max(logits, axis=-1, keepdims=True)
    i1 = jnp.min(jnp.where(logits == m1, lane, LANES), axis=-1, keepdims=True)
    rest = jnp.where(lane == i1, neg, logits)
    m2 = jnp.max(rest, axis=-1, keepdims=True)
    i2 = jnp.min(jnp.where(rest == m2, lane, LANES), axis=-1, keepdims=True)
    e2 = jnp.exp(m2 - m1)
    denom = 1.0 + e2
    pick1 = lane == i1
    pick2 = lane == i2
    onehot = jnp.where(pick1 | pick2, 1.0, 0.0)
    r = lax.broadcasted_iota(jnp.int32, (tm, tm), 0)
    c = lax.broadcasted_iota(jnp.int32, (tm, tm), 1)
    strict_lower = jnp.where(r > c, 1.0, 0.0).astype(BF16)
    before = _dot(strict_lower, onehot.astype(BF16)) + carry_ref[...]
    rank1 = jnp.sum(jnp.where(pick1, before, 0.0), axis=-1, keepdims=True)
    rank2 = jnp.sum(jnp.where(pick2, before, 0.0), axis=-1, keepdims=True)
    record = jnp.zeros_like(logits)
    for slot, val in ((META_E1, i1.astype(F32)), (META_E2, i2.astype(F32)),
                      (META_R1, rank1), (META_R2, rank2),
                      (META_W1, 1.0 / denom), (META_W2, e2 / denom)):
        record = jnp.where(lane == slot, val, record)
    meta_ref[...] = record
    carry_ref[...] += jnp.sum(onehot, axis=0, keepdims=True)
    cnt_ref[...] = carry_ref[...]


def _dispatch_kernel(ends_ref, sizes_ref, h_ref, p1_ref, p2_ref, xs_ref, zero_ref, row_sem, zero_sem):
    tm = h_ref.shape[0]

    @pl.when(pl.program_id(0) == 0)
    def _():
        zero_ref[...] = jnp.zeros_like(zero_ref)
        n_tiles = xs_ref.shape[0] // TM_EXPERT
        total = ends_ref[N_EXPERTS - 1]

        def zero_tile(row):
            start = pl.multiple_of(row, TM_EXPERT)
            return pltpu.make_async_copy(zero_ref, xs_ref.at[pl.ds(start, TM_EXPERT)], zero_sem)

        fills = [(sizes_ref[e] > 0, ends_ref[e] - TM_EXPERT) for e in range(N_EXPERTS)]
        fills += [(j * TM_EXPERT >= total, j * TM_EXPERT) for j in range(n_tiles - N_EXPERTS, n_tiles)]
        for needed, row in fills:
            @pl.when(needed)
            def _():
                zero_tile(row).start()
        for needed, row in fills:
            @pl.when(needed)
            def _():
                zero_tile(row).wait()

    def scatter_row(r, carry):
        _row_copy(h_ref, r, xs_ref, p1_ref[0, 0, r], row_sem).start(priority=0)
        _row_copy(h_ref, r, xs_ref, p2_ref[0, 0, r], row_sem).start(priority=1)
        return carry

    lax.fori_loop(0, tm, scatter_row, 0, unroll=8)
    for _ in range(2):
        pltpu.make_async_copy(h_ref, xs_ref.at[pl.ds(0, tm)], row_sem).wait()


def _dispatch(h, pos1, pos2, ends, sizes, n_rows, tm=512):
    t = h.shape[0]
    idx = lambda a: a.reshape(t // tm, 1, tm)
    smem_block = pl.BlockSpec((1, 1, tm), lambda i, *_: (i, 0, 0), memory_space=pltpu.SMEM)
    grid_spec = pltpu.PrefetchScalarGridSpec(
        num_scalar_prefetch=2,
        grid=(t // tm,),
        in_specs=[pl.BlockSpec((tm, D_MODEL), lambda i, *_: (i, 0)), smem_block, smem_block],
        out_specs=pl.BlockSpec(memory_space=pl.ANY),
        scratch_shapes=[pltpu.VMEM((TM_EXPERT, D_MODEL), F32),
                        pltpu.SemaphoreType.DMA(()), pltpu.SemaphoreType.DMA(())],
    )
    return pl.pallas_call(
        _dispatch_kernel,
        grid_spec=grid_spec,
        out_shape=jax.ShapeDtypeStruct((n_rows, D_MODEL), F32),
        compiler_params=_cparams(1),
        name="moe_dispatch",
    )(ends, sizes, h, idx(pos1), idx(pos2))


def _expert_kernel(tile_expert_ref, tile_src_ref, x_ref, g_ref, wg_ref, wu_ref, wd_ref, y_ref):
    j = pl.program_id(0)

    @pl.when(tile_src_ref[j] != j)
    def _():
        y_ref[...] = jnp.zeros_like(y_ref)

    @pl.when(tile_src_ref[j] == j)
    def _():
        xb = _rms(x_ref[...], g_ref[...]).astype(BF16)
        half = wd_ref.shape[0] // 2
        y = None
        for c0 in (0, half):
            g = _dot(xb, wg_ref[:, c0:c0 + half])
            u = _dot(xb, wu_ref[:, c0:c0 + half])
            a = (g * jax.nn.sigmoid(g) * u).astype(BF16)
            part = _dot(a, wd_ref[c0:c0 + half, :])
            y = part if y is None else y + part
        y_ref[...] = y


def _experts(xs, g, tile_expert, tile_src, we_gu, we_down):
    n_rows = xs.shape[0]
    d_e = we_down.shape[1]
    tile = lambda j, te, ts: (ts[j], 0)
    grid_spec = pltpu.PrefetchScalarGridSpec(
        num_scalar_prefetch=2,
        grid=(n_rows // TM_EXPERT,),
        in_specs=[
            pl.BlockSpec((TM_EXPERT, D_MODEL), tile),
            pl.BlockSpec((1, D_MODEL), lambda j, te, ts: (0, 0)),
            pl.BlockSpec((None, D_MODEL, d_e), lambda j, te, ts: (te[j], 0, 0)),
            pl.BlockSpec((None, D_MODEL, d_e), lambda j, te, ts: (te[j], 0, 1)),
            pl.BlockSpec((None, d_e, D_MODEL), lambda j, te, ts: (te[j], 0, 0)),
        ],
        out_specs=pl.BlockSpec((TM_EXPERT, D_MODEL), lambda j, te, ts: (j, 0)),
    )
    return pl.pallas_call(
        _expert_kernel,
        grid_spec=grid_spec,
        out_shape=jax.ShapeDtypeStruct((n_rows, D_MODEL), F32),
        compiler_params=pltpu.CompilerParams(
            dimension_semantics=("arbitrary",), vmem_limit_bytes=VMEM_LIMIT_EXPERT),
        name="moe_experts",
    )(tile_expert, tile_src, xs, g, we_gu, we_gu, we_down)


def _ple_tail(x, p_ref, g_ref, wg_ref, wp_ref, gf_ref, out_ref, final_norm):
    h = _rms(x, g_ref[...]).astype(BF16)
    gate = jax.nn.sigmoid(_dot(h, wg_ref[...]))
    y = x + gate * _dot(p_ref[...].astype(BF16), wp_ref[...])
    if final_norm:
        y = _rms(y, gf_ref[...])
    out_ref[...] = y


def _combine_ple_kernel(x_ref, meta_ref, p1_ref, p2_ref, p1_next_ref, p2_next_ref, y_ref,
                        p_ref, g_ref, wg_ref, wp_ref, gf_ref, out_ref, rows_ref, sems, *, final_norm):
    i = pl.program_id(0)
    n = pl.num_programs(0)
    tm = x_ref.shape[0]
    slot = i % 2

    def gather(p1, p2, s):
        def gather_row(r, carry):
            _row_copy(y_ref, p1[0, 0, r], rows_ref.at[s, 0], r, sems.at[s]).start(priority=0)
            _row_copy(y_ref, p2[0, 0, r], rows_ref.at[s, 1], r, sems.at[s]).start(priority=1)
            return carry
        lax.fori_loop(0, tm, gather_row, 0, unroll=8)

    @pl.when(i == 0)
    def _():
        gather(p1_ref, p2_ref, 0)

    @pl.when(i + 1 < n)
    def _():
        gather(p1_next_ref, p2_next_ref, 1 - slot)

    for k in range(2):
        pltpu.make_async_copy(y_ref.at[pl.ds(0, tm)], rows_ref.at[slot, k], sems.at[slot]).wait()
    meta = meta_ref[...]
    x = (x_ref[...] + meta[:, META_W1:META_W1 + 1] * rows_ref[slot, 0]
         + meta[:, META_W2:META_W2 + 1] * rows_ref[slot, 1])
    _ple_tail(x, p_ref, g_ref, wg_ref, wp_ref, gf_ref, out_ref, final_norm)


def _combine_ple(x, meta, pos1, pos2, y, p, g, w_gate, w_proj, g_final, final_norm, tm=256):
    t = x.shape[0]
    nb = t // tm
    ple_dim = p.shape[1]
    row = lambda i: (i, 0)
    idx = lambda a: a.reshape(nb, 1, tm)
    cur = pl.BlockSpec((1, 1, tm), lambda i: (i, 0, 0), memory_space=pltpu.SMEM)
    nxt = pl.BlockSpec((1, 1, tm), lambda i: (jnp.minimum(i + 1, nb - 1), 0, 0),
                       memory_space=pltpu.SMEM)
    return pl.pallas_call(
        functools.partial(_combine_ple_kernel, final_norm=final_norm),
        grid=(nb,),
        in_specs=[
            pl.BlockSpec((tm, D_MODEL), row),
            pl.BlockSpec((tm, LANES), row),
            cur, cur, nxt, nxt,
            pl.BlockSpec(memory_space=pl.ANY),
            pl.BlockSpec((tm, ple_dim), row),
            _resident((1, D_MODEL)),
            _resident(w_gate.shape),
            _resident(w_proj.shape),
            _resident((1, D_MODEL)),
        ],
        out_specs=pl.BlockSpec((tm, D_MODEL), row),
        out_shape=jax.ShapeDtypeStruct((t, D_MODEL), F32),
        scratch_shapes=[pltpu.VMEM((2, 2, tm, D_MODEL), F32), pltpu.SemaphoreType.DMA((2,))],
        compiler_params=_cparams(1),
        name="moe_combine_ple",
    )(x, meta, idx(pos1), idx(pos2), idx(pos1), idx(pos2), y, p, g, w_gate, w_proj, g_final)


def _moe_plan(meta, counts):
    cnt = counts[0, :N_EXPERTS].astype(jnp.int32)
    sizes = (cnt + TM_EXPERT - 1) // TM_EXPERT * TM_EXPERT
    ends = jnp.cumsum(sizes)
    starts = ends - sizes
    e1 = meta[:, META_E1].astype(jnp.int32)
    e2 = meta[:, META_E2].astype(jnp.int32)
    start_of = lambda e: jnp.sum(jnp.where(e[:, None] == jnp.arange(N_EXPERTS)[None, :],
                                           starts[None, :], 0), axis=1)
    pos1 = start_of(e1) + meta[:, META_R1].astype(jnp.int32)
    pos2 = start_of(e2) + meta[:, META_R2].astype(jnp.int32)
    n_rows = 2 * meta.shape[0] + N_EXPERTS * TM_EXPERT
    n_valid = ends[-1] // TM_EXPERT
    tile_src = jnp.minimum(jnp.arange(n_rows // TM_EXPERT, dtype=jnp.int32), n_valid - 1)
    tile_expert = jnp.sum((tile_src * TM_EXPERT)[:, None] >= ends[None, :], axis=1).astype(jnp.int32)
    return pos1, pos2, ends.astype(jnp.int32), sizes, tile_expert, tile_src, n_rows


def _rope_tables(seq):
    inv_freq = ROPE_THETA ** (-jnp.arange(0, HEAD_DIM, 2, dtype=F32) / HEAD_DIM)
    ang = jnp.arange(seq, dtype=F32)[:, None] * inv_freq[None, :]
    cos, sin = jnp.cos(ang), jnp.sin(ang)
    cos_t = jnp.concatenate([cos] * (LANES // (HEAD_DIM // 2)), axis=-1)
    sin_t = jnp.concatenate([-sin, sin] * (LANES // HEAD_DIM), axis=-1)
    return cos_t, sin_t


def kernel(x, p, g_mix, w_in, lam, g_subln, w_attn_out, conv_w, conv_b, conv_ln_g, conv_ln_b, w_conv_out, w_o, g_ffn, w_ff_gu, w_ff_down, w_router, b_router, we_gu, we_down, g_ple, w_ple_gate, w_ple_proj, g_final):
    batch, seq, _ = x.shape
    depth = w_in.shape[0]
    t = batch * seq
    cos_t, sin_t = _rope_tables(seq)
    xs = x.reshape(t, D_MODEL)
    row2 = lambda a: a.reshape(1, -1)
    expert_bf16 = {}
    for i in range(depth):
        q, k, v, z, gates = _in_proj(xs, row2(g_mix[i]), w_in[i].astype(BF16), cos_t, sin_t, seq)
        lam_init = 0.8 - 0.6 * math.exp(-0.3 * i)
        lf = lam[i].astype(F32)
        lam_full = (jnp.exp(jnp.sum(lf[0] * lf[1])) - jnp.exp(jnp.sum(lf[2] * lf[3]))
                    + lam_init).reshape(1)
        o = _attention(q, k, v, lam_full, row2(g_subln[i]), batch, seq, 1.0 - lam_init)
        mix_args = (xs, o, z, gates, conv_w[i], row2(conv_b[i]), row2(conv_ln_g[i]), row2(conv_ln_b[i]),
                    w_attn_out[i].astype(BF16), w_conv_out[i].astype(BF16), w_o[i].astype(BF16), seq)
        ple_args = (p[i].reshape(t, -1), row2(g_ple[i]), w_ple_gate[i].astype(BF16),
                    w_ple_proj[i].astype(BF16), row2(g_final))
        final_norm = i == depth - 1
        if i % 2 == 0:
            xs = _mix(*mix_args)
            m_next = i // 2
            to_cast = ()
            if i + 1 < depth:
                to_cast = (we_gu[m_next].reshape(-1, we_gu.shape[-1]),
                           we_down[m_next].reshape(-1, we_down.shape[-1]))
            xs, cast = _ffn_ple(xs, row2(g_ffn[i]), w_ff_gu[i // 2].astype(BF16),
                                w_ff_down[i // 2].astype(BF16), *ple_args, final_norm, to_cast)
            if cast:
                expert_bf16[m_next] = (cast[0].reshape(we_gu.shape[1:]), cast[1].reshape(we_down.shape[1:]))
        else:
            m = i // 2
            wr = jnp.pad(w_router[m], ((0, 0), (0, LANES - N_EXPERTS)))
            wr_hi = wr.astype(BF16)
            wr_lo = (wr - wr_hi.astype(F32)).astype(BF16)
            br = jnp.pad(b_router[m], (0, LANES - N_EXPERTS)).reshape(1, LANES)
            xs, meta, counts = _mix(*mix_args, router=(row2(g_ffn[i]), wr_hi, wr_lo, br))
            pos1, pos2, ends, sizes, tile_expert, tile_src, n_rows = _moe_plan(meta, counts)
            routed = _dispatch(xs, pos1, pos2, ends, sizes, n_rows)
            if m not in expert_bf16:
                expert_bf16[m] = (we_gu[m].astype(BF16), we_down[m].astype(BF16))
            y = _experts(routed, row2(g_ffn[i]), tile_expert, tile_src, *expert_bf16[m])
            xs = _combine_ple(xs, meta, pos1, pos2, y, *ple_args, final_norm=final_norm)
    return xs.reshape(batch, seq, D_MODEL)
```

```python
import functools
import math

import jax
import jax.numpy as jnp
from jax import lax
from jax.experimental import pallas as pl
from jax.experimental.pallas import tpu as pltpu

D_MODEL = 1024
N_HEADS = 4
HEAD_DIM = 64
V_DIM = 2 * HEAD_DIM
ATTN_QK = N_HEADS * 2 * HEAD_DIM
ATTN_V = N_HEADS * V_DIM
ROPE_THETA = 10000.0
CONV_CH = 512
CONV_K = 31
CONV_PAD = (CONV_K - 1) // 2
N_EXPERTS = 8
EPS = 1e-6

LANES = 128
SUBLANES = 8
VMEM_LIMIT = 56 * 1024 * 1024
VMEM_LIMIT_EXPERT = 60 * 1024 * 1024

F32 = jnp.float32
BF16 = jnp.bfloat16


def _cparams(n_axes):
    return pltpu.CompilerParams(
        dimension_semantics=("arbitrary",) * n_axes, vmem_limit_bytes=VMEM_LIMIT)


def _resident(shape):
    nd = len(shape)
    return pl.BlockSpec(shape, lambda *_: (0,) * nd, pipeline_mode=pl.Buffered(1))


def _rms(x, g):
    return x * lax.rsqrt(jnp.mean(x * x, axis=-1, keepdims=True) + EPS) * g


def _dot(a, b):
    return jnp.dot(a, b, preferred_element_type=F32)


BF16_ROW_TILE = 2 * SUBLANES


def _cast_specs(to_cast, steps):
    specs = []
    for w in to_cast:
        rows = w.shape[0]
        n_blocks = max(n for n in range(1, steps + 1)
                       if steps % n == 0 and rows % (n * BF16_ROW_TILE) == 0)
        every = steps // n_blocks
        specs.append(pl.BlockSpec((rows // n_blocks, w.shape[1]), lambda i, every=every: (i // every, 0)))
    return specs, [jax.ShapeDtypeStruct(w.shape, BF16) for w in to_cast]


def _cast_slabs(src_refs, dst_refs):
    for src, dst in zip(src_refs, dst_refs):
        dst[...] = src[...].astype(BF16)


def _rope(t, cos, sin_signed):
    lane = lax.broadcasted_iota(jnp.int32, t.shape, 1)
    first_half = (lane & (HEAD_DIM // 2)) == 0
    width = t.shape[-1]
    partner = jnp.where(first_half,
                        pltpu.roll(t, width - HEAD_DIM // 2, axis=1),
                        pltpu.roll(t, HEAD_DIM // 2, axis=1))
    reps = width // LANES
    c = jnp.concatenate([cos] * reps, axis=-1)
    s = jnp.concatenate([sin_signed] * reps, axis=-1)
    return t * c + partner * s


def _in_proj_kernel(x_ref, g_ref, w_ref, cos_ref, sin_ref, *rest, n_cast):
    cast_in, (q_ref, k_ref, v_ref, z_ref, gate_ref), cast_out = (
        rest[:n_cast], rest[n_cast:n_cast + 5], rest[n_cast + 5:])
    _cast_slabs(cast_in, cast_out)
    h = _rms(x_ref[...], g_ref[...]).astype(BF16)
    cos = cos_ref[...]
    sin = sin_ref[...]

    def proj(c0, n):
        return _dot(h, w_ref[:, c0:c0 + n])

    q = _rope(proj(0, ATTN_QK), cos, sin) * (HEAD_DIM ** -0.5)
    q_ref[...] = q.astype(BF16)
    k_ref[...] = _rope(proj(ATTN_QK, ATTN_QK), cos, sin).astype(BF16)
    v_ref[...] = proj(2 * ATTN_QK, ATTN_V).astype(BF16)
    u0 = 2 * ATTN_QK + ATTN_V
    a = proj(u0, CONV_CH)
    g = proj(u0 + CONV_CH, CONV_CH)
    z_ref[...] = a * jax.nn.sigmoid(g)
    g0 = u0 + 2 * CONV_CH
    step = 512
    for c in range(0, 2 * D_MODEL, step):
        gate_ref[:, c:c + step] = jax.nn.sigmoid(proj(g0 + c, step)).astype(BF16)


def _in_proj(x, g, w_bf, cos_t, sin_t, seq, to_cast=(), tm=512):
    t = x.shape[0]
    n_in = w_bf.shape[1]
    seq_blocks = seq // tm
    row = lambda i: (i, 0)
    tab = lambda i: (i % seq_blocks, 0)
    cast_specs, cast_shapes = _cast_specs(to_cast, t // tm)
    outs = pl.pallas_call(
        functools.partial(_in_proj_kernel, n_cast=len(to_cast)),
        grid=(t // tm,),
        in_specs=[
            pl.BlockSpec((tm, D_MODEL), row),
            _resident((1, D_MODEL)),
            _resident((D_MODEL, n_in)),
            pl.BlockSpec((tm, LANES), tab),
            pl.BlockSpec((tm, LANES), tab),
            *cast_specs,
        ],
        out_specs=[
            pl.BlockSpec((tm, ATTN_QK), row),
            pl.BlockSpec((tm, ATTN_QK), row),
            pl.BlockSpec((tm, ATTN_V), row),
            pl.BlockSpec((tm, CONV_CH), row),
            pl.BlockSpec((tm, 2 * D_MODEL), row),
            *cast_specs,
        ],
        out_shape=[
            jax.ShapeDtypeStruct((t, ATTN_QK), BF16),
            jax.ShapeDtypeStruct((t, ATTN_QK), BF16),
            jax.ShapeDtypeStruct((t, ATTN_V), BF16),
            jax.ShapeDtypeStruct((t, CONV_CH), F32),
            jax.ShapeDtypeStruct((t, 2 * D_MODEL), BF16),
            *cast_shapes,
        ],
        compiler_params=_cparams(1),
        name="in_proj",
    )(x, g, w_bf, cos_t, sin_t, *to_cast)
    return outs[:5], outs[5:]


_ATTN_SUB = 256


_ATTN_HEADS = 2


def _attn_kernel(lam_ref, q_ref, k_ref, v_ref, g_ref, o_ref, v1_ref, *, out_scale):
    contract_last = (((1,), (1,)), ((), ()))
    for hh in range(_ATTN_HEADS):
        head = slice(hh * V_DIM, (hh + 1) * V_DIM)
        v1_ref[hh, :, 0:V_DIM] = v_ref[:, head]
        v1_ref[hh, :, V_DIM:2 * V_DIM] = jnp.ones((v_ref.shape[0], V_DIM), BF16)
        k = k_ref[:, head]
        v1 = v1_ref[hh]

        def softmax_pv(qc):
            s = lax.dot_general(qc, k, contract_last, preferred_element_type=F32)
            p = jnp.exp(s - jnp.max(s, axis=-1, keepdims=True)).astype(BF16)
            ov = _dot(p, v1)
            return ov[:, 0:V_DIM] / ov[:, V_DIM:V_DIM + 1]

        for r0 in range(0, q_ref.shape[0], _ATTN_SUB):
            q = q_ref[r0:r0 + _ATTN_SUB, head]
            lane = lax.broadcasted_iota(jnp.int32, q.shape, 1)
            zero = jnp.zeros_like(q)
            o0 = softmax_pv(jnp.where(lane < HEAD_DIM, q, zero))
            o1 = softmax_pv(jnp.where(lane >= HEAD_DIM, q, zero))
            o = o0 - lam_ref[0] * o1
            o_ref[r0:r0 + _ATTN_SUB, head] = (_rms(o, g_ref[...]) * out_scale).astype(BF16)


def _attention(q, k, v, lam_full, g_subln, batch, seq, out_scale):
    t = q.shape[0]
    width = _ATTN_HEADS * V_DIM
    block = pl.BlockSpec((seq, width), lambda b, h, lam: (b, h))
    grid_spec = pltpu.PrefetchScalarGridSpec(
        num_scalar_prefetch=1,
        grid=(batch, N_HEADS // _ATTN_HEADS),
        in_specs=[block, block, block, pl.BlockSpec((1, V_DIM), lambda b, h, lam: (0, 0))],
        out_specs=block,
        scratch_shapes=[pltpu.VMEM((_ATTN_HEADS, seq, 2 * V_DIM), BF16)],
    )
    return pl.pallas_call(
        functools.partial(_attn_kernel, out_scale=out_scale),
        grid_spec=grid_spec,
        out_shape=jax.ShapeDtypeStruct((t, ATTN_V), BF16),
        compiler_params=_cparams(2),
        name="diff_attn",
    )(lam_full, q, k, v, g_subln)


_CONV_HALO = 16
_CONV_ROWS = 128


def _conv_block(zp_ref, w_ref, r0, c0):
    first = _CONV_HALO - CONV_PAD
    window = _CONV_ROWS + SUBLANES
    n_aligned = (CONV_K + first + SUBLANES - 1) // SUBLANES
    acc = None
    for r in reversed(range(SUBLANES)):
        y_r = None
        for a in range(n_aligned):
            tap = SUBLANES * a + r - first
            if 0 <= tap < CONV_K:
                lo = r0 + SUBLANES * a
                term = zp_ref[lo:lo + window, c0:c0 + LANES] * w_ref[tap:tap + 1, c0:c0 + LANES]
                y_r = term if y_r is None else y_r + term
        if acc is None:
            acc = y_r
        else:
            shifted = pltpu.roll(acc, window - 1, axis=0)
            acc = shifted if y_r is None else y_r + shifted
    return acc[0:_CONV_ROWS, :]


def _conv_tail(y, b_ref, lg_ref, lb_ref):
    y = y + b_ref[...]
    mu = jnp.mean(y, axis=-1, keepdims=True)
    yc = y - mu
    yn = yc * lax.rsqrt(jnp.mean(yc * yc, axis=-1, keepdims=True) + EPS) * lg_ref[...] + lb_ref[...]
    return (yn * jax.nn.sigmoid(yn)).astype(BF16)


_MIX_SUB = 256


def _mixed_residual(x_ref, o_ref, zprev_ref, z_ref, znext_ref, gate_ref, cw_ref, cb_ref, lg_ref, lb_ref,
                    wa_ref, wc_ref, wo_ref, out_ref, zp_ref, *, tiles_per_seq):
    tm = x_ref.shape[0]
    pos = pl.program_id(0) % tiles_per_seq
    zp_ref[0:_CONV_HALO, :] = jnp.where(pos > 0, zprev_ref[0], 0.0)
    zp_ref[_CONV_HALO:_CONV_HALO + tm, :] = z_ref[...]
    zp_ref[_CONV_HALO + tm:_CONV_HALO + tm + _CONV_HALO, :] = jnp.where(
        pos < tiles_per_seq - 1, znext_ref[0], 0.0)
    for r0 in range(0, tm, _MIX_SUB):
        rows = slice(r0, r0 + _MIX_SUB)
        conv = jnp.concatenate(
            [jnp.concatenate([_conv_block(zp_ref, cw_ref, r, c0) for c0 in range(0, CONV_CH, LANES)], axis=-1)
             for r in range(r0, r0 + _MIX_SUB, _CONV_ROWS)], axis=0)
        c = _conv_tail(conv, cb_ref, lg_ref, lb_ref)
        attn_d = _dot(o_ref[rows, :], wa_ref[...])
        conv_d = _dot(c, wc_ref[...])
        ga = gate_ref[rows, 0:D_MODEL].astype(F32)
        gc = gate_ref[rows, D_MODEL:2 * D_MODEL].astype(F32)
        mix = (ga * attn_d + gc * conv_d).astype(BF16)
        out_ref[rows, :] = x_ref[rows, :] + _dot(mix, wo_ref[...])


def _mix_kernel(*refs, tiles_per_seq):
    _mixed_residual(*refs, tiles_per_seq=tiles_per_seq)


def _mix_route_kernel(*refs, tiles_per_seq):
    mix_refs, (g_ref, wr_hi_ref, wr_lo_ref, br_ref) = refs[:13], refs[13:17]
    out_ref, meta_ref, cnt_ref, zp_ref, carry_ref = refs[17:]
    _mixed_residual(*mix_refs, out_ref, zp_ref, tiles_per_seq=tiles_per_seq)
    _route(_rms(out_ref[...], g_ref[...]), wr_hi_ref, wr_lo_ref, br_ref, meta_ref, cnt_ref, carry_ref)


def _mix(x, o, z, gates, conv_w, conv_b, ln_g, ln_b, wa, wc, wo, seq, router=None, tm=512):
    t = x.shape[0]
    row = lambda i: (i, 0)
    halo_per_tile = tm // _CONV_HALO
    n_halo = t // _CONV_HALO
    z_halo = z.reshape(n_halo, _CONV_HALO, CONV_CH)
    in_specs = [
        pl.BlockSpec((tm, D_MODEL), row),
        pl.BlockSpec((tm, ATTN_V), row),
        pl.BlockSpec((1, _CONV_HALO, CONV_CH), lambda i: (jnp.maximum(i * halo_per_tile - 1, 0), 0, 0)),
        pl.BlockSpec((tm, CONV_CH), row),
        pl.BlockSpec((1, _CONV_HALO, CONV_CH),
                     lambda i: (jnp.minimum((i + 1) * halo_per_tile, n_halo - 1), 0, 0)),
        pl.BlockSpec((tm, 2 * D_MODEL), row),
        _resident(conv_w.shape), _resident(conv_b.shape), _resident(ln_g.shape), _resident(ln_b.shape),
        _resident(wa.shape), _resident(wc.shape), _resident(wo.shape),
    ]
    args = (x, o, z_halo, z, z_halo, gates, conv_w, conv_b, ln_g, ln_b, wa, wc, wo)
    x_spec = pl.BlockSpec((tm, D_MODEL), row)
    x_shape = jax.ShapeDtypeStruct((t, D_MODEL), F32)
    zp_scratch = pltpu.VMEM((tm + 2 * _CONV_HALO, CONV_CH), F32)
    tiles_per_seq = seq // tm
    if router is None:
        return pl.pallas_call(
            functools.partial(_mix_kernel, tiles_per_seq=tiles_per_seq),
            grid=(t // tm,), in_specs=in_specs, out_specs=x_spec, out_shape=x_shape,
            scratch_shapes=[zp_scratch], compiler_params=_cparams(1), name="branch_mix",
        )(*args)
    return pl.pallas_call(
        functools.partial(_mix_route_kernel, tiles_per_seq=tiles_per_seq),
        grid=(t // tm,),
        in_specs=in_specs + [_resident(a.shape) for a in router],
        out_specs=[x_spec, pl.BlockSpec((tm, LANES), row), pl.BlockSpec((1, LANES), lambda i: (0, 0))],
        out_shape=[x_shape, jax.ShapeDtypeStruct((t, LANES), F32), jax.ShapeDtypeStruct((1, LANES), F32)],
        scratch_shapes=[zp_scratch, pltpu.VMEM((1, LANES), F32)],
        compiler_params=_cparams(1),
        name="branch_mix_route",
    )(*args, *router)


def _swiglu(h, wg_ref, wu_ref, wd_ref):
    g = _dot(h, wg_ref[...])
    u = _dot(h, wu_ref[...])
    a = (g * jax.nn.sigmoid(g) * u).astype(BF16)
    return _dot(a, wd_ref[...])


def _ffn_kernel(x_ref, g_ref, wg_ref, wu_ref, wd_ref, p_ref, gp_ref, wpg_ref, wpp_ref, gf_ref, *rest,
                n_cast, final_norm):
    cast_in, out_ref, cast_out = rest[:n_cast], rest[n_cast], rest[n_cast + 1:]
    x = x_ref[...]
    h = _rms(x, g_ref[...]).astype(BF16)
    x = x + _swiglu(h, wg_ref, wu_ref, wd_ref)
    _ple_tail(x, p_ref, gp_ref, wpg_ref, wpp_ref, gf_ref, out_ref, final_norm)
    _cast_slabs(cast_in, cast_out)


def _ffn_ple(x, g, w_gu, w_down, p, g_ple, w_gate, w_proj, g_final, final_norm, to_cast=(), tm=256):
    t = x.shape[0]
    d_ff = w_down.shape[0]
    steps = t // tm
    row = lambda i: (i, 0)
    slabs, cast_shapes = _cast_specs(to_cast, steps)
    outs = pl.pallas_call(
        functools.partial(_ffn_kernel, n_cast=len(to_cast), final_norm=final_norm),
        grid=(steps,),
        in_specs=[
            pl.BlockSpec((tm, D_MODEL), row),
            _resident((1, D_MODEL)),
            pl.BlockSpec((D_MODEL, d_ff), lambda i: (0, 0), pipeline_mode=pl.Buffered(1)),
            pl.BlockSpec((D_MODEL, d_ff), lambda i: (0, 1), pipeline_mode=pl.Buffered(1)),
            _resident((d_ff, D_MODEL)),
            pl.BlockSpec((tm, p.shape[1]), row),
            _resident((1, D_MODEL)),
            _resident(w_gate.shape),
            _resident(w_proj.shape),
            _resident((1, D_MODEL)),
            *slabs,
        ],
        out_specs=[pl.BlockSpec((tm, D_MODEL), row), *slabs],
        out_shape=[jax.ShapeDtypeStruct((t, D_MODEL), F32), *cast_shapes],
        compiler_params=_cparams(1),
        name="dense_ffn_ple",
    )(x, g, w_gu, w_gu, w_down, p, g_ple, w_gate, w_proj, g_final, *to_cast)
    return outs[0], outs[1:]


TM_EXPERT = 256
META_E1, META_E2, META_R1, META_R2, META_W1, META_W2 = range(6)


def _row_copy(src_ref, src_row, dst_ref, dst_row, sem):
    return pltpu.make_async_copy(src_ref.at[pl.ds(src_row, 1)], dst_ref.at[pl.ds(dst_row, 1)], sem)


def _route(hf, wr_hi_ref, wr_lo_ref, br_ref, meta_ref, cnt_ref, carry_ref):
    @pl.when(pl.program_id(0) == 0)
    def _():
        carry_ref[...] = jnp.zeros_like(carry_ref)

    h_hi = hf.astype(BF16)
    h_lo = (hf - h_hi.astype(F32)).astype(BF16)
    logits = (_dot(h_hi, wr_hi_ref[...]) + _dot(h_lo, wr_hi_ref[...])
              + _dot(h_hi, wr_lo_ref[...])) + br_ref[...]
    tm = logits.shape[0]
    lane = lax.broadcasted_iota(jnp.int32, logits.shape, 1)
    neg = jnp.full_like(logits, -jnp.inf)
    logits = jnp.where(lane < N_EXPERTS, logits, neg)
    m1 = jnp.max(logits, axis=-1, keepdims=True)
    i1 = jnp.min(jnp.where(logits == m1, lane, LANES), axis=-1, keepdims=True)
    rest = jnp.where(lane == i1, neg, logits)
    m2 = jnp.max(rest, axis=-1, keepdims=True)
    i2 = jnp.min(jnp.where(rest == m2, lane, LANES), axis=-1, keepdims=True)
    e2 = jnp.exp(m2 - m1)
    denom = 1.0 + e2
    pick1 = lane == i1
    pick2 = lane == i2
    onehot = jnp.where(pick1 | pick2, 1.0, 0.0)
    r = lax.broadcasted_iota(jnp.int32, (tm, tm), 0)
    c = lax.broadcasted_iota(jnp.int32, (tm, tm), 1)
    strict_lower = jnp.where(r > c, 1.0, 0.0).astype(BF16)
    before = _dot(strict_lower, onehot.astype(BF16)) + carry_ref[...]
    rank1 = jnp.sum(jnp.where(pick1, before, 0.0), axis=-1, keepdims=True)
    rank2 = jnp.sum(jnp.where(pick2, before, 0.0), axis=-1, keepdims=True)
    record = jnp.zeros_like(logits)
    for slot, val in ((META_E1, i1.astype(F32)), (META_E2, i2.astype(F32)),
                      (META_R1, rank1), (META_R2, rank2),
                      (META_W1, 1.0 / denom), (META_W2, e2 / denom)):
        record = jnp.where(lane == slot, val, record)
    meta_ref[...] = record
    carry_ref[...] += jnp.sum(onehot, axis=0, keepdims=True)
    cnt_ref[...] = carry_ref[...]


def _dispatch_kernel(ends_ref, sizes_ref, h_ref, p1_ref, p2_ref, xs_ref, zero_ref, row_sem, zero_sem):
    tm = h_ref.shape[0]

    @pl.when(pl.program_id(0) == 0)
    def _():
        zero_ref[...] = jnp.zeros_like(zero_ref)
        n_tiles = xs_ref.shape[0] // TM_EXPERT
        total = ends_ref[N_EXPERTS - 1]

        def zero_tile(row):
            start = pl.multiple_of(row, TM_EXPERT)
            return pltpu.make_async_copy(zero_ref, xs_ref.at[pl.ds(start, TM_EXPERT)], zero_sem)

        fills = [(sizes_ref[e] > 0, ends_ref[e] - TM_EXPERT) for e in range(N_EXPERTS)]
        fills += [(j * TM_EXPERT >= total, j * TM_EXPERT) for j in range(n_tiles - N_EXPERTS, n_tiles)]
        for needed, row in fills:
            @pl.when(needed)
            def _():
                zero_tile(row).start()
        for needed, row in fills:
            @pl.when(needed)
            def _():
                zero_tile(row).wait()

    def scatter_row(r, carry):
        _row_copy(h_ref, r, xs_ref, p1_ref[0, 0, r], row_sem).start(priority=0)
        _row_copy(h_ref, r, xs_ref, p2_ref[0, 0, r], row_sem).start(priority=1)
        return carry

    lax.fori_loop(0, tm, scatter_row, 0, unroll=8)
    for _ in range(2):
        pltpu.make_async_copy(h_ref, xs_ref.at[pl.ds(0, tm)], row_sem).wait()


def _dispatch(h, pos1, pos2, ends, sizes, n_rows, tm=512):
    t = h.shape[0]
    idx = lambda a: a.reshape(t // tm, 1, tm)
    smem_block = pl.BlockSpec((1, 1, tm), lambda i, *_: (i, 0, 0), memory_space=pltpu.SMEM)
    grid_spec = pltpu.PrefetchScalarGridSpec(
        num_scalar_prefetch=2,
        grid=(t // tm,),
        in_specs=[pl.BlockSpec((tm, D_MODEL), lambda i, *_: (i, 0)), smem_block, smem_block],
        out_specs=pl.BlockSpec(memory_space=pl.ANY),
        scratch_shapes=[pltpu.VMEM((TM_EXPERT, D_MODEL), F32),
                        pltpu.SemaphoreType.DMA(()), pltpu.SemaphoreType.DMA(())],
    )
    return pl.pallas_call(
        _dispatch_kernel,
        grid_spec=grid_spec,
        out_shape=jax.ShapeDtypeStruct((n_rows, D_MODEL), F32),
        compiler_params=_cparams(1),
        name="moe_dispatch",
    )(ends, sizes, h, idx(pos1), idx(pos2))


def _expert_kernel(tile_expert_ref, tile_src_ref, x_ref, g_ref, wg_ref, wu_ref, wd_ref, y_ref):
    j = pl.program_id(0)

    @pl.when(tile_src_ref[j] != j)
    def _():
        y_ref[...] = jnp.zeros_like(y_ref)

    @pl.when(tile_src_ref[j] == j)
    def _():
        xb = _rms(x_ref[...], g_ref[...]).astype(BF16)
        half = wd_ref.shape[0] // 2
        y = None
        for c0 in (0, half):
            g = _dot(xb, wg_ref[:, c0:c0 + half])
            u = _dot(xb, wu_ref[:, c0:c0 + half])
            a = (g * jax.nn.sigmoid(g) * u).astype(BF16)
            part = _dot(a, wd_ref[c0:c0 + half, :])
            y = part if y is None else y + part
        y_ref[...] = y


def _experts(xs, g, tile_expert, tile_src, we_gu, we_down):
    n_rows = xs.shape[0]
    d_e = we_down.shape[1]
    tile = lambda j, te, ts: (ts[j], 0)
    grid_spec = pltpu.PrefetchScalarGridSpec(
        num_scalar_prefetch=2,
        grid=(n_rows // TM_EXPERT,),
        in_specs=[
            pl.BlockSpec((TM_EXPERT, D_MODEL), tile),
            pl.BlockSpec((1, D_MODEL), lambda j, te, ts: (0, 0)),
            pl.BlockSpec((None, D_MODEL, d_e), lambda j, te, ts: (te[j], 0, 0)),
            pl.BlockSpec((None, D_MODEL, d_e), lambda j, te, ts: (te[j], 0, 1)),
            pl.BlockSpec((None, d_e, D_MODEL), lambda j, te, ts: (te[j], 0, 0)),
        ],
        out_specs=pl.BlockSpec((TM_EXPERT, D_MODEL), lambda j, te, ts: (j, 0)),
    )
    return pl.pallas_call(
        _expert_kernel,
        grid_spec=grid_spec,
        out_shape=jax.ShapeDtypeStruct((n_rows, D_MODEL), F32),
        compiler_params=pltpu.CompilerParams(
            dimension_semantics=("arbitrary",), vmem_limit_bytes=VMEM_LIMIT_EXPERT),
        name="moe_experts",
    )(tile_expert, tile_src, xs, g, we_gu, we_gu, we_down)


def _ple_tail(x, p_ref, g_ref, wg_ref, wp_ref, gf_ref, out_ref, final_norm):
    h = _rms(x, g_ref[...]).astype(BF16)
    gate = jax.nn.sigmoid(_dot(h, wg_ref[...]))
    y = x + gate * _dot(p_ref[...].astype(BF16), wp_ref[...])
    if final_norm:
        y = _rms(y, gf_ref[...])
    out_ref[...] = y


def _combine_ple_kernel(x_ref, meta_ref, p1_ref, p2_ref, p1_next_ref, p2_next_ref, y_ref,
                        p_ref, g_ref, wg_ref, wp_ref, gf_ref, out_ref, rows_ref, sems, *, final_norm):
    i = pl.program_id(0)
    n = pl.num_programs(0)
    tm = x_ref.shape[0]
    slot = i % 2

    def gather(p1, p2, s):
        def gather_row(r, carry):
            _row_copy(y_ref, p1[0, 0, r], rows_ref.at[s, 0], r, sems.at[s]).start(priority=0)
            _row_copy(y_ref, p2[0, 0, r], rows_ref.at[s, 1], r, sems.at[s]).start(priority=1)
            return carry
        lax.fori_loop(0, tm, gather_row, 0, unroll=8)

    @pl.when(i == 0)
    def _():
        gather(p1_ref, p2_ref, 0)

    @pl.when(i + 1 < n)
    def _():
        gather(p1_next_ref, p2_next_ref, 1 - slot)

    for k in range(2):
        pltpu.make_async_copy(y_ref.at[pl.ds(0, tm)], rows_ref.at[slot, k], sems.at[slot]).wait()
    meta = meta_ref[...]
    x = (x_ref[...] + meta[:, META_W1:META_W1 + 1] * rows_ref[slot, 0]
         + meta[:, META_W2:META_W2 + 1] * rows_ref[slot, 1])
    _ple_tail(x, p_ref, g_ref, wg_ref, wp_ref, gf_ref, out_ref, final_norm)


def _combine_ple(x, meta, pos1, pos2, y, p, g, w_gate, w_proj, g_final, final_norm, tm=256):
    t = x.shape[0]
    nb = t // tm
    ple_dim = p.shape[1]
    row = lambda i: (i, 0)
    idx = lambda a: a.reshape(nb, 1, tm)
    cur = pl.BlockSpec((1, 1, tm), lambda i: (i, 0, 0), memory_space=pltpu.SMEM)
    nxt = pl.BlockSpec((1, 1, tm), lambda i: (jnp.minimum(i + 1, nb - 1), 0, 0),
                       memory_space=pltpu.SMEM)
    return pl.pallas_call(
        functools.partial(_combine_ple_kernel, final_norm=final_norm),
        grid=(nb,),
        in_specs=[
            pl.BlockSpec((tm, D_MODEL), row),
            pl.BlockSpec((tm, LANES), row),
            cur, cur, nxt, nxt,
            pl.BlockSpec(memory_space=pl.ANY),
            pl.BlockSpec((tm, ple_dim), row),
            _resident((1, D_MODEL)),
            _resident(w_gate.shape),
            _resident(w_proj.shape),
            _resident((1, D_MODEL)),
        ],
        out_specs=pl.BlockSpec((tm, D_MODEL), row),
        out_shape=jax.ShapeDtypeStruct((t, D_MODEL), F32),
        scratch_shapes=[pltpu.VMEM((2, 2, tm, D_MODEL), F32), pltpu.SemaphoreType.DMA((2,))],
        compiler_params=_cparams(1),
        name="moe_combine_ple",
    )(x, meta, idx(pos1), idx(pos2), idx(pos1), idx(pos2), y, p, g, w_gate, w_proj, g_final)


def _moe_plan(meta, counts):
    cnt = counts[0, :N_EXPERTS].astype(jnp.int32)
    sizes = (cnt + TM_EXPERT - 1) // TM_EXPERT * TM_EXPERT
    ends = jnp.cumsum(sizes)
    starts = ends - sizes
    e1 = meta[:, META_E1].astype(jnp.int32)
    e2 = meta[:, META_E2].astype(jnp.int32)
    start_of = lambda e: jnp.sum(jnp.where(e[:, None] == jnp.arange(N_EXPERTS)[None, :],
                                           starts[None, :], 0), axis=1)
    pos1 = start_of(e1) + meta[:, META_R1].astype(jnp.int32)
    pos2 = start_of(e2) + meta[:, META_R2].astype(jnp.int32)
    n_rows = 2 * meta.shape[0] + N_EXPERTS * TM_EXPERT
    n_valid = ends[-1] // TM_EXPERT
    tile_src = jnp.minimum(jnp.arange(n_rows // TM_EXPERT, dtype=jnp.int32), n_valid - 1)
    tile_expert = jnp.sum((tile_src * TM_EXPERT)[:, None] >= ends[None, :], axis=1).astype(jnp.int32)
    return pos1, pos2, ends.astype(jnp.int32), sizes, tile_expert, tile_src, n_rows


def _rope_tables(seq):
    inv_freq = ROPE_THETA ** (-jnp.arange(0, HEAD_DIM, 2, dtype=F32) / HEAD_DIM)
    ang = jnp.arange(seq, dtype=F32)[:, None] * inv_freq[None, :]
    cos, sin = jnp.cos(ang), jnp.sin(ang)
    cos_t = jnp.concatenate([cos] * (LANES // (HEAD_DIM // 2)), axis=-1)
    sin_t = jnp.concatenate([-sin, sin] * (LANES // HEAD_DIM), axis=-1)
    return cos_t, sin_t


def kernel(x, p, g_mix, w_in, lam, g_subln, w_attn_out, conv_w, conv_b, conv_ln_g, conv_ln_b, w_conv_out, w_o, g_ffn, w_ff_gu, w_ff_down, w_router, b_router, we_gu, we_down, g_ple, w_ple_gate, w_ple_proj, g_final):
    batch, seq, _ = x.shape
    depth = w_in.shape[0]
    t = batch * seq
    cos_t, sin_t = _rope_tables(seq)
    xs = x.reshape(t, D_MODEL)
    row2 = lambda a: a.reshape(1, -1)
    expert_bf16 = {}
    w_in_bf16 = w_in[0].astype(BF16)
    for i in range(depth):
        later = {"attn_out": w_attn_out[i], "conv_out": w_conv_out[i], "w_o": w_o[i],
                 "ple_gate": w_ple_gate[i], "ple_proj": w_ple_proj[i]}
        if i % 2 == 0:
            later.update(ff_gu=w_ff_gu[i // 2], ff_down=w_ff_down[i // 2])
        if i + 1 < depth:
            later["w_in_next"] = w_in[i + 1]
        (q, k, v, z, gates), narrowed = _in_proj(xs, row2(g_mix[i]), w_in_bf16, cos_t, sin_t, seq,
                                                 tuple(later.values()))
        wb = dict(zip(later.keys(), narrowed))
        w_in_bf16 = wb.get("w_in_next")
        lam_init = 0.8 - 0.6 * math.exp(-0.3 * i)
        lf = lam[i].astype(F32)
        lam_full = (jnp.exp(jnp.sum(lf[0] * lf[1])) - jnp.exp(jnp.sum(lf[2] * lf[3]))
                    + lam_init).reshape(1)
        o = _attention(q, k, v, lam_full, row2(g_subln[i]), batch, seq, 1.0 - lam_init)
        mix_args = (xs, o, z, gates, conv_w[i], row2(conv_b[i]), row2(conv_ln_g[i]), row2(conv_ln_b[i]),
                    wb["attn_out"], wb["conv_out"], wb["w_o"], seq)
        ple_args = (p[i].reshape(t, -1), row2(g_ple[i]), wb["ple_gate"], wb["ple_proj"], row2(g_final))
        final_norm = i == depth - 1
        if i % 2 == 0:
            xs = _mix(*mix_args)
            m_next = i // 2
            to_cast = ()
            if i + 1 < depth:
                to_cast = (we_gu[m_next].reshape(-1, we_gu.shape[-1]),
                           we_down[m_next].reshape(-1, we_down.shape[-1]))
            xs, cast = _ffn_ple(xs, row2(g_ffn[i]), wb["ff_gu"], wb["ff_down"], *ple_args, final_norm, to_cast)
            if cast:
                expert_bf16[m_next] = (cast[0].reshape(we_gu.shape[1:]), cast[1].reshape(we_down.shape[1:]))
        else:
            m = i // 2
            wr = jnp.pad(w_router[m], ((0, 0), (0, LANES - N_EXPERTS)))
            wr_hi = wr.astype(BF16)
            wr_lo = (wr - wr_hi.astype(F32)).astype(BF16)
            br = jnp.pad(b_router[m], (0, LANES - N_EXPERTS)).reshape(1, LANES)
            xs, meta, counts = _mix(*mix_args, router=(row2(g_ffn[i]), wr_hi, wr_lo, br))
            pos1, pos2, ends, sizes, tile_expert, tile_src, n_rows = _moe_plan(meta, counts)
            routed = _dispatch(xs, pos1, pos2, ends, sizes, n_rows)
            if m not in expert_bf16:
                expert_bf16[m] = (we_gu[m].astype(BF16), we_down[m].astype(BF16))
            y = _experts(routed, row2(g_ffn[i]), tile_expert, tile_src, *expert_bf16[m])
            xs = _combine_ple(xs, meta, pos1, pos2, y, *ple_args, final_norm=final_norm)
    return xs.reshape(batch, seq, D_MODEL)
```

```python
import functools
import math

import jax
import jax.numpy as jnp
from jax import lax
from jax.experimental import pallas as pl
from jax.experimental.pallas import tpu as pltpu

D_MODEL = 1024
N_HEADS = 4
HEAD_DIM = 64
V_DIM = 2 * HEAD_DIM
ATTN_QK = N_HEADS * 2 * HEAD_DIM
ATTN_V = N_HEADS * V_DIM
ROPE_THETA = 10000.0
CONV_CH = 512
CONV_K = 31
CONV_PAD = (CONV_K - 1) // 2
N_EXPERTS = 8
EPS = 1e-6

LANES = 128
SUBLANES = 8
VMEM_LIMIT = 56 * 1024 * 1024
VMEM_LIMIT_EXPERT = 60 * 1024 * 1024

F32 = jnp.float32
BF16 = jnp.bfloat16


def _cparams(n_axes):
    return pltpu.CompilerParams(
        dimension_semantics=("arbitrary",) * n_axes, vmem_limit_bytes=VMEM_LIMIT)


def _resident(shape):
    nd = len(shape)
    return pl.BlockSpec(shape, lambda *_: (0,) * nd, pipeline_mode=pl.Buffered(1))


def _rms(x, g):
    return x * lax.rsqrt(jnp.mean(x * x, axis=-1, keepdims=True) + EPS) * g


def _dot(a, b):
    return jnp.dot(a, b, preferred_element_type=F32)


BF16_ROW_TILE = 2 * SUBLANES


def _cast_specs(to_cast, steps):
    in_specs, out_specs, shapes = [], [], []
    for w, layer in to_cast:
        _, rows, cols = w.shape
        n_blocks = max(n for n in range(1, steps + 1)
                       if steps % n == 0 and rows % (n * BF16_ROW_TILE) == 0)
        every = steps // n_blocks
        in_specs.append(pl.BlockSpec((None, rows // n_blocks, cols),
                                     lambda i, every=every, layer=layer: (layer, i // every, 0)))
        out_specs.append(pl.BlockSpec((rows // n_blocks, cols), lambda i, every=every: (i // every, 0)))
        shapes.append(jax.ShapeDtypeStruct((rows, cols), BF16))
    return in_specs, out_specs, shapes


def _cast_slabs(src_refs, dst_refs):
    for src, dst in zip(src_refs, dst_refs):
        dst[...] = src[...].astype(BF16)


def _rope(t, cos, sin_signed):
    lane = lax.broadcasted_iota(jnp.int32, t.shape, 1)
    first_half = (lane & (HEAD_DIM // 2)) == 0
    width = t.shape[-1]
    partner = jnp.where(first_half,
                        pltpu.roll(t, width - HEAD_DIM // 2, axis=1),
                        pltpu.roll(t, HEAD_DIM // 2, axis=1))
    reps = width // LANES
    c = jnp.concatenate([cos] * reps, axis=-1)
    s = jnp.concatenate([sin_signed] * reps, axis=-1)
    return t * c + partner * s


def _in_proj_kernel(x_ref, g_ref, w_ref, cos_ref, sin_ref, *rest, n_cast):
    cast_in, (q_ref, k_ref, v_ref, z_ref, gate_ref), cast_out = (
        rest[:n_cast], rest[n_cast:n_cast + 5], rest[n_cast + 5:])
    _cast_slabs(cast_in, cast_out)
    h = _rms(x_ref[...], g_ref[...]).astype(BF16)
    cos = cos_ref[...]
    sin = sin_ref[...]

    def proj(c0, n):
        return _dot(h, w_ref[:, c0:c0 + n])

    q = _rope(proj(0, ATTN_QK), cos, sin) * (HEAD_DIM ** -0.5)
    q_ref[...] = q.astype(BF16)
    k_ref[...] = _rope(proj(ATTN_QK, ATTN_QK), cos, sin).astype(BF16)
    v_ref[...] = proj(2 * ATTN_QK, ATTN_V).astype(BF16)
    u0 = 2 * ATTN_QK + ATTN_V
    a = proj(u0, CONV_CH)
    g = proj(u0 + CONV_CH, CONV_CH)
    z_ref[...] = a * jax.nn.sigmoid(g)
    g0 = u0 + 2 * CONV_CH
    step = 512
    for c in range(0, 2 * D_MODEL, step):
        gate_ref[:, c:c + step] = jax.nn.sigmoid(proj(g0 + c, step)).astype(BF16)


def _in_proj(x, g, w_bf, cos_t, sin_t, seq, to_cast=(), tm=512):
    t = x.shape[0]
    n_in = w_bf.shape[1]
    seq_blocks = seq // tm
    row = lambda i: (i, 0)
    tab = lambda i: (i % seq_blocks, 0)
    cast_in_specs, cast_out_specs, cast_shapes = _cast_specs(to_cast, t // tm)
    outs = pl.pallas_call(
        functools.partial(_in_proj_kernel, n_cast=len(to_cast)),
        grid=(t // tm,),
        in_specs=[
            pl.BlockSpec((tm, D_MODEL), row),
            _resident((1, D_MODEL)),
            _resident((D_MODEL, n_in)),
            pl.BlockSpec((tm, LANES), tab),
            pl.BlockSpec((tm, LANES), tab),
            *cast_in_specs,
        ],
        out_specs=[
            pl.BlockSpec((tm, ATTN_QK), row),
            pl.BlockSpec((tm, ATTN_QK), row),
            pl.BlockSpec((tm, ATTN_V), row),
            pl.BlockSpec((tm, CONV_CH), row),
            pl.BlockSpec((tm, 2 * D_MODEL), row),
            *cast_out_specs,
        ],
        out_shape=[
            jax.ShapeDtypeStruct((t, ATTN_QK), BF16),
            jax.ShapeDtypeStruct((t, ATTN_QK), BF16),
            jax.ShapeDtypeStruct((t, ATTN_V), BF16),
            jax.ShapeDtypeStruct((t, CONV_CH), F32),
            jax.ShapeDtypeStruct((t, 2 * D_MODEL), BF16),
            *cast_shapes,
        ],
        compiler_params=_cparams(1),
        name="in_proj",
    )(x, g, w_bf, cos_t, sin_t, *[w for w, _ in to_cast])
    return outs[:5], outs[5:]


_ATTN_SUB = 256


_ATTN_HEADS = 2


def _attn_kernel(lam_ref, q_ref, k_ref, v_ref, g_ref, o_ref, v1_ref, *, out_scale):
    contract_last = (((1,), (1,)), ((), ()))
    for hh in range(_ATTN_HEADS):
        head = slice(hh * V_DIM, (hh + 1) * V_DIM)
        v1_ref[hh, :, 0:V_DIM] = v_ref[:, head]
        v1_ref[hh, :, V_DIM:2 * V_DIM] = jnp.ones((v_ref.shape[0], V_DIM), BF16)
        k = k_ref[:, head]
        v1 = v1_ref[hh]

        def softmax_pv(qc):
            s = lax.dot_general(qc, k, contract_last, preferred_element_type=F32)
            p = jnp.exp(s - jnp.max(s, axis=-1, keepdims=True)).astype(BF16)
            ov = _dot(p, v1)
            return ov[:, 0:V_DIM] / ov[:, V_DIM:V_DIM + 1]

        for r0 in range(0, q_ref.shape[0], _ATTN_SUB):
            q = q_ref[r0:r0 + _ATTN_SUB, head]
            lane = lax.broadcasted_iota(jnp.int32, q.shape, 1)
            zero = jnp.zeros_like(q)
            o0 = softmax_pv(jnp.where(lane < HEAD_DIM, q, zero))
            o1 = softmax_pv(jnp.where(lane >= HEAD_DIM, q, zero))
            o = o0 - lam_ref[0] * o1
            o_ref[r0:r0 + _ATTN_SUB, head] = (_rms(o, g_ref[...]) * out_scale).astype(BF16)


def _attention(q, k, v, lam_full, g_subln, batch, seq, out_scale):
    t = q.shape[0]
    width = _ATTN_HEADS * V_DIM
    block = pl.BlockSpec((seq, width), lambda b, h, lam: (b, h))
    grid_spec = pltpu.PrefetchScalarGridSpec(
        num_scalar_prefetch=1,
        grid=(batch, N_HEADS // _ATTN_HEADS),
        in_specs=[block, block, block, pl.BlockSpec((1, V_DIM), lambda b, h, lam: (0, 0))],
        out_specs=block,
        scratch_shapes=[pltpu.VMEM((_ATTN_HEADS, seq, 2 * V_DIM), BF16)],
    )
    return pl.pallas_call(
        functools.partial(_attn_kernel, out_scale=out_scale),
        grid_spec=grid_spec,
        out_shape=jax.ShapeDtypeStruct((t, ATTN_V), BF16),
        compiler_params=_cparams(2),
        name="diff_attn",
    )(lam_full, q, k, v, g_subln)


_CONV_HALO = 16
_CONV_ROWS = 128


def _conv_block(zp_ref, w_ref, r0, c0):
    first = _CONV_HALO - CONV_PAD
    window = _CONV_ROWS + SUBLANES
    n_aligned = (CONV_K + first + SUBLANES - 1) // SUBLANES
    acc = None
    for r in reversed(range(SUBLANES)):
        y_r = None
        for a in range(n_aligned):
            tap = SUBLANES * a + r - first
            if 0 <= tap < CONV_K:
                lo = r0 + SUBLANES * a
                term = zp_ref[lo:lo + window, c0:c0 + LANES] * w_ref[tap:tap + 1, c0:c0 + LANES]
                y_r = term if y_r is None else y_r + term
        if acc is None:
            acc = y_r
        else:
            shifted = pltpu.roll(acc, window - 1, axis=0)
            acc = shifted if y_r is None else y_r + shifted
    return acc[0:_CONV_ROWS, :]


def _conv_tail(y, b_ref, lg_ref, lb_ref):
    y = y + b_ref[...]
    mu = jnp.mean(y, axis=-1, keepdims=True)
    yc = y - mu
    yn = yc * lax.rsqrt(jnp.mean(yc * yc, axis=-1, keepdims=True) + EPS) * lg_ref[...] + lb_ref[...]
    return (yn * jax.nn.sigmoid(yn)).astype(BF16)


_MIX_SUB = 256


def _mixed_residual(x_ref, o_ref, zprev_ref, z_ref, znext_ref, gate_ref, cw_ref, cb_ref, lg_ref, lb_ref,
                    wa_ref, wc_ref, wo_ref, out_ref, zp_ref, *, tiles_per_seq):
    tm = x_ref.shape[0]
    pos = pl.program_id(0) % tiles_per_seq
    zp_ref[0:_CONV_HALO, :] = jnp.where(pos > 0, zprev_ref[0], 0.0)
    zp_ref[_CONV_HALO:_CONV_HALO + tm, :] = z_ref[...]
    zp_ref[_CONV_HALO + tm:_CONV_HALO + tm + _CONV_HALO, :] = jnp.where(
        pos < tiles_per_seq - 1, znext_ref[0], 0.0)
    for r0 in range(0, tm, _MIX_SUB):
        rows = slice(r0, r0 + _MIX_SUB)
        conv = jnp.concatenate(
            [jnp.concatenate([_conv_block(zp_ref, cw_ref, r, c0) for c0 in range(0, CONV_CH, LANES)], axis=-1)
             for r in range(r0, r0 + _MIX_SUB, _CONV_ROWS)], axis=0)
        c = _conv_tail(conv, cb_ref, lg_ref, lb_ref)
        attn_d = _dot(o_ref[rows, :], wa_ref[...])
        conv_d = _dot(c, wc_ref[...])
        ga = gate_ref[rows, 0:D_MODEL].astype(F32)
        gc = gate_ref[rows, D_MODEL:2 * D_MODEL].astype(F32)
        mix = (ga * attn_d + gc * conv_d).astype(BF16)
        out_ref[rows, :] = x_ref[rows, :] + _dot(mix, wo_ref[...])


def _mix_kernel(*refs, tiles_per_seq):
    _mixed_residual(*refs, tiles_per_seq=tiles_per_seq)


def _mix_route_kernel(*refs, tiles_per_seq):
    mix_refs, (g_ref, wr_hi_ref, wr_lo_ref, br_ref) = refs[:13], refs[13:17]
    out_ref, meta_ref, plan_ref, cnt_ref, zp_ref, carry_ref = refs[17:]
    _mixed_residual(*mix_refs, out_ref, zp_ref, tiles_per_seq=tiles_per_seq)
    _route(_rms(out_ref[...], g_ref[...]), wr_hi_ref, wr_lo_ref, br_ref, meta_ref, plan_ref, cnt_ref,
           carry_ref)


def _mix(x, o, z, gates, conv_w, conv_b, ln_g, ln_b, wa, wc, wo, seq, router=None, tm=512):
    t = x.shape[0]
    row = lambda i: (i, 0)
    halo_per_tile = tm // _CONV_HALO
    n_halo = t // _CONV_HALO
    z_halo = z.reshape(n_halo, _CONV_HALO, CONV_CH)
    in_specs = [
        pl.BlockSpec((tm, D_MODEL), row),
        pl.BlockSpec((tm, ATTN_V), row),
        pl.BlockSpec((1, _CONV_HALO, CONV_CH), lambda i: (jnp.maximum(i * halo_per_tile - 1, 0), 0, 0)),
        pl.BlockSpec((tm, CONV_CH), row),
        pl.BlockSpec((1, _CONV_HALO, CONV_CH),
                     lambda i: (jnp.minimum((i + 1) * halo_per_tile, n_halo - 1), 0, 0)),
        pl.BlockSpec((tm, 2 * D_MODEL), row),
        _resident(conv_w.shape), _resident(conv_b.shape), _resident(ln_g.shape), _resident(ln_b.shape),
        _resident(wa.shape), _resident(wc.shape), _resident(wo.shape),
    ]
    args = (x, o, z_halo, z, z_halo, gates, conv_w, conv_b, ln_g, ln_b, wa, wc, wo)
    x_spec = pl.BlockSpec((tm, D_MODEL), row)
    x_shape = jax.ShapeDtypeStruct((t, D_MODEL), F32)
    zp_scratch = pltpu.VMEM((tm + 2 * _CONV_HALO, CONV_CH), F32)
    tiles_per_seq = seq // tm
    if router is None:
        return pl.pallas_call(
            functools.partial(_mix_kernel, tiles_per_seq=tiles_per_seq),
            grid=(t // tm,), in_specs=in_specs, out_specs=x_spec, out_shape=x_shape,
            scratch_shapes=[zp_scratch], compiler_params=_cparams(1), name="branch_mix",
        )(*args)
    return pl.pallas_call(
        functools.partial(_mix_route_kernel, tiles_per_seq=tiles_per_seq),
        grid=(t // tm,),
        in_specs=in_specs + [_resident(a.shape) for a in router],
        out_specs=[x_spec, pl.BlockSpec((tm, LANES), row), pl.BlockSpec((SUBLANES, tm), lambda i: (0, i)),
                   pl.BlockSpec((1, LANES), lambda i: (0, 0))],
        out_shape=[x_shape, jax.ShapeDtypeStruct((t, LANES), F32), jax.ShapeDtypeStruct((SUBLANES, t), F32),
                   jax.ShapeDtypeStruct((1, LANES), F32)],
        scratch_shapes=[zp_scratch, pltpu.VMEM((1, LANES), F32)],
        compiler_params=_cparams(1),
        name="branch_mix_route",
    )(*args, *router)


def _swiglu(h, wg_ref, wu_ref, wd_ref):
    g = _dot(h, wg_ref[...])
    u = _dot(h, wu_ref[...])
    a = (g * jax.nn.sigmoid(g) * u).astype(BF16)
    return _dot(a, wd_ref[...])


def _ffn_kernel(x_ref, g_ref, wg_ref, wu_ref, wd_ref, p_ref, gp_ref, wpg_ref, wpp_ref, gf_ref, *rest,
                n_cast, final_norm):
    cast_in, out_ref, cast_out = rest[:n_cast], rest[n_cast], rest[n_cast + 1:]
    x = x_ref[...]
    h = _rms(x, g_ref[...]).astype(BF16)
    x = x + _swiglu(h, wg_ref, wu_ref, wd_ref)
    _ple_tail(x, p_ref, gp_ref, wpg_ref, wpp_ref, gf_ref, out_ref, final_norm)
    _cast_slabs(cast_in, cast_out)


def _ffn_ple(x, g, w_gu, w_down, p, g_ple, w_gate, w_proj, g_final, final_norm, to_cast=(), tm=256):
    t = x.shape[0]
    d_ff = w_down.shape[0]
    steps = t // tm
    row = lambda i: (i, 0)
    cast_in_specs, cast_out_specs, cast_shapes = _cast_specs(to_cast, steps)
    outs = pl.pallas_call(
        functools.partial(_ffn_kernel, n_cast=len(to_cast), final_norm=final_norm),
        grid=(steps,),
        in_specs=[
            pl.BlockSpec((tm, D_MODEL), row),
            _resident((1, D_MODEL)),
            pl.BlockSpec((D_MODEL, d_ff), lambda i: (0, 0), pipeline_mode=pl.Buffered(1)),
            pl.BlockSpec((D_MODEL, d_ff), lambda i: (0, 1), pipeline_mode=pl.Buffered(1)),
            _resident((d_ff, D_MODEL)),
            pl.BlockSpec((None, tm, p[0].shape[2]), lambda i, layer=p[1]: (layer, i, 0)),
            _resident((1, D_MODEL)),
            _resident(w_gate.shape),
            _resident(w_proj.shape),
            _resident((1, D_MODEL)),
            *cast_in_specs,
        ],
        out_specs=[pl.BlockSpec((tm, D_MODEL), row), *cast_out_specs],
        out_shape=[jax.ShapeDtypeStruct((t, D_MODEL), F32), *cast_shapes],
        compiler_params=_cparams(1),
        name="dense_ffn_ple",
    )(x, g, w_gu, w_gu, w_down, p[0], g_ple, w_gate, w_proj, g_final, *[w for w, _ in to_cast])
    return outs[0], outs[1:]


TM_EXPERT = 256
META_E1, META_E2, META_R1, META_R2, META_W1, META_W2 = range(6)


def _row_copy(src_ref, src_row, dst_ref, dst_row, sem):
    return pltpu.make_async_copy(src_ref.at[pl.ds(src_row, 1)], dst_ref.at[pl.ds(dst_row, 1)], sem)


def _route(hf, wr_hi_ref, wr_lo_ref, br_ref, meta_ref, plan_ref, cnt_ref, carry_ref):
    @pl.when(pl.program_id(0) == 0)
    def _():
        carry_ref[...] = jnp.zeros_like(carry_ref)

    h_hi = hf.astype(BF16)
    h_lo = (hf - h_hi.astype(F32)).astype(BF16)
    logits = (_dot(h_hi, wr_hi_ref[...]) + _dot(h_lo, wr_hi_ref[...])
              + _dot(h_hi, wr_lo_ref[...])) + br_ref[...]
    tm = logits.shape[0]
    lane = lax.broadcasted_iota(jnp.int32, logits.shape, 1)
    neg = jnp.full_like(logits, -jnp.inf)
    logits = jnp.where(lane < N_EXPERTS, logits, neg)
    m1 = jnp.max(logits, axis=-1, keepdims=True)
    i1 = jnp.min(jnp.where(logits == m1, lane, LANES), axis=-1, keepdims=True)
    rest = jnp.where(lane == i1, neg, logits)
    m2 = jnp.max(rest, axis=-1, keepdims=True)
    i2 = jnp.min(jnp.where(rest == m2, lane, LANES), axis=-1, keepdims=True)
    e2 = jnp.exp(m2 - m1)
    denom = 1.0 + e2
    pick1 = lane == i1
    pick2 = lane == i2
    onehot = jnp.where(pick1 | pick2, 1.0, 0.0)
    r = lax.broadcasted_iota(jnp.int32, (tm, tm), 0)
    c = lax.broadcasted_iota(jnp.int32, (tm, tm), 1)
    strict_lower = jnp.where(r > c, 1.0, 0.0).astype(BF16)
    before = _dot(strict_lower, onehot.astype(BF16)) + carry_ref[...]
    rank1 = jnp.sum(jnp.where(pick1, before, 0.0), axis=-1, keepdims=True)
    rank2 = jnp.sum(jnp.where(pick2, before, 0.0), axis=-1, keepdims=True)
    record = jnp.zeros_like(logits)
    for slot, val in ((META_E1, i1.astype(F32)), (META_E2, i2.astype(F32)),
                      (META_R1, rank1), (META_R2, rank2),
                      (META_W1, 1.0 / denom), (META_W2, e2 / denom)):
        record = jnp.where(lane == slot, val, record)
    meta_ref[...] = record
    plan_ref[...] = record.T[0:SUBLANES, :]
    carry_ref[...] += jnp.sum(onehot, axis=0, keepdims=True)
    cnt_ref[...] = carry_ref[...]


def _dispatch_kernel(ends_ref, sizes_ref, h_ref, p1_ref, p2_ref, xs_ref, zero_ref, row_sem, zero_sem):
    tm = h_ref.shape[0]

    @pl.when(pl.program_id(0) == 0)
    def _():
        zero_ref[...] = jnp.zeros_like(zero_ref)
        n_tiles = xs_ref.shape[0] // TM_EXPERT
        total = ends_ref[N_EXPERTS - 1]

        def zero_tile(row):
            start = pl.multiple_of(row, TM_EXPERT)
            return pltpu.make_async_copy(zero_ref, xs_ref.at[pl.ds(start, TM_EXPERT)], zero_sem)

        fills = [(sizes_ref[e] > 0, ends_ref[e] - TM_EXPERT) for e in range(N_EXPERTS)]
        fills += [(j * TM_EXPERT >= total, j * TM_EXPERT) for j in range(n_tiles - N_EXPERTS, n_tiles)]
        for needed, row in fills:
            @pl.when(needed)
            def _():
                zero_tile(row).start()
        for needed, row in fills:
            @pl.when(needed)
            def _():
                zero_tile(row).wait()

    def scatter_row(r, carry):
        _row_copy(h_ref, r, xs_ref, p1_ref[0, 0, r], row_sem).start(priority=0)
        _row_copy(h_ref, r, xs_ref, p2_ref[0, 0, r], row_sem).start(priority=1)
        return carry

    lax.fori_loop(0, tm, scatter_row, 0, unroll=8)
    for _ in range(2):
        pltpu.make_async_copy(h_ref, xs_ref.at[pl.ds(0, tm)], row_sem).wait()


def _dispatch(h, pos1, pos2, ends, sizes, n_rows, tm=512):
    t = h.shape[0]
    idx = lambda a: a.reshape(t // tm, 1, tm)
    smem_block = pl.BlockSpec((1, 1, tm), lambda i, *_: (i, 0, 0), memory_space=pltpu.SMEM)
    grid_spec = pltpu.PrefetchScalarGridSpec(
        num_scalar_prefetch=2,
        grid=(t // tm,),
        in_specs=[pl.BlockSpec((tm, D_MODEL), lambda i, *_: (i, 0)), smem_block, smem_block],
        out_specs=pl.BlockSpec(memory_space=pl.ANY),
        scratch_shapes=[pltpu.VMEM((TM_EXPERT, D_MODEL), F32),
                        pltpu.SemaphoreType.DMA(()), pltpu.SemaphoreType.DMA(())],
    )
    return pl.pallas_call(
        _dispatch_kernel,
        grid_spec=grid_spec,
        out_shape=jax.ShapeDtypeStruct((n_rows, D_MODEL), F32),
        compiler_params=_cparams(1),
        name="moe_dispatch",
    )(ends, sizes, h, idx(pos1), idx(pos2))


def _expert_kernel(tile_expert_ref, tile_src_ref, x_ref, g_ref, wg_ref, wu_ref, wd_ref, y_ref):
    j = pl.program_id(0)

    @pl.when(tile_src_ref[j] != j)
    def _():
        y_ref[...] = jnp.zeros_like(y_ref)

    @pl.when(tile_src_ref[j] == j)
    def _():
        xb = _rms(x_ref[...], g_ref[...]).astype(BF16)
        half = wd_ref.shape[0] // 2
        y = None
        for c0 in (0, half):
            g = _dot(xb, wg_ref[:, c0:c0 + half])
            u = _dot(xb, wu_ref[:, c0:c0 + half])
            a = (g * jax.nn.sigmoid(g) * u).astype(BF16)
            part = _dot(a, wd_ref[c0:c0 + half, :])
            y = part if y is None else y + part
        y_ref[...] = y


def _experts(xs, g, tile_expert, tile_src, we_gu, we_down):
    n_rows = xs.shape[0]
    d_e = we_down.shape[1]
    tile = lambda j, te, ts: (ts[j], 0)
    grid_spec = pltpu.PrefetchScalarGridSpec(
        num_scalar_prefetch=2,
        grid=(n_rows // TM_EXPERT,),
        in_specs=[
            pl.BlockSpec((TM_EXPERT, D_MODEL), tile),
            pl.BlockSpec((1, D_MODEL), lambda j, te, ts: (0, 0)),
            pl.BlockSpec((None, D_MODEL, d_e), lambda j, te, ts: (te[j], 0, 0)),
            pl.BlockSpec((None, D_MODEL, d_e), lambda j, te, ts: (te[j], 0, 1)),
            pl.BlockSpec((None, d_e, D_MODEL), lambda j, te, ts: (te[j], 0, 0)),
        ],
        out_specs=pl.BlockSpec((TM_EXPERT, D_MODEL), lambda j, te, ts: (j, 0)),
    )
    return pl.pallas_call(
        _expert_kernel,
        grid_spec=grid_spec,
        out_shape=jax.ShapeDtypeStruct((n_rows, D_MODEL), F32),
        compiler_params=pltpu.CompilerParams(
            dimension_semantics=("arbitrary",), vmem_limit_bytes=VMEM_LIMIT_EXPERT),
        name="moe_experts",
    )(tile_expert, tile_src, xs, g, we_gu, we_gu, we_down)


def _ple_tail(x, p_ref, g_ref, wg_ref, wp_ref, gf_ref, out_ref, final_norm):
    h = _rms(x, g_ref[...]).astype(BF16)
    gate = jax.nn.sigmoid(_dot(h, wg_ref[...]))
    y = x + gate * _dot(p_ref[...].astype(BF16), wp_ref[...])
    if final_norm:
        y = _rms(y, gf_ref[...])
    out_ref[...] = y


def _combine_ple_kernel(x_ref, meta_ref, p1_ref, p2_ref, p1_next_ref, p2_next_ref, y_ref,
                        p_ref, g_ref, wg_ref, wp_ref, gf_ref, out_ref, rows_ref, sems, *, final_norm):
    i = pl.program_id(0)
    n = pl.num_programs(0)
    tm = x_ref.shape[0]
    slot = i % 2

    def gather(p1, p2, s):
        def gather_row(r, carry):
            _row_copy(y_ref, p1[0, 0, r], rows_ref.at[s, 0], r, sems.at[s]).start(priority=0)
            _row_copy(y_ref, p2[0, 0, r], rows_ref.at[s, 1], r, sems.at[s]).start(priority=1)
            return carry
        lax.fori_loop(0, tm, gather_row, 0, unroll=8)

    @pl.when(i == 0)
    def _():
        gather(p1_ref, p2_ref, 0)

    @pl.when(i + 1 < n)
    def _():
        gather(p1_next_ref, p2_next_ref, 1 - slot)

    for k in range(2):
        pltpu.make_async_copy(y_ref.at[pl.ds(0, tm)], rows_ref.at[slot, k], sems.at[slot]).wait()
    meta = meta_ref[...]
    x = (x_ref[...] + meta[:, META_W1:META_W1 + 1] * rows_ref[slot, 0]
         + meta[:, META_W2:META_W2 + 1] * rows_ref[slot, 1])
    _ple_tail(x, p_ref, g_ref, wg_ref, wp_ref, gf_ref, out_ref, final_norm)


def _combine_ple(x, meta, pos1, pos2, y, p, g, w_gate, w_proj, g_final, final_norm, tm=256):
    t = x.shape[0]
    nb = t // tm
    p_all, layer = p
    row = lambda i: (i, 0)
    idx = lambda a: a.reshape(nb, 1, tm)
    cur = pl.BlockSpec((1, 1, tm), lambda i: (i, 0, 0), memory_space=pltpu.SMEM)
    nxt = pl.BlockSpec((1, 1, tm), lambda i: (jnp.minimum(i + 1, nb - 1), 0, 0),
                       memory_space=pltpu.SMEM)
    return pl.pallas_call(
        functools.partial(_combine_ple_kernel, final_norm=final_norm),
        grid=(nb,),
        in_specs=[
            pl.BlockSpec((tm, D_MODEL), row),
            pl.BlockSpec((tm, LANES), row),
            cur, cur, nxt, nxt,
            pl.BlockSpec(memory_space=pl.ANY),
            pl.BlockSpec((None, tm, p_all.shape[2]), lambda i: (layer, i, 0)),
            _resident((1, D_MODEL)),
            _resident(w_gate.shape),
            _resident(w_proj.shape),
            _resident((1, D_MODEL)),
        ],
        out_specs=pl.BlockSpec((tm, D_MODEL), row),
        out_shape=jax.ShapeDtypeStruct((t, D_MODEL), F32),
        scratch_shapes=[pltpu.VMEM((2, 2, tm, D_MODEL), F32), pltpu.SemaphoreType.DMA((2,))],
        compiler_params=_cparams(1),
        name="moe_combine_ple",
    )(x, meta, idx(pos1), idx(pos2), idx(pos1), idx(pos2), y, p_all, g, w_gate, w_proj, g_final)


def _moe_plan(plan, counts):
    cnt = counts[0, :N_EXPERTS].astype(jnp.int32)
    sizes = (cnt + TM_EXPERT - 1) // TM_EXPERT * TM_EXPERT
    ends = jnp.cumsum(sizes)
    starts = ends - sizes

    def start_of(e):
        out = jnp.zeros_like(e)
        for j in range(N_EXPERTS):
            out = jnp.where(e == j, starts[j], out)
        return out

    pos1 = start_of(plan[META_E1].astype(jnp.int32)) + plan[META_R1].astype(jnp.int32)
    pos2 = start_of(plan[META_E2].astype(jnp.int32)) + plan[META_R2].astype(jnp.int32)
    n_rows = 2 * plan.shape[1] + N_EXPERTS * TM_EXPERT
    n_valid = ends[-1] // TM_EXPERT
    tile_src = jnp.minimum(jnp.arange(n_rows // TM_EXPERT, dtype=jnp.int32), n_valid - 1)
    tile_expert = jnp.sum((tile_src * TM_EXPERT)[:, None] >= ends[None, :], axis=1).astype(jnp.int32)
    return pos1, pos2, ends.astype(jnp.int32), sizes, tile_expert, tile_src, n_rows


def _rope_tables(seq):
    inv_freq = ROPE_THETA ** (-jnp.arange(0, HEAD_DIM, 2, dtype=F32) / HEAD_DIM)
    ang = jnp.arange(seq, dtype=F32)[:, None] * inv_freq[None, :]
    cos, sin = jnp.cos(ang), jnp.sin(ang)
    cos_t = jnp.concatenate([cos] * (LANES // (HEAD_DIM // 2)), axis=-1)
    sin_t = jnp.concatenate([-sin, sin] * (LANES // HEAD_DIM), axis=-1)
    return cos_t, sin_t


def kernel(x, p, g_mix, w_in, lam, g_subln, w_attn_out, conv_w, conv_b, conv_ln_g, conv_ln_b, w_conv_out, w_o, g_ffn, w_ff_gu, w_ff_down, w_router, b_router, we_gu, we_down, g_ple, w_ple_gate, w_ple_proj, g_final):
    batch, seq, _ = x.shape
    depth = w_in.shape[0]
    t = batch * seq
    cos_t, sin_t = _rope_tables(seq)
    xs = x.reshape(t, D_MODEL)
    row2 = lambda a: a.reshape(1, -1)
    expert_bf16 = {}
    w_in_bf16 = w_in[0].astype(BF16)
    for i in range(depth):
        later = {"attn_out": (w_attn_out, i), "conv_out": (w_conv_out, i), "w_o": (w_o, i),
                 "ple_gate": (w_ple_gate, i), "ple_proj": (w_ple_proj, i)}
        if i % 2 == 0:
            later.update(ff_gu=(w_ff_gu, i // 2), ff_down=(w_ff_down, i // 2))
        if i + 1 < depth:
            later["w_in_next"] = (w_in, i + 1)
        (q, k, v, z, gates), narrowed = _in_proj(xs, row2(g_mix[i]), w_in_bf16, cos_t, sin_t, seq,
                                                 tuple(later.values()))
        wb = dict(zip(later.keys(), narrowed))
        w_in_bf16 = wb.get("w_in_next")
        lam_init = 0.8 - 0.6 * math.exp(-0.3 * i)
        lf = lam[i].astype(F32)
        lam_full = (jnp.exp(jnp.sum(lf[0] * lf[1])) - jnp.exp(jnp.sum(lf[2] * lf[3]))
                    + lam_init).reshape(1)
        o = _attention(q, k, v, lam_full, row2(g_subln[i]), batch, seq, 1.0 - lam_init)
        mix_args = (xs, o, z, gates, conv_w[i], row2(conv_b[i]), row2(conv_ln_g[i]), row2(conv_ln_b[i]),
                    wb["attn_out"], wb["conv_out"], wb["w_o"], seq)
        ple_args = ((p.reshape(depth, t, -1), i), row2(g_ple[i]), wb["ple_gate"], wb["ple_proj"],
                    row2(g_final))
        final_norm = i == depth - 1
        if i % 2 == 0:
            xs = _mix(*mix_args)
            m_next = i // 2
            to_cast = ()
            if i + 1 < depth:
                to_cast = ((we_gu.reshape(we_gu.shape[0], -1, we_gu.shape[-1]), m_next),
                           (we_down.reshape(we_down.shape[0], -1, we_down.shape[-1]), m_next))
            xs, cast = _ffn_ple(xs, row2(g_ffn[i]), wb["ff_gu"], wb["ff_down"], *ple_args, final_norm, to_cast)
            if cast:
                expert_bf16[m_next] = (cast[0].reshape(we_gu.shape[1:]), cast[1].reshape(we_down.shape[1:]))
        else:
            m = i // 2
            wr = jnp.pad(w_router[m], ((0, 0), (0, LANES - N_EXPERTS)))
            wr_hi = wr.astype(BF16)
            wr_lo = (wr - wr_hi.astype(F32)).astype(BF16)
            br = jnp.pad(b_router[m], (0, LANES - N_EXPERTS)).reshape(1, LANES)
            xs, meta, plan, counts = _mix(*mix_args, router=(row2(g_ffn[i]), wr_hi, wr_lo, br))
            pos1, pos2, ends, sizes, tile_expert, tile_src, n_rows = _moe_plan(plan, counts)
            routed = _dispatch(xs, pos1, pos2, ends, sizes, n_rows)
            if m not in expert_bf16:
                expert_bf16[m] = (we_gu[m].astype(BF16), we_down[m].astype(BF16))
            y = _experts(routed, row2(g_ffn[i]), tile_expert, tile_src, *expert_bf16[m])
            xs = _combine_ple(xs, meta, pos1, pos2, y, *ple_args, final_norm=final_norm)
    return xs.reshape(batch, seq, D_MODEL)
```

```python
import functools
import math

import jax
import jax.numpy as jnp
from jax import lax
from jax.experimental import pallas as pl
from jax.experimental.pallas import tpu as pltpu

D_MODEL = 1024
N_HEADS = 4
HEAD_DIM = 64
V_DIM = 2 * HEAD_DIM
ATTN_QK = N_HEADS * 2 * HEAD_DIM
ATTN_V = N_HEADS * V_DIM
ROPE_THETA = 10000.0
CONV_CH = 512
CONV_K = 31
CONV_PAD = (CONV_K - 1) // 2
N_EXPERTS = 8
EPS = 1e-6

LANES = 128
SUBLANES = 8
VMEM_LIMIT = 56 * 1024 * 1024
VMEM_LIMIT_EXPERT = 60 * 1024 * 1024

F32 = jnp.float32
BF16 = jnp.bfloat16


def _cparams(n_axes):
    return pltpu.CompilerParams(
        dimension_semantics=("arbitrary",) * n_axes, vmem_limit_bytes=VMEM_LIMIT)


def _resident(shape):
    nd = len(shape)
    return pl.BlockSpec(shape, lambda *_: (0,) * nd, pipeline_mode=pl.Buffered(1))


def _rms(x, g):
    return x * lax.rsqrt(jnp.mean(x * x, axis=-1, keepdims=True) + EPS) * g


def _dot(a, b):
    return jnp.dot(a, b, preferred_element_type=F32)


BF16_ROW_TILE = 2 * SUBLANES


def _cast_specs(to_cast, steps):
    in_specs, out_specs, shapes = [], [], []
    for w, layer in to_cast:
        _, rows, cols = w.shape
        n_blocks = max(n for n in range(1, steps + 1)
                       if steps % n == 0 and rows % (n * BF16_ROW_TILE) == 0)
        every = steps // n_blocks
        in_specs.append(pl.BlockSpec((None, rows // n_blocks, cols),
                                     lambda i, every=every, layer=layer: (layer, i // every, 0)))
        out_specs.append(pl.BlockSpec((rows // n_blocks, cols), lambda i, every=every: (i // every, 0)))
        shapes.append(jax.ShapeDtypeStruct((rows, cols), BF16))
    return in_specs, out_specs, shapes


def _cast_slabs(src_refs, dst_refs):
    for src, dst in zip(src_refs, dst_refs):
        dst[...] = src[...].astype(BF16)


def _rope(t, cos, sin_signed):
    lane = lax.broadcasted_iota(jnp.int32, t.shape, 1)
    first_half = (lane & (HEAD_DIM // 2)) == 0
    width = t.shape[-1]
    partner = jnp.where(first_half,
                        pltpu.roll(t, width - HEAD_DIM // 2, axis=1),
                        pltpu.roll(t, HEAD_DIM // 2, axis=1))
    reps = width // LANES
    c = jnp.concatenate([cos] * reps, axis=-1)
    s = jnp.concatenate([sin_signed] * reps, axis=-1)
    return t * c + partner * s


def _in_proj_kernel(x_ref, g_ref, w_ref, cos_ref, sin_ref, *rest, n_cast):
    cast_in, (q_ref, k_ref, v_ref, z_ref, gate_ref), cast_out = (
        rest[:n_cast], rest[n_cast:n_cast + 5], rest[n_cast + 5:])
    _cast_slabs(cast_in, cast_out)
    h = _rms(x_ref[...], g_ref[...]).astype(BF16)
    cos = cos_ref[...]
    sin = sin_ref[...]

    def proj(c0, n):
        return _dot(h, w_ref[:, c0:c0 + n])

    q = _rope(proj(0, ATTN_QK), cos, sin) * (HEAD_DIM ** -0.5)
    q_ref[...] = q.astype(BF16)
    k_ref[...] = _rope(proj(ATTN_QK, ATTN_QK), cos, sin).astype(BF16)
    v_ref[...] = proj(2 * ATTN_QK, ATTN_V).astype(BF16)
    u0 = 2 * ATTN_QK + ATTN_V
    a = proj(u0, CONV_CH)
    g = proj(u0 + CONV_CH, CONV_CH)
    z_ref[...] = a * jax.nn.sigmoid(g)
    g0 = u0 + 2 * CONV_CH
    step = 512
    for c in range(0, 2 * D_MODEL, step):
        gate_ref[:, c:c + step] = jax.nn.sigmoid(proj(g0 + c, step)).astype(BF16)


def _in_proj(x, g, w_bf, cos_t, sin_t, seq, to_cast=(), tm=1024):
    t = x.shape[0]
    n_in = w_bf.shape[1]
    seq_blocks = seq // tm
    row = lambda i: (i, 0)
    tab = lambda i: (i % seq_blocks, 0)
    cast_in_specs, cast_out_specs, cast_shapes = _cast_specs(to_cast, t // tm)
    outs = pl.pallas_call(
        functools.partial(_in_proj_kernel, n_cast=len(to_cast)),
        grid=(t // tm,),
        in_specs=[
            pl.BlockSpec((tm, D_MODEL), row),
            _resident((1, D_MODEL)),
            _resident((D_MODEL, n_in)),
            pl.BlockSpec((tm, LANES), tab),
            pl.BlockSpec((tm, LANES), tab),
            *cast_in_specs,
        ],
        out_specs=[
            pl.BlockSpec((tm, ATTN_QK), row),
            pl.BlockSpec((tm, ATTN_QK), row),
            pl.BlockSpec((tm, ATTN_V), row),
            pl.BlockSpec((tm, CONV_CH), row),
            pl.BlockSpec((tm, 2 * D_MODEL), row),
            *cast_out_specs,
        ],
        out_shape=[
            jax.ShapeDtypeStruct((t, ATTN_QK), BF16),
            jax.ShapeDtypeStruct((t, ATTN_QK), BF16),
            jax.ShapeDtypeStruct((t, ATTN_V), BF16),
            jax.ShapeDtypeStruct((t, CONV_CH), F32),
            jax.ShapeDtypeStruct((t, 2 * D_MODEL), BF16),
            *cast_shapes,
        ],
        compiler_params=_cparams(1),
        name="in_proj",
    )(x, g, w_bf, cos_t, sin_t, *[w for w, _ in to_cast])
    return outs[:5], outs[5:]


_ATTN_SUB = 256


_ATTN_HEADS = 2


def _attn_kernel(lam_ref, q_ref, k_ref, v_ref, g_ref, o_ref, v1_ref, *, out_scale):
    contract_last = (((1,), (1,)), ((), ()))
    for hh in range(_ATTN_HEADS):
        head = slice(hh * V_DIM, (hh + 1) * V_DIM)
        v1_ref[hh, :, 0:V_DIM] = v_ref[:, head]
        v1_ref[hh, :, V_DIM:2 * V_DIM] = jnp.ones((v_ref.shape[0], V_DIM), BF16)
        k = k_ref[:, head]
        v1 = v1_ref[hh]

        def softmax_pv(qc):
            s = lax.dot_general(qc, k, contract_last, preferred_element_type=F32)
            p = jnp.exp(s - jnp.max(s, axis=-1, keepdims=True)).astype(BF16)
            ov = _dot(p, v1)
            return ov[:, 0:V_DIM] / ov[:, V_DIM:V_DIM + 1]

        for r0 in range(0, q_ref.shape[0], _ATTN_SUB):
            q = q_ref[r0:r0 + _ATTN_SUB, head]
            lane = lax.broadcasted_iota(jnp.int32, q.shape, 1)
            zero = jnp.zeros_like(q)
            o0 = softmax_pv(jnp.where(lane < HEAD_DIM, q, zero))
            o1 = softmax_pv(jnp.where(lane >= HEAD_DIM, q, zero))
            o = o0 - lam_ref[0] * o1
            o_ref[r0:r0 + _ATTN_SUB, head] = (_rms(o, g_ref[...]) * out_scale).astype(BF16)


def _attention(q, k, v, lam_full, g_subln, batch, seq, out_scale):
    t = q.shape[0]
    width = _ATTN_HEADS * V_DIM
    block = pl.BlockSpec((seq, width), lambda b, h, lam: (b, h))
    grid_spec = pltpu.PrefetchScalarGridSpec(
        num_scalar_prefetch=1,
        grid=(batch, N_HEADS // _ATTN_HEADS),
        in_specs=[block, block, block, pl.BlockSpec((1, V_DIM), lambda b, h, lam: (0, 0))],
        out_specs=block,
        scratch_shapes=[pltpu.VMEM((_ATTN_HEADS, seq, 2 * V_DIM), BF16)],
    )
    return pl.pallas_call(
        functools.partial(_attn_kernel, out_scale=out_scale),
        grid_spec=grid_spec,
        out_shape=jax.ShapeDtypeStruct((t, ATTN_V), BF16),
        compiler_params=_cparams(2),
        name="diff_attn",
    )(lam_full, q, k, v, g_subln)


_CONV_HALO = 16
_CONV_ROWS = 128


def _conv_block(zp_ref, w_ref, r0, c0):
    first = _CONV_HALO - CONV_PAD
    window = _CONV_ROWS + SUBLANES
    n_aligned = (CONV_K + first + SUBLANES - 1) // SUBLANES
    acc = None
    for r in reversed(range(SUBLANES)):
        y_r = None
        for a in range(n_aligned):
            tap = SUBLANES * a + r - first
            if 0 <= tap < CONV_K:
                lo = r0 + SUBLANES * a
                term = zp_ref[lo:lo + window, c0:c0 + LANES] * w_ref[tap:tap + 1, c0:c0 + LANES]
                y_r = term if y_r is None else y_r + term
        if acc is None:
            acc = y_r
        else:
            shifted = pltpu.roll(acc, window - 1, axis=0)
            acc = shifted if y_r is None else y_r + shifted
    return acc[0:_CONV_ROWS, :]


def _conv_tail(y, b_ref, lg_ref, lb_ref):
    y = y + b_ref[...]
    mu = jnp.mean(y, axis=-1, keepdims=True)
    yc = y - mu
    yn = yc * lax.rsqrt(jnp.mean(yc * yc, axis=-1, keepdims=True) + EPS) * lg_ref[...] + lb_ref[...]
    return (yn * jax.nn.sigmoid(yn)).astype(BF16)


_MIX_SUB = 256


def _mixed_residual(x_ref, o_ref, zprev_ref, z_ref, znext_ref, gate_ref, cw_ref, cb_ref, lg_ref, lb_ref,
                    wa_ref, wc_ref, wo_ref, out_ref, zp_ref, *, tiles_per_seq):
    tm = x_ref.shape[0]
    pos = pl.program_id(0) % tiles_per_seq
    zp_ref[0:_CONV_HALO, :] = jnp.where(pos > 0, zprev_ref[0], 0.0)
    zp_ref[_CONV_HALO:_CONV_HALO + tm, :] = z_ref[...]
    zp_ref[_CONV_HALO + tm:_CONV_HALO + tm + _CONV_HALO, :] = jnp.where(
        pos < tiles_per_seq - 1, znext_ref[0], 0.0)
    for r0 in range(0, tm, _MIX_SUB):
        rows = slice(r0, r0 + _MIX_SUB)
        conv = jnp.concatenate(
            [jnp.concatenate([_conv_block(zp_ref, cw_ref, r, c0) for c0 in range(0, CONV_CH, LANES)], axis=-1)
             for r in range(r0, r0 + _MIX_SUB, _CONV_ROWS)], axis=0)
        c = _conv_tail(conv, cb_ref, lg_ref, lb_ref)
        attn_d = _dot(o_ref[rows, :], wa_ref[...])
        conv_d = _dot(c, wc_ref[...])
        ga = gate_ref[rows, 0:D_MODEL].astype(F32)
        gc = gate_ref[rows, D_MODEL:2 * D_MODEL].astype(F32)
        mix = (ga * attn_d + gc * conv_d).astype(BF16)
        out_ref[rows, :] = x_ref[rows, :] + _dot(mix, wo_ref[...])


def _mix_kernel(*refs, tiles_per_seq):
    _mixed_residual(*refs, tiles_per_seq=tiles_per_seq)


def _mix_route_kernel(*refs, tiles_per_seq):
    mix_refs, (g_ref, wr_hi_ref, wr_lo_ref, br_ref) = refs[:13], refs[13:17]
    out_ref, meta_ref, plan_ref, cnt_ref, zp_ref, carry_ref = refs[17:]
    _mixed_residual(*mix_refs, out_ref, zp_ref, tiles_per_seq=tiles_per_seq)
    _route(_rms(out_ref[...], g_ref[...]), wr_hi_ref, wr_lo_ref, br_ref, meta_ref, plan_ref, cnt_ref,
           carry_ref)


def _mix(x, o, z, gates, conv_w, conv_b, ln_g, ln_b, wa, wc, wo, seq, router=None, tm=512):
    t = x.shape[0]
    row = lambda i: (i, 0)
    halo_per_tile = tm // _CONV_HALO
    n_halo = t // _CONV_HALO
    z_halo = z.reshape(n_halo, _CONV_HALO, CONV_CH)
    in_specs = [
        pl.BlockSpec((tm, D_MODEL), row),
        pl.BlockSpec((tm, ATTN_V), row),
        pl.BlockSpec((1, _CONV_HALO, CONV_CH), lambda i: (jnp.maximum(i * halo_per_tile - 1, 0), 0, 0)),
        pl.BlockSpec((tm, CONV_CH), row),
        pl.BlockSpec((1, _CONV_HALO, CONV_CH),
                     lambda i: (jnp.minimum((i + 1) * halo_per_tile, n_halo - 1), 0, 0)),
        pl.BlockSpec((tm, 2 * D_MODEL), row),
        _resident(conv_w.shape), _resident(conv_b.shape), _resident(ln_g.shape), _resident(ln_b.shape),
        _resident(wa.shape), _resident(wc.shape), _resident(wo.shape),
    ]
    args = (x, o, z_halo, z, z_halo, gates, conv_w, conv_b, ln_g, ln_b, wa, wc, wo)
    x_spec = pl.BlockSpec((tm, D_MODEL), row)
    x_shape = jax.ShapeDtypeStruct((t, D_MODEL), F32)
    zp_scratch = pltpu.VMEM((tm + 2 * _CONV_HALO, CONV_CH), F32)
    tiles_per_seq = seq // tm
    if router is None:
        return pl.pallas_call(
            functools.partial(_mix_kernel, tiles_per_seq=tiles_per_seq),
            grid=(t // tm,), in_specs=in_specs, out_specs=x_spec, out_shape=x_shape,
            scratch_shapes=[zp_scratch], compiler_params=_cparams(1), name="branch_mix",
        )(*args)
    return pl.pallas_call(
        functools.partial(_mix_route_kernel, tiles_per_seq=tiles_per_seq),
        grid=(t // tm,),
        in_specs=in_specs + [_resident(a.shape) for a in router],
        out_specs=[x_spec, pl.BlockSpec((tm, LANES), row), pl.BlockSpec((SUBLANES, tm), lambda i: (0, i)),
                   pl.BlockSpec((1, LANES), lambda i: (0, 0))],
        out_shape=[x_shape, jax.ShapeDtypeStruct((t, LANES), F32), jax.ShapeDtypeStruct((SUBLANES, t), F32),
                   jax.ShapeDtypeStruct((1, LANES), F32)],
        scratch_shapes=[zp_scratch, pltpu.VMEM((1, LANES), F32)],
        compiler_params=_cparams(1),
        name="branch_mix_route",
    )(*args, *router)


def _swiglu(h, wg_ref, wu_ref, wd_ref):
    g = _dot(h, wg_ref[...])
    u = _dot(h, wu_ref[...])
    a = (g * jax.nn.sigmoid(g) * u).astype(BF16)
    return _dot(a, wd_ref[...])


def _ffn_kernel(x_ref, g_ref, wg_ref, wu_ref, wd_ref, p_ref, gp_ref, wpg_ref, wpp_ref, gf_ref, *rest,
                n_cast, final_norm):
    cast_in, out_ref, cast_out = rest[:n_cast], rest[n_cast], rest[n_cast + 1:]
    x = x_ref[...]
    h = _rms(x, g_ref[...]).astype(BF16)
    x = x + _swiglu(h, wg_ref, wu_ref, wd_ref)
    _ple_tail(x, p_ref, gp_ref, wpg_ref, wpp_ref, gf_ref, out_ref, final_norm)
    _cast_slabs(cast_in, cast_out)


def _ffn_ple(x, g, w_gu, w_down, p, g_ple, w_gate, w_proj, g_final, final_norm, to_cast=(), tm=256):
    t = x.shape[0]
    d_ff = w_down.shape[0]
    steps = t // tm
    row = lambda i: (i, 0)
    cast_in_specs, cast_out_specs, cast_shapes = _cast_specs(to_cast, steps)
    outs = pl.pallas_call(
        functools.partial(_ffn_kernel, n_cast=len(to_cast), final_norm=final_norm),
        grid=(steps,),
        in_specs=[
            pl.BlockSpec((tm, D_MODEL), row),
            _resident((1, D_MODEL)),
            pl.BlockSpec((D_MODEL, d_ff), lambda i: (0, 0), pipeline_mode=pl.Buffered(1)),
            pl.BlockSpec((D_MODEL, d_ff), lambda i: (0, 1), pipeline_mode=pl.Buffered(1)),
            _resident((d_ff, D_MODEL)),
            pl.BlockSpec((None, tm, p[0].shape[2]), lambda i, layer=p[1]: (layer, i, 0)),
            _resident((1, D_MODEL)),
            _resident(w_gate.shape),
            _resident(w_proj.shape),
            _resident((1, D_MODEL)),
            *cast_in_specs,
        ],
        out_specs=[pl.BlockSpec((tm, D_MODEL), row), *cast_out_specs],
        out_shape=[jax.ShapeDtypeStruct((t, D_MODEL), F32), *cast_shapes],
        compiler_params=_cparams(1),
        name="dense_ffn_ple",
    )(x, g, w_gu, w_gu, w_down, p[0], g_ple, w_gate, w_proj, g_final, *[w for w, _ in to_cast])
    return outs[0], outs[1:]


TM_EXPERT = 256
META_E1, META_E2, META_R1, META_R2, META_W1, META_W2 = range(6)


def _row_copy(src_ref, src_row, dst_ref, dst_row, sem):
    return pltpu.make_async_copy(src_ref.at[pl.ds(src_row, 1)], dst_ref.at[pl.ds(dst_row, 1)], sem)


def _route(hf, wr_hi_ref, wr_lo_ref, br_ref, meta_ref, plan_ref, cnt_ref, carry_ref):
    @pl.when(pl.program_id(0) == 0)
    def _():
        carry_ref[...] = jnp.zeros_like(carry_ref)

    h_hi = hf.astype(BF16)
    h_lo = (hf - h_hi.astype(F32)).astype(BF16)
    logits = (_dot(h_hi, wr_hi_ref[...]) + _dot(h_lo, wr_hi_ref[...])
              + _dot(h_hi, wr_lo_ref[...])) + br_ref[...]
    tm = logits.shape[0]
    lane = lax.broadcasted_iota(jnp.int32, logits.shape, 1)
    neg = jnp.full_like(logits, -jnp.inf)
    logits = jnp.where(lane < N_EXPERTS, logits, neg)
    m1 = jnp.max(logits, axis=-1, keepdims=True)
    i1 = jnp.min(jnp.where(logits == m1, lane, LANES), axis=-1, keepdims=True)
    rest = jnp.where(lane == i1, neg, logits)
    m2 = jnp.max(rest, axis=-1, keepdims=True)
    i2 = jnp.min(jnp.where(rest == m2, lane, LANES), axis=-1, keepdims=True)
    e2 = jnp.exp(m2 - m1)
    denom = 1.0 + e2
    pick1 = lane == i1
    pick2 = lane == i2
    onehot = jnp.where(pick1 | pick2, 1.0, 0.0)
    r = lax.broadcasted_iota(jnp.int32, (tm, tm), 0)
    c = lax.broadcasted_iota(jnp.int32, (tm, tm), 1)
    strict_lower = jnp.where(r > c, 1.0, 0.0).astype(BF16)
    before = _dot(strict_lower, onehot.astype(BF16)) + carry_ref[...]
    rank1 = jnp.sum(jnp.where(pick1, before, 0.0), axis=-1, keepdims=True)
    rank2 = jnp.sum(jnp.where(pick2, before, 0.0), axis=-1, keepdims=True)
    record = jnp.zeros_like(logits)
    for slot, val in ((META_E1, i1.astype(F32)), (META_E2, i2.astype(F32)),
                      (META_R1, rank1), (META_R2, rank2),
                      (META_W1, 1.0 / denom), (META_W2, e2 / denom)):
        record = jnp.where(lane == slot, val, record)
    meta_ref[...] = record
    plan_ref[...] = record.T[0:SUBLANES, :]
    carry_ref[...] += jnp.sum(onehot, axis=0, keepdims=True)
    cnt_ref[...] = carry_ref[...]


def _dispatch_kernel(ends_ref, sizes_ref, h_ref, p1_ref, p2_ref, xs_ref, zero_ref, row_sem, zero_sem):
    tm = h_ref.shape[0]

    @pl.when(pl.program_id(0) == 0)
    def _():
        zero_ref[...] = jnp.zeros_like(zero_ref)
        n_tiles = xs_ref.shape[0] // TM_EXPERT
        total = ends_ref[N_EXPERTS - 1]

        def zero_tile(row):
            start = pl.multiple_of(row, TM_EXPERT)
            return pltpu.make_async_copy(zero_ref, xs_ref.at[pl.ds(start, TM_EXPERT)], zero_sem)

        fills = [(sizes_ref[e] > 0, ends_ref[e] - TM_EXPERT) for e in range(N_EXPERTS)]
        fills += [(j * TM_EXPERT >= total, j * TM_EXPERT) for j in range(n_tiles - N_EXPERTS, n_tiles)]
        for needed, row in fills:
            @pl.when(needed)
            def _():
                zero_tile(row).start()
        for needed, row in fills:
            @pl.when(needed)
            def _():
                zero_tile(row).wait()

    def scatter_row(r, carry):
        _row_copy(h_ref, r, xs_ref, p1_ref[0, 0, r], row_sem).start(priority=0)
        _row_copy(h_ref, r, xs_ref, p2_ref[0, 0, r], row_sem).start(priority=1)
        return carry

    lax.fori_loop(0, tm, scatter_row, 0, unroll=True)
    for _ in range(2):
        pltpu.make_async_copy(h_ref, xs_ref.at[pl.ds(0, tm)], row_sem).wait()


def _dispatch(h, pos1, pos2, ends, sizes, n_rows, tm=512):
    t = h.shape[0]
    idx = lambda a: a.reshape(t // tm, 1, tm)
    smem_block = pl.BlockSpec((1, 1, tm), lambda i, *_: (i, 0, 0), memory_space=pltpu.SMEM)
    grid_spec = pltpu.PrefetchScalarGridSpec(
        num_scalar_prefetch=2,
        grid=(t // tm,),
        in_specs=[pl.BlockSpec((tm, D_MODEL), lambda i, *_: (i, 0)), smem_block, smem_block],
        out_specs=pl.BlockSpec(memory_space=pl.ANY),
        scratch_shapes=[pltpu.VMEM((TM_EXPERT, D_MODEL), F32),
                        pltpu.SemaphoreType.DMA(()), pltpu.SemaphoreType.DMA(())],
    )
    return pl.pallas_call(
        _dispatch_kernel,
        grid_spec=grid_spec,
        out_shape=jax.ShapeDtypeStruct((n_rows, D_MODEL), F32),
        compiler_params=_cparams(1),
        name="moe_dispatch",
    )(ends, sizes, h, idx(pos1), idx(pos2))


def _expert_kernel(tile_expert_ref, tile_src_ref, x_ref, g_ref, wg_ref, wu_ref, wd_ref, y_ref):
    j = pl.program_id(0)

    @pl.when(tile_src_ref[j] != j)
    def _():
        y_ref[...] = jnp.zeros_like(y_ref)

    @pl.when(tile_src_ref[j] == j)
    def _():
        xb = _rms(x_ref[...], g_ref[...]).astype(BF16)
        half = wd_ref.shape[0] // 2
        y = None
        for c0 in (0, half):
            g = _dot(xb, wg_ref[:, c0:c0 + half])
            u = _dot(xb, wu_ref[:, c0:c0 + half])
            a = (g * jax.nn.sigmoid(g) * u).astype(BF16)
            part = _dot(a, wd_ref[c0:c0 + half, :])
            y = part if y is None else y + part
        y_ref[...] = y


def _experts(xs, g, tile_expert, tile_src, we_gu, we_down):
    n_rows = xs.shape[0]
    d_e = we_down.shape[1]
    tile = lambda j, te, ts: (ts[j], 0)
    grid_spec = pltpu.PrefetchScalarGridSpec(
        num_scalar_prefetch=2,
        grid=(n_rows // TM_EXPERT,),
        in_specs=[
            pl.BlockSpec((TM_EXPERT, D_MODEL), tile),
            pl.BlockSpec((1, D_MODEL), lambda j, te, ts: (0, 0)),
            pl.BlockSpec((None, D_MODEL, d_e), lambda j, te, ts: (te[j], 0, 0)),
            pl.BlockSpec((None, D_MODEL, d_e), lambda j, te, ts: (te[j], 0, 1)),
            pl.BlockSpec((None, d_e, D_MODEL), lambda j, te, ts: (te[j], 0, 0)),
        ],
        out_specs=pl.BlockSpec((TM_EXPERT, D_MODEL), lambda j, te, ts: (j, 0)),
    )
    return pl.pallas_call(
        _expert_kernel,
        grid_spec=grid_spec,
        out_shape=jax.ShapeDtypeStruct((n_rows, D_MODEL), F32),
        compiler_params=pltpu.CompilerParams(
            dimension_semantics=("arbitrary",), vmem_limit_bytes=VMEM_LIMIT_EXPERT),
        name="moe_experts",
    )(tile_expert, tile_src, xs, g, we_gu, we_gu, we_down)


def _ple_tail(x, p_ref, g_ref, wg_ref, wp_ref, gf_ref, out_ref, final_norm):
    h = _rms(x, g_ref[...]).astype(BF16)
    gate = jax.nn.sigmoid(_dot(h, wg_ref[...]))
    y = x + gate * _dot(p_ref[...].astype(BF16), wp_ref[...])
    if final_norm:
        y = _rms(y, gf_ref[...])
    out_ref[...] = y


def _combine_ple_kernel(x_ref, meta_ref, p1_ref, p2_ref, p1_next_ref, p2_next_ref, y_ref,
                        p_ref, g_ref, wg_ref, wp_ref, gf_ref, out_ref, rows_ref, sems, *, final_norm):
    i = pl.program_id(0)
    n = pl.num_programs(0)
    tm = x_ref.shape[0]
    slot = i % 2

    def gather(p1, p2, s):
        def gather_row(r, carry):
            _row_copy(y_ref, p1[0, 0, r], rows_ref.at[s, 0], r, sems.at[s]).start(priority=0)
            _row_copy(y_ref, p2[0, 0, r], rows_ref.at[s, 1], r, sems.at[s]).start(priority=1)
            return carry
        lax.fori_loop(0, tm, gather_row, 0, unroll=True)

    @pl.when(i == 0)
    def _():
        gather(p1_ref, p2_ref, 0)

    @pl.when(i + 1 < n)
    def _():
        gather(p1_next_ref, p2_next_ref, 1 - slot)

    for k in range(2):
        pltpu.make_async_copy(y_ref.at[pl.ds(0, tm)], rows_ref.at[slot, k], sems.at[slot]).wait()
    meta = meta_ref[...]
    x = (x_ref[...] + meta[:, META_W1:META_W1 + 1] * rows_ref[slot, 0]
         + meta[:, META_W2:META_W2 + 1] * rows_ref[slot, 1])
    _ple_tail(x, p_ref, g_ref, wg_ref, wp_ref, gf_ref, out_ref, final_norm)


def _combine_ple(x, meta, pos1, pos2, y, p, g, w_gate, w_proj, g_final, final_norm, tm=256):
    t = x.shape[0]
    nb = t // tm
    p_all, layer = p
    row = lambda i: (i, 0)
    idx = lambda a: a.reshape(nb, 1, tm)
    cur = pl.BlockSpec((1, 1, tm), lambda i: (i, 0, 0), memory_space=pltpu.SMEM)
    nxt = pl.BlockSpec((1, 1, tm), lambda i: (jnp.minimum(i + 1, nb - 1), 0, 0),
                       memory_space=pltpu.SMEM)
    return pl.pallas_call(
        functools.partial(_combine_ple_kernel, final_norm=final_norm),
        grid=(nb,),
        in_specs=[
            pl.BlockSpec((tm, D_MODEL), row),
            pl.BlockSpec((tm, LANES), row),
            cur, cur, nxt, nxt,
            pl.BlockSpec(memory_space=pl.ANY),
            pl.BlockSpec((None, tm, p_all.shape[2]), lambda i: (layer, i, 0)),
            _resident((1, D_MODEL)),
            _resident(w_gate.shape),
            _resident(w_proj.shape),
            _resident((1, D_MODEL)),
        ],
        out_specs=pl.BlockSpec((tm, D_MODEL), row),
        out_shape=jax.ShapeDtypeStruct((t, D_MODEL), F32),
        scratch_shapes=[pltpu.VMEM((2, 2, tm, D_MODEL), F32), pltpu.SemaphoreType.DMA((2,))],
        compiler_params=_cparams(1),
        name="moe_combine_ple",
    )(x, meta, idx(pos1), idx(pos2), idx(pos1), idx(pos2), y, p_all, g, w_gate, w_proj, g_final)


def _moe_plan(plan, counts):
    cnt = counts[0, :N_EXPERTS].astype(jnp.int32)
    sizes = (cnt + TM_EXPERT - 1) // TM_EXPERT * TM_EXPERT
    ends = jnp.cumsum(sizes)
    starts = ends - sizes

    def start_of(e):
        out = jnp.zeros_like(e)
        for j in range(N_EXPERTS):
            out = jnp.where(e == j, starts[j], out)
        return out

    pos1 = start_of(plan[META_E1].astype(jnp.int32)) + plan[META_R1].astype(jnp.int32)
    pos2 = start_of(plan[META_E2].astype(jnp.int32)) + plan[META_R2].astype(jnp.int32)
    n_rows = 2 * plan.shape[1] + N_EXPERTS * TM_EXPERT
    n_valid = ends[-1] // TM_EXPERT
    tile_src = jnp.minimum(jnp.arange(n_rows // TM_EXPERT, dtype=jnp.int32), n_valid - 1)
    tile_expert = jnp.sum((tile_src * TM_EXPERT)[:, None] >= ends[None, :], axis=1).astype(jnp.int32)
    return pos1, pos2, ends.astype(jnp.int32), sizes, tile_expert, tile_src, n_rows


def _rope_tables(seq):
    inv_freq = ROPE_THETA ** (-jnp.arange(0, HEAD_DIM, 2, dtype=F32) / HEAD_DIM)
    ang = jnp.arange(seq, dtype=F32)[:, None] * inv_freq[None, :]
    cos, sin = jnp.cos(ang), jnp.sin(ang)
    cos_t = jnp.concatenate([cos] * (LANES // (HEAD_DIM // 2)), axis=-1)
    sin_t = jnp.concatenate([-sin, sin] * (LANES // HEAD_DIM), axis=-1)
    return cos_t, sin_t


def kernel(x, p, g_mix, w_in, lam, g_subln, w_attn_out, conv_w, conv_b, conv_ln_g, conv_ln_b, w_conv_out, w_o, g_ffn, w_ff_gu, w_ff_down, w_router, b_router, we_gu, we_down, g_ple, w_ple_gate, w_ple_proj, g_final):
    batch, seq, _ = x.shape
    depth = w_in.shape[0]
    t = batch * seq
    cos_t, sin_t = _rope_tables(seq)
    xs = x.reshape(t, D_MODEL)
    row2 = lambda a: a.reshape(1, -1)
    expert_bf16 = {}
    w_in_bf16 = w_in[0].astype(BF16)
    for i in range(depth):
        later = {"attn_out": (w_attn_out, i), "conv_out": (w_conv_out, i), "w_o": (w_o, i),
                 "ple_gate": (w_ple_gate, i), "ple_proj": (w_ple_proj, i)}
        if i % 2 == 0:
            later.update(ff_gu=(w_ff_gu, i // 2), ff_down=(w_ff_down, i // 2))
        if i + 1 < depth:
            later["w_in_next"] = (w_in, i + 1)
        (q, k, v, z, gates), narrowed = _in_proj(xs, row2(g_mix[i]), w_in_bf16, cos_t, sin_t, seq,
                                                 tuple(later.values()))
        wb = dict(zip(later.keys(), narrowed))
        w_in_bf16 = wb.get("w_in_next")
        lam_init = 0.8 - 0.6 * math.exp(-0.3 * i)
        lf = lam[i].astype(F32)
        lam_full = (jnp.exp(jnp.sum(lf[0] * lf[1])) - jnp.exp(jnp.sum(lf[2] * lf[3]))
                    + lam_init).reshape(1)
        o = _attention(q, k, v, lam_full, row2(g_subln[i]), batch, seq, 1.0 - lam_init)
        mix_args = (xs, o, z, gates, conv_w[i], row2(conv_b[i]), row2(conv_ln_g[i]), row2(conv_ln_b[i]),
                    wb["attn_out"], wb["conv_out"], wb["w_o"], seq)
        ple_args = ((p.reshape(depth, t, -1), i), row2(g_ple[i]), wb["ple_gate"], wb["ple_proj"],
                    row2(g_final))
        final_norm = i == depth - 1
        if i % 2 == 0:
            xs = _mix(*mix_args)
            m_next = i // 2
            to_cast = ()
            if i + 1 < depth:
                to_cast = ((we_gu.reshape(we_gu.shape[0], -1, we_gu.shape[-1]), m_next),
                           (we_down.reshape(we_down.shape[0], -1, we_down.shape[-1]), m_next))
            xs, cast = _ffn_ple(xs, row2(g_ffn[i]), wb["ff_gu"], wb["ff_down"], *ple_args, final_norm, to_cast)
            if cast:
                expert_bf16[m_next] = (cast[0].reshape(we_gu.shape[1:]), cast[1].reshape(we_down.shape[1:]))
        else:
            m = i // 2
            wr = jnp.pad(w_router[m], ((0, 0), (0, LANES - N_EXPERTS)))
            wr_hi = wr.astype(BF16)
            wr_lo = (wr - wr_hi.astype(F32)).astype(BF16)
            br = jnp.pad(b_router[m], (0, LANES - N_EXPERTS)).reshape(1, LANES)
            xs, meta, plan, counts = _mix(*mix_args, router=(row2(g_ffn[i]), wr_hi, wr_lo, br))
            pos1, pos2, ends, sizes, tile_expert, tile_src, n_rows = _moe_plan(plan, counts)
            routed = _dispatch(xs, pos1, pos2, ends, sizes, n_rows)
            if m not in expert_bf16:
                expert_bf16[m] = (we_gu[m].astype(BF16), we_down[m].astype(BF16))
            y = _experts(routed, row2(g_ffn[i]), tile_expert, tile_src, *expert_bf16[m])
            xs = _combine_ple(xs, meta, pos1, pos2, y, *ple_args, final_norm=final_norm)
    return xs.reshape(batch, seq, D_MODEL)
```

```python
import functools
import math

import jax
import jax.numpy as jnp
from jax import lax
from jax.experimental import pallas as pl
from jax.experimental.pallas import tpu as pltpu

D_MODEL = 1024
N_HEADS = 4
HEAD_DIM = 64
V_DIM = 2 * HEAD_DIM
ATTN_QK = N_HEADS * 2 * HEAD_DIM
ATTN_V = N_HEADS * V_DIM
ROPE_THETA = 10000.0
CONV_CH = 512
CONV_K = 31
CONV_PAD = (CONV_K - 1) // 2
N_EXPERTS = 8
EPS = 1e-6

LANES = 128
SUBLANES = 8
VMEM_LIMIT = 56 * 1024 * 1024
VMEM_LIMIT_EXPERT = 60 * 1024 * 1024

F32 = jnp.float32
BF16 = jnp.bfloat16


def _cparams(n_axes):
    return pltpu.CompilerParams(
        dimension_semantics=("arbitrary",) * n_axes, vmem_limit_bytes=VMEM_LIMIT)


def _resident(shape):
    nd = len(shape)
    return pl.BlockSpec(shape, lambda *_: (0,) * nd, pipeline_mode=pl.Buffered(1))


def _rms(x, g):
    return x * lax.rsqrt(jnp.mean(x * x, axis=-1, keepdims=True) + EPS) * g


def _dot(a, b):
    return jnp.dot(a, b, preferred_element_type=F32)


BF16_ROW_TILE = 2 * SUBLANES


def _cast_specs(to_cast, steps):
    in_specs, out_specs, shapes = [], [], []
    for w, layer in to_cast:
        _, rows, cols = w.shape
        n_blocks = max(n for n in range(1, steps + 1)
                       if steps % n == 0 and rows % (n * BF16_ROW_TILE) == 0)
        every = steps // n_blocks
        in_specs.append(pl.BlockSpec((None, rows // n_blocks, cols),
                                     lambda i, every=every, layer=layer: (layer, i // every, 0)))
        out_specs.append(pl.BlockSpec((rows // n_blocks, cols), lambda i, every=every: (i // every, 0)))
        shapes.append(jax.ShapeDtypeStruct((rows, cols), BF16))
    return in_specs, out_specs, shapes


def _cast_slabs(src_refs, dst_refs):
    for src, dst in zip(src_refs, dst_refs):
        dst[...] = src[...].astype(BF16)


def _rope(t, cos, sin_signed):
    lane = lax.broadcasted_iota(jnp.int32, t.shape, 1)
    first_half = (lane & (HEAD_DIM // 2)) == 0
    width = t.shape[-1]
    partner = jnp.where(first_half,
                        pltpu.roll(t, width - HEAD_DIM // 2, axis=1),
                        pltpu.roll(t, HEAD_DIM // 2, axis=1))
    reps = width // LANES
    c = jnp.concatenate([cos] * reps, axis=-1)
    s = jnp.concatenate([sin_signed] * reps, axis=-1)
    return t * c + partner * s


def _in_proj_kernel(x_ref, g_ref, w_ref, cos_ref, sin_ref, *rest, n_cast):
    cast_in, (q_ref, k_ref, v_ref, z_ref, gate_ref), cast_out = (
        rest[:n_cast], rest[n_cast:n_cast + 5], rest[n_cast + 5:])
    _cast_slabs(cast_in, cast_out)
    h = _rms(x_ref[...], g_ref[...]).astype(BF16)
    cos = cos_ref[...]
    sin = sin_ref[...]

    def proj(c0, n):
        return _dot(h, w_ref[:, c0:c0 + n])

    q = _rope(proj(0, ATTN_QK), cos, sin) * (HEAD_DIM ** -0.5)
    q_ref[...] = q.astype(BF16)
    k_ref[...] = _rope(proj(ATTN_QK, ATTN_QK), cos, sin).astype(BF16)
    v_ref[...] = proj(2 * ATTN_QK, ATTN_V).astype(BF16)
    u0 = 2 * ATTN_QK + ATTN_V
    a = proj(u0, CONV_CH)
    g = proj(u0 + CONV_CH, CONV_CH)
    z_ref[...] = a * jax.nn.sigmoid(g)
    g0 = u0 + 2 * CONV_CH
    step = 512
    for c in range(0, 2 * D_MODEL, step):
        gate_ref[:, c:c + step] = jax.nn.sigmoid(proj(g0 + c, step)).astype(BF16)


def _in_proj(x, g, w_bf, cos_t, sin_t, seq, to_cast=(), tm=1024):
    t = x.shape[0]
    n_in = w_bf.shape[1]
    seq_blocks = seq // tm
    row = lambda i: (i, 0)
    tab = lambda i: (i % seq_blocks, 0)
    cast_in_specs, cast_out_specs, cast_shapes = _cast_specs(to_cast, t // tm)
    outs = pl.pallas_call(
        functools.partial(_in_proj_kernel, n_cast=len(to_cast)),
        grid=(t // tm,),
        in_specs=[
            pl.BlockSpec((tm, D_MODEL), row),
            _resident((1, D_MODEL)),
            _resident((D_MODEL, n_in)),
            pl.BlockSpec((tm, LANES), tab),
            pl.BlockSpec((tm, LANES), tab),
            *cast_in_specs,
        ],
        out_specs=[
            pl.BlockSpec((tm, ATTN_QK), row),
            pl.BlockSpec((tm, ATTN_QK), row),
            pl.BlockSpec((tm, ATTN_V), row),
            pl.BlockSpec((tm, CONV_CH), row),
            pl.BlockSpec((tm, 2 * D_MODEL), row),
            *cast_out_specs,
        ],
        out_shape=[
            jax.ShapeDtypeStruct((t, ATTN_QK), BF16),
            jax.ShapeDtypeStruct((t, ATTN_QK), BF16),
            jax.ShapeDtypeStruct((t, ATTN_V), BF16),
            jax.ShapeDtypeStruct((t, CONV_CH), F32),
            jax.ShapeDtypeStruct((t, 2 * D_MODEL), BF16),
            *cast_shapes,
        ],
        compiler_params=_cparams(1),
        name="in_proj",
    )(x, g, w_bf, cos_t, sin_t, *[w for w, _ in to_cast])
    return outs[:5], outs[5:]


_ATTN_SUB = 256


_ATTN_HEADS = 2


def _attn_kernel(lam_ref, q_ref, k_ref, v_ref, g_ref, o_ref, v1_ref, *, out_scale):
    contract_last = (((1,), (1,)), ((), ()))
    for hh in range(_ATTN_HEADS):
        head = slice(hh * V_DIM, (hh + 1) * V_DIM)
        v1_ref[hh, :, 0:V_DIM] = v_ref[:, head]
        v1_ref[hh, :, V_DIM:2 * V_DIM] = jnp.ones((v_ref.shape[0], V_DIM), BF16)
        k = k_ref[:, head]
        v1 = v1_ref[hh]

        def softmax_pv(qc):
            s = lax.dot_general(qc, k, contract_last, preferred_element_type=F32)
            p = jnp.exp(s - jnp.max(s, axis=-1, keepdims=True)).astype(BF16)
            ov = _dot(p, v1)
            return ov[:, 0:V_DIM] / ov[:, V_DIM:V_DIM + 1]

        for r0 in range(0, q_ref.shape[0], _ATTN_SUB):
            q = q_ref[r0:r0 + _ATTN_SUB, head]
            lane = lax.broadcasted_iota(jnp.int32, q.shape, 1)
            zero = jnp.zeros_like(q)
            o0 = softmax_pv(jnp.where(lane < HEAD_DIM, q, zero))
            o1 = softmax_pv(jnp.where(lane >= HEAD_DIM, q, zero))
            o = o0 - lam_ref[0] * o1
            o_ref[r0:r0 + _ATTN_SUB, head] = (_rms(o, g_ref[...]) * out_scale).astype(BF16)


def _attention(q, k, v, lam_full, g_subln, batch, seq, out_scale):
    t = q.shape[0]
    width = _ATTN_HEADS * V_DIM
    block = pl.BlockSpec((seq, width), lambda b, h, lam: (b, h))
    grid_spec = pltpu.PrefetchScalarGridSpec(
        num_scalar_prefetch=1,
        grid=(batch, N_HEADS // _ATTN_HEADS),
        in_specs=[block, block, block, pl.BlockSpec((1, V_DIM), lambda b, h, lam: (0, 0))],
        out_specs=block,
        scratch_shapes=[pltpu.VMEM((_ATTN_HEADS, seq, 2 * V_DIM), BF16)],
    )
    return pl.pallas_call(
        functools.partial(_attn_kernel, out_scale=out_scale),
        grid_spec=grid_spec,
        out_shape=jax.ShapeDtypeStruct((t, ATTN_V), BF16),
        compiler_params=_cparams(2),
        name="diff_attn",
    )(lam_full, q, k, v, g_subln)


_CONV_HALO = 16
_CONV_ROWS = 128


def _conv_block(zp_ref, w_ref, r0, c0):
    first = _CONV_HALO - CONV_PAD
    window = _CONV_ROWS + SUBLANES
    n_aligned = (CONV_K + first + SUBLANES - 1) // SUBLANES
    acc = None
    for r in reversed(range(SUBLANES)):
        y_r = None
        for a in range(n_aligned):
            tap = SUBLANES * a + r - first
            if 0 <= tap < CONV_K:
                lo = r0 + SUBLANES * a
                term = zp_ref[lo:lo + window, c0:c0 + LANES] * w_ref[tap:tap + 1, c0:c0 + LANES]
                y_r = term if y_r is None else y_r + term
        if acc is None:
            acc = y_r
        else:
            shifted = pltpu.roll(acc, window - 1, axis=0)
            acc = shifted if y_r is None else y_r + shifted
    return acc[0:_CONV_ROWS, :]


def _conv_tail(y, b_ref, lg_ref, lb_ref):
    y = y + b_ref[...]
    mu = jnp.mean(y, axis=-1, keepdims=True)
    yc = y - mu
    yn = yc * lax.rsqrt(jnp.mean(yc * yc, axis=-1, keepdims=True) + EPS) * lg_ref[...] + lb_ref[...]
    return (yn * jax.nn.sigmoid(yn)).astype(BF16)


_MIX_SUB = 512


def _mixed_residual(x_ref, o_ref, zprev_ref, z_ref, znext_ref, gate_ref, cw_ref, cb_ref, lg_ref, lb_ref,
                    wa_ref, wc_ref, wo_ref, out_ref, zp_ref, *, tiles_per_seq):
    tm = x_ref.shape[0]
    pos = pl.program_id(0) % tiles_per_seq
    zp_ref[0:_CONV_HALO, :] = jnp.where(pos > 0, zprev_ref[0], 0.0)
    zp_ref[_CONV_HALO:_CONV_HALO + tm, :] = z_ref[...]
    zp_ref[_CONV_HALO + tm:_CONV_HALO + tm + _CONV_HALO, :] = jnp.where(
        pos < tiles_per_seq - 1, znext_ref[0], 0.0)
    for r0 in range(0, tm, _MIX_SUB):
        rows = slice(r0, r0 + _MIX_SUB)
        conv = jnp.concatenate(
            [jnp.concatenate([_conv_block(zp_ref, cw_ref, r, c0) for c0 in range(0, CONV_CH, LANES)], axis=-1)
             for r in range(r0, r0 + _MIX_SUB, _CONV_ROWS)], axis=0)
        c = _conv_tail(conv, cb_ref, lg_ref, lb_ref)
        attn_d = _dot(o_ref[rows, :], wa_ref[...])
        conv_d = _dot(c, wc_ref[...])
        ga = gate_ref[rows, 0:D_MODEL].astype(F32)
        gc = gate_ref[rows, D_MODEL:2 * D_MODEL].astype(F32)
        mix = (ga * attn_d + gc * conv_d).astype(BF16)
        out_ref[rows, :] = x_ref[rows, :] + _dot(mix, wo_ref[...])


def _mix_kernel(*refs, tiles_per_seq):
    _mixed_residual(*refs, tiles_per_seq=tiles_per_seq)


def _mix_route_kernel(*refs, tiles_per_seq):
    mix_refs, (g_ref, wr_hi_ref, wr_lo_ref, br_ref) = refs[:13], refs[13:17]
    out_ref, meta_ref, plan_ref, cnt_ref, zp_ref, carry_ref = refs[17:]
    _mixed_residual(*mix_refs, out_ref, zp_ref, tiles_per_seq=tiles_per_seq)
    _route(_rms(out_ref[...], g_ref[...]), wr_hi_ref, wr_lo_ref, br_ref, meta_ref, plan_ref, cnt_ref,
           carry_ref)


def _mix(x, o, z, gates, conv_w, conv_b, ln_g, ln_b, wa, wc, wo, seq, router=None, tm=512):
    t = x.shape[0]
    row = lambda i: (i, 0)
    halo_per_tile = tm // _CONV_HALO
    n_halo = t // _CONV_HALO
    z_halo = z.reshape(n_halo, _CONV_HALO, CONV_CH)
    in_specs = [
        pl.BlockSpec((tm, D_MODEL), row),
        pl.BlockSpec((tm, ATTN_V), row),
        pl.BlockSpec((1, _CONV_HALO, CONV_CH), lambda i: (jnp.maximum(i * halo_per_tile - 1, 0), 0, 0)),
        pl.BlockSpec((tm, CONV_CH), row),
        pl.BlockSpec((1, _CONV_HALO, CONV_CH),
                     lambda i: (jnp.minimum((i + 1) * halo_per_tile, n_halo - 1), 0, 0)),
        pl.BlockSpec((tm, 2 * D_MODEL), row),
        _resident(conv_w.shape), _resident(conv_b.shape), _resident(ln_g.shape), _resident(ln_b.shape),
        _resident(wa.shape), _resident(wc.shape), _resident(wo.shape),
    ]
    args = (x, o, z_halo, z, z_halo, gates, conv_w, conv_b, ln_g, ln_b, wa, wc, wo)
    x_spec = pl.BlockSpec((tm, D_MODEL), row)
    x_shape = jax.ShapeDtypeStruct((t, D_MODEL), F32)
    zp_scratch = pltpu.VMEM((tm + 2 * _CONV_HALO, CONV_CH), F32)
    tiles_per_seq = seq // tm
    if router is None:
        return pl.pallas_call(
            functools.partial(_mix_kernel, tiles_per_seq=tiles_per_seq),
            grid=(t // tm,), in_specs=in_specs, out_specs=x_spec, out_shape=x_shape,
            scratch_shapes=[zp_scratch], compiler_params=_cparams(1), name="branch_mix",
        )(*args)
    return pl.pallas_call(
        functools.partial(_mix_route_kernel, tiles_per_seq=tiles_per_seq),
        grid=(t // tm,),
        in_specs=in_specs + [_resident(a.shape) for a in router],
        out_specs=[x_spec, pl.BlockSpec((tm, LANES), row), pl.BlockSpec((SUBLANES, tm), lambda i: (0, i)),
                   pl.BlockSpec((1, LANES), lambda i: (0, 0))],
        out_shape=[x_shape, jax.ShapeDtypeStruct((t, LANES), F32), jax.ShapeDtypeStruct((SUBLANES, t), F32),
                   jax.ShapeDtypeStruct((1, LANES), F32)],
        scratch_shapes=[zp_scratch, pltpu.VMEM((1, LANES), F32)],
        compiler_params=_cparams(1),
        name="branch_mix_route",
    )(*args, *router)


def _swiglu(h, wg_ref, wu_ref, wd_ref):
    g = _dot(h, wg_ref[...])
    u = _dot(h, wu_ref[...])
    a = (g * jax.nn.sigmoid(g) * u).astype(BF16)
    return _dot(a, wd_ref[...])


def _ffn_kernel(x_ref, g_ref, wg_ref, wu_ref, wd_ref, p_ref, gp_ref, wpg_ref, wpp_ref, gf_ref, *rest,
                n_cast, final_norm):
    cast_in, out_ref, cast_out = rest[:n_cast], rest[n_cast], rest[n_cast + 1:]
    x = x_ref[...]
    h = _rms(x, g_ref[...]).astype(BF16)
    x = x + _swiglu(h, wg_ref, wu_ref, wd_ref)
    _ple_tail(x, p_ref, gp_ref, wpg_ref, wpp_ref, gf_ref, out_ref, final_norm)
    _cast_slabs(cast_in, cast_out)


def _ffn_ple(x, g, w_gu, w_down, p, g_ple, w_gate, w_proj, g_final, final_norm, to_cast=(), tm=256):
    t = x.shape[0]
    d_ff = w_down.shape[0]
    steps = t // tm
    row = lambda i: (i, 0)
    cast_in_specs, cast_out_specs, cast_shapes = _cast_specs(to_cast, steps)
    outs = pl.pallas_call(
        functools.partial(_ffn_kernel, n_cast=len(to_cast), final_norm=final_norm),
        grid=(steps,),
        in_specs=[
            pl.BlockSpec((tm, D_MODEL), row),
            _resident((1, D_MODEL)),
            pl.BlockSpec((D_MODEL, d_ff), lambda i: (0, 0), pipeline_mode=pl.Buffered(1)),
            pl.BlockSpec((D_MODEL, d_ff), lambda i: (0, 1), pipeline_mode=pl.Buffered(1)),
            _resident((d_ff, D_MODEL)),
            pl.BlockSpec((None, tm, p[0].shape[2]), lambda i, layer=p[1]: (layer, i, 0)),
            _resident((1, D_MODEL)),
            _resident(w_gate.shape),
            _resident(w_proj.shape),
            _resident((1, D_MODEL)),
            *cast_in_specs,
        ],
        out_specs=[pl.BlockSpec((tm, D_MODEL), row), *cast_out_specs],
        out_shape=[jax.ShapeDtypeStruct((t, D_MODEL), F32), *cast_shapes],
        compiler_params=_cparams(1),
        name="dense_ffn_ple",
    )(x, g, w_gu, w_gu, w_down, p[0], g_ple, w_gate, w_proj, g_final, *[w for w, _ in to_cast])
    return outs[0], outs[1:]


TM_EXPERT = 256
META_E1, META_E2, META_R1, META_R2, META_W1, META_W2 = range(6)


def _row_copy(src_ref, src_row, dst_ref, dst_row, sem):
    return pltpu.make_async_copy(src_ref.at[pl.ds(src_row, 1)], dst_ref.at[pl.ds(dst_row, 1)], sem)


def _route(hf, wr_hi_ref, wr_lo_ref, br_ref, meta_ref, plan_ref, cnt_ref, carry_ref):
    @pl.when(pl.program_id(0) == 0)
    def _():
        carry_ref[...] = jnp.zeros_like(carry_ref)

    h_hi = hf.astype(BF16)
    h_lo = (hf - h_hi.astype(F32)).astype(BF16)
    logits = (_dot(h_hi, wr_hi_ref[...]) + _dot(h_lo, wr_hi_ref[...])
              + _dot(h_hi, wr_lo_ref[...])) + br_ref[...]
    tm = logits.shape[0]
    lane = lax.broadcasted_iota(jnp.int32, logits.shape, 1)
    neg = jnp.full_like(logits, -jnp.inf)
    logits = jnp.where(lane < N_EXPERTS, logits, neg)
    m1 = jnp.max(logits, axis=-1, keepdims=True)
    i1 = jnp.min(jnp.where(logits == m1, lane, LANES), axis=-1, keepdims=True)
    rest = jnp.where(lane == i1, neg, logits)
    m2 = jnp.max(rest, axis=-1, keepdims=True)
    i2 = jnp.min(jnp.where(rest == m2, lane, LANES), axis=-1, keepdims=True)
    e2 = jnp.exp(m2 - m1)
    denom = 1.0 + e2
    pick1 = lane == i1
    pick2 = lane == i2
    onehot = jnp.where(pick1 | pick2, 1.0, 0.0)
    r = lax.broadcasted_iota(jnp.int32, (tm, tm), 0)
    c = lax.broadcasted_iota(jnp.int32, (tm, tm), 1)
    strict_lower = jnp.where(r > c, 1.0, 0.0).astype(BF16)
    before = _dot(strict_lower, onehot.astype(BF16)) + carry_ref[...]
    rank1 = jnp.sum(jnp.where(pick1, before, 0.0), axis=-1, keepdims=True)
    rank2 = jnp.sum(jnp.where(pick2, before, 0.0), axis=-1, keepdims=True)
    record = jnp.zeros_like(logits)
    for slot, val in ((META_E1, i1.astype(F32)), (META_E2, i2.astype(F32)),
                      (META_R1, rank1), (META_R2, rank2),
                      (META_W1, 1.0 / denom), (META_W2, e2 / denom)):
        record = jnp.where(lane == slot, val, record)
    meta_ref[...] = record
    plan_ref[...] = record.T[0:SUBLANES, :]
    carry_ref[...] += jnp.sum(onehot, axis=0, keepdims=True)
    cnt_ref[...] = carry_ref[...]


def _dispatch_kernel(ends_ref, sizes_ref, h_ref, p1_ref, p2_ref, xs_ref, zero_ref, row_sem, zero_sem):
    tm = h_ref.shape[0]

    @pl.when(pl.program_id(0) == 0)
    def _():
        zero_ref[...] = jnp.zeros_like(zero_ref)
        n_tiles = xs_ref.shape[0] // TM_EXPERT
        total = ends_ref[N_EXPERTS - 1]

        def zero_tile(row):
            start = pl.multiple_of(row, TM_EXPERT)
            return pltpu.make_async_copy(zero_ref, xs_ref.at[pl.ds(start, TM_EXPERT)], zero_sem)

        fills = [(sizes_ref[e] > 0, ends_ref[e] - TM_EXPERT) for e in range(N_EXPERTS)]
        fills += [(j * TM_EXPERT >= total, j * TM_EXPERT) for j in range(n_tiles - N_EXPERTS, n_tiles)]
        for needed, row in fills:
            @pl.when(needed)
            def _():
                zero_tile(row).start()
        for needed, row in fills:
            @pl.when(needed)
            def _():
                zero_tile(row).wait()

    def scatter_row(r, carry):
        _row_copy(h_ref, r, xs_ref, p1_ref[0, 0, r], row_sem).start(priority=0)
        _row_copy(h_ref, r, xs_ref, p2_ref[0, 0, r], row_sem).start(priority=1)
        return carry

    lax.fori_loop(0, tm, scatter_row, 0, unroll=True)
    for _ in range(2):
        pltpu.make_async_copy(h_ref, xs_ref.at[pl.ds(0, tm)], row_sem).wait()


def _dispatch(h, pos1, pos2, ends, sizes, n_rows, tm=512):
    t = h.shape[0]
    idx = lambda a: a.reshape(t // tm, 1, tm)
    smem_block = pl.BlockSpec((1, 1, tm), lambda i, *_: (i, 0, 0), memory_space=pltpu.SMEM)
    grid_spec = pltpu.PrefetchScalarGridSpec(
        num_scalar_prefetch=2,
        grid=(t // tm,),
        in_specs=[pl.BlockSpec((tm, D_MODEL), lambda i, *_: (i, 0)), smem_block, smem_block],
        out_specs=pl.BlockSpec(memory_space=pl.ANY),
        scratch_shapes=[pltpu.VMEM((TM_EXPERT, D_MODEL), F32),
                        pltpu.SemaphoreType.DMA(()), pltpu.SemaphoreType.DMA(())],
    )
    return pl.pallas_call(
        _dispatch_kernel,
        grid_spec=grid_spec,
        out_shape=jax.ShapeDtypeStruct((n_rows, D_MODEL), F32),
        compiler_params=_cparams(1),
        name="moe_dispatch",
    )(ends, sizes, h, idx(pos1), idx(pos2))


def _expert_kernel(tile_expert_ref, tile_src_ref, x_ref, g_ref, wg_ref, wu_ref, wd_ref, y_ref):
    j = pl.program_id(0)

    @pl.when(tile_src_ref[j] != j)
    def _():
        y_ref[...] = jnp.zeros_like(y_ref)

    @pl.when(tile_src_ref[j] == j)
    def _():
        xb = _rms(x_ref[...], g_ref[...]).astype(BF16)
        half = wd_ref.shape[0] // 2
        y = None
        for c0 in (0, half):
            g = _dot(xb, wg_ref[:, c0:c0 + half])
            u = _dot(xb, wu_ref[:, c0:c0 + half])
            a = (g * jax.nn.sigmoid(g) * u).astype(BF16)
            part = _dot(a, wd_ref[c0:c0 + half, :])
            y = part if y is None else y + part
        y_ref[...] = y


def _experts(xs, g, tile_expert, tile_src, we_gu, we_down):
    n_rows = xs.shape[0]
    d_e = we_down.shape[1]
    tile = lambda j, te, ts: (ts[j], 0)
    grid_spec = pltpu.PrefetchScalarGridSpec(
        num_scalar_prefetch=2,
        grid=(n_rows // TM_EXPERT,),
        in_specs=[
            pl.BlockSpec((TM_EXPERT, D_MODEL), tile),
            pl.BlockSpec((1, D_MODEL), lambda j, te, ts: (0, 0)),
            pl.BlockSpec((None, D_MODEL, d_e), lambda j, te, ts: (te[j], 0, 0)),
            pl.BlockSpec((None, D_MODEL, d_e), lambda j, te, ts: (te[j], 0, 1)),
            pl.BlockSpec((None, d_e, D_MODEL), lambda j, te, ts: (te[j], 0, 0)),
        ],
        out_specs=pl.BlockSpec((TM_EXPERT, D_MODEL), lambda j, te, ts: (j, 0)),
    )
    return pl.pallas_call(
        _expert_kernel,
        grid_spec=grid_spec,
        out_shape=jax.ShapeDtypeStruct((n_rows, D_MODEL), F32),
        compiler_params=pltpu.CompilerParams(
            dimension_semantics=("arbitrary",), vmem_limit_bytes=VMEM_LIMIT_EXPERT),
        name="moe_experts",
    )(tile_expert, tile_src, xs, g, we_gu, we_gu, we_down)


def _ple_tail(x, p_ref, g_ref, wg_ref, wp_ref, gf_ref, out_ref, final_norm):
    h = _rms(x, g_ref[...]).astype(BF16)
    gate = jax.nn.sigmoid(_dot(h, wg_ref[...]))
    y = x + gate * _dot(p_ref[...].astype(BF16), wp_ref[...])
    if final_norm:
        y = _rms(y, gf_ref[...])
    out_ref[...] = y


def _combine_ple_kernel(x_ref, meta_ref, p1_ref, p2_ref, p1_next_ref, p2_next_ref, y_ref,
                        p_ref, g_ref, wg_ref, wp_ref, gf_ref, out_ref, rows_ref, sems, *, final_norm):
    i = pl.program_id(0)
    n = pl.num_programs(0)
    tm = x_ref.shape[0]
    slot = i % 2

    def gather(p1, p2, s):
        def gather_row(r, carry):
            _row_copy(y_ref, p1[0, 0, r], rows_ref.at[s, 0], r, sems.at[s]).start(priority=0)
            _row_copy(y_ref, p2[0, 0, r], rows_ref.at[s, 1], r, sems.at[s]).start(priority=1)
            return carry
        lax.fori_loop(0, tm, gather_row, 0, unroll=True)

    @pl.when(i == 0)
    def _():
        gather(p1_ref, p2_ref, 0)

    @pl.when(i + 1 < n)
    def _():
        gather(p1_next_ref, p2_next_ref, 1 - slot)

    for k in range(2):
        pltpu.make_async_copy(y_ref.at[pl.ds(0, tm)], rows_ref.at[slot, k], sems.at[slot]).wait()
    meta = meta_ref[...]
    x = (x_ref[...] + meta[:, META_W1:META_W1 + 1] * rows_ref[slot, 0]
         + meta[:, META_W2:META_W2 + 1] * rows_ref[slot, 1])
    _ple_tail(x, p_ref, g_ref, wg_ref, wp_ref, gf_ref, out_ref, final_norm)


def _combine_ple(x, meta, pos1, pos2, y, p, g, w_gate, w_proj, g_final, final_norm, tm=512):
    t = x.shape[0]
    nb = t // tm
    p_all, layer = p
    row = lambda i: (i, 0)
    idx = lambda a: a.reshape(nb, 1, tm)
    cur = pl.BlockSpec((1, 1, tm), lambda i: (i, 0, 0), memory_space=pltpu.SMEM)
    nxt = pl.BlockSpec((1, 1, tm), lambda i: (jnp.minimum(i + 1, nb - 1), 0, 0),
                       memory_space=pltpu.SMEM)
    return pl.pallas_call(
        functools.partial(_combine_ple_kernel, final_norm=final_norm),
        grid=(nb,),
        in_specs=[
            pl.BlockSpec((tm, D_MODEL), row),
            pl.BlockSpec((tm, LANES), row),
            cur, cur, nxt, nxt,
            pl.BlockSpec(memory_space=pl.ANY),
            pl.BlockSpec((None, tm, p_all.shape[2]), lambda i: (layer, i, 0)),
            _resident((1, D_MODEL)),
            _resident(w_gate.shape),
            _resident(w_proj.shape),
            _resident((1, D_MODEL)),
        ],
        out_specs=pl.BlockSpec((tm, D_MODEL), row),
        out_shape=jax.ShapeDtypeStruct((t, D_MODEL), F32),
        scratch_shapes=[pltpu.VMEM((2, 2, tm, D_MODEL), F32), pltpu.SemaphoreType.DMA((2,))],
        compiler_params=_cparams(1),
        name="moe_combine_ple",
    )(x, meta, idx(pos1), idx(pos2), idx(pos1), idx(pos2), y, p_all, g, w_gate, w_proj, g_final)


def _moe_plan(plan, counts):
    cnt = counts[0, :N_EXPERTS].astype(jnp.int32)
    sizes = (cnt + TM_EXPERT - 1) // TM_EXPERT * TM_EXPERT
    ends = jnp.cumsum(sizes)
    starts = ends - sizes

    def start_of(e):
        out = jnp.zeros_like(e)
        for j in range(N_EXPERTS):
            out = jnp.where(e == j, starts[j], out)
        return out

    pos1 = start_of(plan[META_E1].astype(jnp.int32)) + plan[META_R1].astype(jnp.int32)
    pos2 = start_of(plan[META_E2].astype(jnp.int32)) + plan[META_R2].astype(jnp.int32)
    n_rows = 2 * plan.shape[1] + N_EXPERTS * TM_EXPERT
    n_valid = ends[-1] // TM_EXPERT
    tile_src = jnp.minimum(jnp.arange(n_rows // TM_EXPERT, dtype=jnp.int32), n_valid - 1)
    tile_expert = jnp.sum((tile_src * TM_EXPERT)[:, None] >= ends[None, :], axis=1).astype(jnp.int32)
    return pos1, pos2, ends.astype(jnp.int32), sizes, tile_expert, tile_src, n_rows


def _rope_tables(seq):
    inv_freq = ROPE_THETA ** (-jnp.arange(0, HEAD_DIM, 2, dtype=F32) / HEAD_DIM)
    ang = jnp.arange(seq, dtype=F32)[:, None] * inv_freq[None, :]
    cos, sin = jnp.cos(ang), jnp.sin(ang)
    cos_t = jnp.concatenate([cos] * (LANES // (HEAD_DIM // 2)), axis=-1)
    sin_t = jnp.concatenate([-sin, sin] * (LANES // HEAD_DIM), axis=-1)
    return cos_t, sin_t


def kernel(x, p, g_mix, w_in, lam, g_subln, w_attn_out, conv_w, conv_b, conv_ln_g, conv_ln_b, w_conv_out, w_o, g_ffn, w_ff_gu, w_ff_down, w_router, b_router, we_gu, we_down, g_ple, w_ple_gate, w_ple_proj, g_final):
    batch, seq, _ = x.shape
    depth = w_in.shape[0]
    t = batch * seq
    cos_t, sin_t = _rope_tables(seq)
    xs = x.reshape(t, D_MODEL)
    row2 = lambda a: a.reshape(1, -1)
    expert_bf16 = {}
    w_in_bf16 = w_in[0].astype(BF16)
    for i in range(depth):
        later = {"attn_out": (w_attn_out, i), "conv_out": (w_conv_out, i), "w_o": (w_o, i),
                 "ple_gate": (w_ple_gate, i), "ple_proj": (w_ple_proj, i)}
        if i % 2 == 0:
            later.update(ff_gu=(w_ff_gu, i // 2), ff_down=(w_ff_down, i // 2))
        if i + 1 < depth:
            later["w_in_next"] = (w_in, i + 1)
        (q, k, v, z, gates), narrowed = _in_proj(xs, row2(g_mix[i]), w_in_bf16, cos_t, sin_t, seq,
                                                 tuple(later.values()))
        wb = dict(zip(later.keys(), narrowed))
        w_in_bf16 = wb.get("w_in_next")
        lam_init = 0.8 - 0.6 * math.exp(-0.3 * i)
        lf = lam[i].astype(F32)
        lam_full = (jnp.exp(jnp.sum(lf[0] * lf[1])) - jnp.exp(jnp.sum(lf[2] * lf[3]))
                    + lam_init).reshape(1)
        o = _attention(q, k, v, lam_full, row2(g_subln[i]), batch, seq, 1.0 - lam_init)
        mix_args = (xs, o, z, gates, conv_w[i], row2(conv_b[i]), row2(conv_ln_g[i]), row2(conv_ln_b[i]),
                    wb["attn_out"], wb["conv_out"], wb["w_o"], seq)
        ple_args = ((p.reshape(depth, t, -1), i), row2(g_ple[i]), wb["ple_gate"], wb["ple_proj"],
                    row2(g_final))
        final_norm = i == depth - 1
        if i % 2 == 0:
            xs = _mix(*mix_args)
            m_next = i // 2
            to_cast = ()
            if i + 1 < depth:
                to_cast = ((we_gu.reshape(we_gu.shape[0], -1, we_gu.shape[-1]), m_next),
                           (we_down.reshape(we_down.shape[0], -1, we_down.shape[-1]), m_next))
            xs, cast = _ffn_ple(xs, row2(g_ffn[i]), wb["ff_gu"], wb["ff_down"], *ple_args, final_norm, to_cast)
            if cast:
                expert_bf16[m_next] = (cast[0].reshape(we_gu.shape[1:]), cast[1].reshape(we_down.shape[1:]))
        else:
            m = i // 2
            wr = jnp.pad(w_router[m], ((0, 0), (0, LANES - N_EXPERTS)))
            wr_hi = wr.astype(BF16)
            wr_lo = (wr - wr_hi.astype(F32)).astype(BF16)
            br = jnp.pad(b_router[m], (0, LANES - N_EXPERTS)).reshape(1, LANES)
            xs, meta, plan, counts = _mix(*mix_args, router=(row2(g_ffn[i]), wr_hi, wr_lo, br))
            pos1, pos2, ends, sizes, tile_expert, tile_src, n_rows = _moe_plan(plan, counts)
            routed = _dispatch(xs, pos1, pos2, ends, sizes, n_rows)
            if m not in expert_bf16:
                expert_bf16[m] = (we_gu[m].astype(BF16), we_down[m].astype(BF16))
            y = _experts(routed, row2(g_ffn[i]), tile_expert, tile_src, *expert_bf16[m])
            xs = _combine_ple(xs, meta, pos1, pos2, y, *ple_args, final_norm=final_norm)
    return xs.reshape(batch, seq, D_MODEL)
```

```python
import functools
import math

import jax
import jax.numpy as jnp
from jax import lax
from jax.experimental import pallas as pl
from jax.experimental.pallas import tpu as pltpu

D_MODEL = 1024
N_HEADS = 4
HEAD_DIM = 64
V_DIM = 2 * HEAD_DIM
ATTN_QK = N_HEADS * 2 * HEAD_DIM
ATTN_V = N_HEADS * V_DIM
ROPE_THETA = 10000.0
CONV_CH = 512
CONV_K = 31
CONV_PAD = (CONV_K - 1) // 2
N_EXPERTS = 8
EPS = 1e-6

LANES = 128
SUBLANES = 8
VMEM_LIMIT = 56 * 1024 * 1024
VMEM_LIMIT_EXPERT = 60 * 1024 * 1024

F32 = jnp.float32
BF16 = jnp.bfloat16


def _cparams(n_axes):
    return pltpu.CompilerParams(
        dimension_semantics=("arbitrary",) * n_axes, vmem_limit_bytes=VMEM_LIMIT)


def _resident(shape):
    nd = len(shape)
    return pl.BlockSpec(shape, lambda *_: (0,) * nd, pipeline_mode=pl.Buffered(1))


def _rms(x, g):
    return x * lax.rsqrt(jnp.mean(x * x, axis=-1, keepdims=True) + EPS) * g


def _dot(a, b):
    return jnp.dot(a, b, preferred_element_type=F32)


BF16_ROW_TILE = 2 * SUBLANES


def _cast_specs(to_cast, steps):
    in_specs, out_specs, shapes = [], [], []
    for w, layer in to_cast:
        _, rows, cols = w.shape
        n_blocks = max(n for n in range(1, steps + 1)
                       if steps % n == 0 and rows % (n * BF16_ROW_TILE) == 0)
        every = steps // n_blocks
        in_specs.append(pl.BlockSpec((None, rows // n_blocks, cols),
                                     lambda i, every=every, layer=layer: (layer, i // every, 0)))
        out_specs.append(pl.BlockSpec((rows // n_blocks, cols), lambda i, every=every: (i // every, 0)))
        shapes.append(jax.ShapeDtypeStruct((rows, cols), BF16))
    return in_specs, out_specs, shapes


def _cast_slabs(src_refs, dst_refs):
    for src, dst in zip(src_refs, dst_refs):
        dst[...] = src[...].astype(BF16)


def _rope(t, cos, sin_signed):
    lane = lax.broadcasted_iota(jnp.int32, t.shape, 1)
    first_half = (lane & (HEAD_DIM // 2)) == 0
    width = t.shape[-1]
    partner = jnp.where(first_half,
                        pltpu.roll(t, width - HEAD_DIM // 2, axis=1),
                        pltpu.roll(t, HEAD_DIM // 2, axis=1))
    reps = width // LANES
    c = jnp.concatenate([cos] * reps, axis=-1)
    s = jnp.concatenate([sin_signed] * reps, axis=-1)
    return t * c + partner * s


def _in_proj_kernel(x_ref, g_ref, w_ref, cos_ref, sin_ref, *rest, n_cast):
    cast_in, (q_ref, k_ref, v_ref, z_ref, gate_ref), cast_out = (
        rest[:n_cast], rest[n_cast:n_cast + 5], rest[n_cast + 5:])
    _cast_slabs(cast_in, cast_out)
    h = _rms(x_ref[...], g_ref[...]).astype(BF16)
    cos = cos_ref[...]
    sin = sin_ref[...]

    def proj(c0, n):
        return _dot(h, w_ref[:, c0:c0 + n])

    q = _rope(proj(0, ATTN_QK), cos, sin) * (HEAD_DIM ** -0.5)
    q_ref[...] = q.astype(BF16)
    k_ref[...] = _rope(proj(ATTN_QK, ATTN_QK), cos, sin).T.astype(BF16)
    v_ref[...] = proj(2 * ATTN_QK, ATTN_V).astype(BF16)
    u0 = 2 * ATTN_QK + ATTN_V
    a = proj(u0, CONV_CH)
    g = proj(u0 + CONV_CH, CONV_CH)
    z_ref[...] = a * jax.nn.sigmoid(g)
    g0 = u0 + 2 * CONV_CH
    step = 512
    for c in range(0, 2 * D_MODEL, step):
        gate_ref[:, c:c + step] = jax.nn.sigmoid(proj(g0 + c, step)).astype(BF16)


def _in_proj(x, g, w_bf, cos_t, sin_t, seq, to_cast=(), tm=1024):
    t = x.shape[0]
    n_in = w_bf.shape[1]
    seq_blocks = seq // tm
    row = lambda i: (i, 0)
    tab = lambda i: (i % seq_blocks, 0)
    cast_in_specs, cast_out_specs, cast_shapes = _cast_specs(to_cast, t // tm)
    outs = pl.pallas_call(
        functools.partial(_in_proj_kernel, n_cast=len(to_cast)),
        grid=(t // tm,),
        in_specs=[
            pl.BlockSpec((tm, D_MODEL), row),
            _resident((1, D_MODEL)),
            _resident((D_MODEL, n_in)),
            pl.BlockSpec((tm, LANES), tab),
            pl.BlockSpec((tm, LANES), tab),
            *cast_in_specs,
        ],
        out_specs=[
            pl.BlockSpec((tm, ATTN_QK), row),
            pl.BlockSpec((ATTN_QK, tm), lambda i: (0, i)),
            pl.BlockSpec((tm, ATTN_V), row),
            pl.BlockSpec((tm, CONV_CH), row),
            pl.BlockSpec((tm, 2 * D_MODEL), row),
            *cast_out_specs,
        ],
        out_shape=[
            jax.ShapeDtypeStruct((t, ATTN_QK), BF16),
            jax.ShapeDtypeStruct((ATTN_QK, t), BF16),
            jax.ShapeDtypeStruct((t, ATTN_V), BF16),
            jax.ShapeDtypeStruct((t, CONV_CH), F32),
            jax.ShapeDtypeStruct((t, 2 * D_MODEL), BF16),
            *cast_shapes,
        ],
        compiler_params=_cparams(1),
        name="in_proj",
    )(x, g, w_bf, cos_t, sin_t, *[w for w, _ in to_cast])
    return outs[:5], outs[5:]


_ATTN_SUB = 256


_ATTN_HEADS = 2


def _attn_kernel(lam_ref, q_ref, kt_ref, v_ref, g_ref, o_ref, v1_ref, *, out_scale):
    for hh in range(_ATTN_HEADS):
        head = slice(hh * V_DIM, (hh + 1) * V_DIM)
        v1_ref[hh, :, 0:V_DIM] = v_ref[:, head]
        v1_ref[hh, :, V_DIM:2 * V_DIM] = jnp.ones((v_ref.shape[0], V_DIM), BF16)
        kt = kt_ref[head, :]
        v1 = v1_ref[hh]

        def softmax_pv(qc):
            s = _dot(qc, kt)
            p = jnp.exp(s - jnp.max(s, axis=-1, keepdims=True)).astype(BF16)
            ov = _dot(p, v1)
            return ov[:, 0:V_DIM] / ov[:, V_DIM:V_DIM + 1]

        for r0 in range(0, q_ref.shape[0], _ATTN_SUB):
            q = q_ref[r0:r0 + _ATTN_SUB, head]
            lane = lax.broadcasted_iota(jnp.int32, q.shape, 1)
            zero = jnp.zeros_like(q)
            o0 = softmax_pv(jnp.where(lane < HEAD_DIM, q, zero))
            o1 = softmax_pv(jnp.where(lane >= HEAD_DIM, q, zero))
            o = o0 - lam_ref[0] * o1
            o_ref[r0:r0 + _ATTN_SUB, head] = (_rms(o, g_ref[...]) * out_scale).astype(BF16)


def _attention(q, kt, v, lam_full, g_subln, batch, seq, out_scale):
    t = q.shape[0]
    width = _ATTN_HEADS * V_DIM
    block = pl.BlockSpec((seq, width), lambda b, h, lam: (b, h))
    grid_spec = pltpu.PrefetchScalarGridSpec(
        num_scalar_prefetch=1,
        grid=(batch, N_HEADS // _ATTN_HEADS),
        in_specs=[block, pl.BlockSpec((width, seq), lambda b, h, lam: (h, b)), block,
                  pl.BlockSpec((1, V_DIM), lambda b, h, lam: (0, 0))],
        out_specs=block,
        scratch_shapes=[pltpu.VMEM((_ATTN_HEADS, seq, 2 * V_DIM), BF16)],
    )
    return pl.pallas_call(
        functools.partial(_attn_kernel, out_scale=out_scale),
        grid_spec=grid_spec,
        out_shape=jax.ShapeDtypeStruct((t, ATTN_V), BF16),
        compiler_params=_cparams(2),
        name="diff_attn",
    )(lam_full, q, kt, v, g_subln)


_CONV_HALO = 16
_CONV_ROWS = 128


def _conv_block(zp_ref, w_ref, r0, c0):
    first = _CONV_HALO - CONV_PAD
    window = _CONV_ROWS + SUBLANES
    n_aligned = (CONV_K + first + SUBLANES - 1) // SUBLANES
    acc = None
    for r in reversed(range(SUBLANES)):
        y_r = None
        for a in range(n_aligned):
            tap = SUBLANES * a + r - first
            if 0 <= tap < CONV_K:
                lo = r0 + SUBLANES * a
                term = zp_ref[lo:lo + window, c0:c0 + LANES] * w_ref[tap:tap + 1, c0:c0 + LANES]
                y_r = term if y_r is None else y_r + term
        if acc is None:
            acc = y_r
        else:
            shifted = pltpu.roll(acc, window - 1, axis=0)
            acc = shifted if y_r is None else y_r + shifted
    return acc[0:_CONV_ROWS, :]


def _conv_tail(y, b_ref, lg_ref, lb_ref):
    y = y + b_ref[...]
    mu = jnp.mean(y, axis=-1, keepdims=True)
    yc = y - mu
    yn = yc * lax.rsqrt(jnp.mean(yc * yc, axis=-1, keepdims=True) + EPS) * lg_ref[...] + lb_ref[...]
    return (yn * jax.nn.sigmoid(yn)).astype(BF16)


_MIX_SUB = 512


def _mixed_residual(x_ref, o_ref, zprev_ref, z_ref, znext_ref, gate_ref, cw_ref, cb_ref, lg_ref, lb_ref,
                    wa_ref, wc_ref, wo_ref, out_ref, zp_ref, *, tiles_per_seq):
    tm = x_ref.shape[0]
    pos = pl.program_id(0) % tiles_per_seq
    zp_ref[0:_CONV_HALO, :] = jnp.where(pos > 0, zprev_ref[0], 0.0)
    zp_ref[_CONV_HALO:_CONV_HALO + tm, :] = z_ref[...]
    zp_ref[_CONV_HALO + tm:_CONV_HALO + tm + _CONV_HALO, :] = jnp.where(
        pos < tiles_per_seq - 1, znext_ref[0], 0.0)
    for r0 in range(0, tm, _MIX_SUB):
        rows = slice(r0, r0 + _MIX_SUB)
        conv = jnp.concatenate(
            [jnp.concatenate([_conv_block(zp_ref, cw_ref, r, c0) for c0 in range(0, CONV_CH, LANES)], axis=-1)
             for r in range(r0, r0 + _MIX_SUB, _CONV_ROWS)], axis=0)
        c = _conv_tail(conv, cb_ref, lg_ref, lb_ref)
        attn_d = _dot(o_ref[rows, :], wa_ref[...])
        conv_d = _dot(c, wc_ref[...])
        ga = gate_ref[rows, 0:D_MODEL].astype(F32)
        gc = gate_ref[rows, D_MODEL:2 * D_MODEL].astype(F32)
        mix = (ga * attn_d + gc * conv_d).astype(BF16)
        out_ref[rows, :] = x_ref[rows, :] + _dot(mix, wo_ref[...])


def _mix_kernel(*refs, tiles_per_seq):
    _mixed_residual(*refs, tiles_per_seq=tiles_per_seq)


def _mix_route_kernel(*refs, tiles_per_seq):
    mix_refs, (g_ref, wr_hi_ref, wr_lo_ref, br_ref) = refs[:13], refs[13:17]
    out_ref, meta_ref, plan_ref, cnt_ref, zp_ref, carry_ref = refs[17:]
    _mixed_residual(*mix_refs, out_ref, zp_ref, tiles_per_seq=tiles_per_seq)
    _route(_rms(out_ref[...], g_ref[...]), wr_hi_ref, wr_lo_ref, br_ref, meta_ref, plan_ref, cnt_ref,
           carry_ref)


def _mix(x, o, z, gates, conv_w, conv_b, ln_g, ln_b, wa, wc, wo, seq, router=None, tm=512):
    t = x.shape[0]
    row = lambda i: (i, 0)
    halo_per_tile = tm // _CONV_HALO
    n_halo = t // _CONV_HALO
    z_halo = z.reshape(n_halo, _CONV_HALO, CONV_CH)
    in_specs = [
        pl.BlockSpec((tm, D_MODEL), row),
        pl.BlockSpec((tm, ATTN_V), row),
        pl.BlockSpec((1, _CONV_HALO, CONV_CH), lambda i: (jnp.maximum(i * halo_per_tile - 1, 0), 0, 0)),
        pl.BlockSpec((tm, CONV_CH), row),
        pl.BlockSpec((1, _CONV_HALO, CONV_CH),
                     lambda i: (jnp.minimum((i + 1) * halo_per_tile, n_halo - 1), 0, 0)),
        pl.BlockSpec((tm, 2 * D_MODEL), row),
        _resident(conv_w.shape), _resident(conv_b.shape), _resident(ln_g.shape), _resident(ln_b.shape),
        _resident(wa.shape), _resident(wc.shape), _resident(wo.shape),
    ]
    args = (x, o, z_halo, z, z_halo, gates, conv_w, conv_b, ln_g, ln_b, wa, wc, wo)
    x_spec = pl.BlockSpec((tm, D_MODEL), row)
    x_shape = jax.ShapeDtypeStruct((t, D_MODEL), F32)
    zp_scratch = pltpu.VMEM((tm + 2 * _CONV_HALO, CONV_CH), F32)
    tiles_per_seq = seq // tm
    if router is None:
        return pl.pallas_call(
            functools.partial(_mix_kernel, tiles_per_seq=tiles_per_seq),
            grid=(t // tm,), in_specs=in_specs, out_specs=x_spec, out_shape=x_shape,
            scratch_shapes=[zp_scratch], compiler_params=_cparams(1), name="branch_mix",
        )(*args)
    return pl.pallas_call(
        functools.partial(_mix_route_kernel, tiles_per_seq=tiles_per_seq),
        grid=(t // tm,),
        in_specs=in_specs + [_resident(a.shape) for a in router],
        out_specs=[x_spec, pl.BlockSpec((tm, LANES), row), pl.BlockSpec((SUBLANES, tm), lambda i: (0, i)),
                   pl.BlockSpec((1, LANES), lambda i: (0, 0))],
        out_shape=[x_shape, jax.ShapeDtypeStruct((t, LANES), F32), jax.ShapeDtypeStruct((SUBLANES, t), F32),
                   jax.ShapeDtypeStruct((1, LANES), F32)],
        scratch_shapes=[zp_scratch, pltpu.VMEM((1, LANES), F32)],
        compiler_params=_cparams(1),
        name="branch_mix_route",
    )(*args, *router)


def _swiglu(h, wg_ref, wu_ref, wd_ref):
    g = _dot(h, wg_ref[...])
    u = _dot(h, wu_ref[...])
    a = (g * jax.nn.sigmoid(g) * u).astype(BF16)
    return _dot(a, wd_ref[...])


def _ffn_kernel(x_ref, g_ref, wg_ref, wu_ref, wd_ref, p_ref, gp_ref, wpg_ref, wpp_ref, gf_ref, *rest,
                n_cast, final_norm):
    cast_in, out_ref, cast_out = rest[:n_cast], rest[n_cast], rest[n_cast + 1:]
    x = x_ref[...]
    h = _rms(x, g_ref[...]).astype(BF16)
    x = x + _swiglu(h, wg_ref, wu_ref, wd_ref)
    _ple_tail(x, p_ref, gp_ref, wpg_ref, wpp_ref, gf_ref, out_ref, final_norm)
    _cast_slabs(cast_in, cast_out)


def _ffn_ple(x, g, w_gu, w_down, p, g_ple, w_gate, w_proj, g_final, final_norm, to_cast=(), tm=256):
    t = x.shape[0]
    d_ff = w_down.shape[0]
    steps = t // tm
    row = lambda i: (i, 0)
    cast_in_specs, cast_out_specs, cast_shapes = _cast_specs(to_cast, steps)
    outs = pl.pallas_call(
        functools.partial(_ffn_kernel, n_cast=len(to_cast), final_norm=final_norm),
        grid=(steps,),
        in_specs=[
            pl.BlockSpec((tm, D_MODEL), row),
            _resident((1, D_MODEL)),
            pl.BlockSpec((D_MODEL, d_ff), lambda i: (0, 0), pipeline_mode=pl.Buffered(1)),
            pl.BlockSpec((D_MODEL, d_ff), lambda i: (0, 1), pipeline_mode=pl.Buffered(1)),
            _resident((d_ff, D_MODEL)),
            pl.BlockSpec((None, tm, p[0].shape[2]), lambda i, layer=p[1]: (layer, i, 0)),
            _resident((1, D_MODEL)),
            _resident(w_gate.shape),
            _resident(w_proj.shape),
            _resident((1, D_MODEL)),
            *cast_in_specs,
        ],
        out_specs=[pl.BlockSpec((tm, D_MODEL), row), *cast_out_specs],
        out_shape=[jax.ShapeDtypeStruct((t, D_MODEL), F32), *cast_shapes],
        compiler_params=_cparams(1),
        name="dense_ffn_ple",
    )(x, g, w_gu, w_gu, w_down, p[0], g_ple, w_gate, w_proj, g_final, *[w for w, _ in to_cast])
    return outs[0], outs[1:]


TM_EXPERT = 256
META_E1, META_E2, META_R1, META_R2, META_W1, META_W2 = range(6)


def _row_copy(src_ref, src_row, dst_ref, dst_row, sem):
    return pltpu.make_async_copy(src_ref.at[pl.ds(src_row, 1)], dst_ref.at[pl.ds(dst_row, 1)], sem)


def _route(hf, wr_hi_ref, wr_lo_ref, br_ref, meta_ref, plan_ref, cnt_ref, carry_ref):
    @pl.when(pl.program_id(0) == 0)
    def _():
        carry_ref[...] = jnp.zeros_like(carry_ref)

    h_hi = hf.astype(BF16)
    h_lo = (hf - h_hi.astype(F32)).astype(BF16)
    logits = (_dot(h_hi, wr_hi_ref[...]) + _dot(h_lo, wr_hi_ref[...])
              + _dot(h_hi, wr_lo_ref[...])) + br_ref[...]
    tm = logits.shape[0]
    lane = lax.broadcasted_iota(jnp.int32, logits.shape, 1)
    neg = jnp.full_like(logits, -jnp.inf)
    logits = jnp.where(lane < N_EXPERTS, logits, neg)
    m1 = jnp.max(logits, axis=-1, keepdims=True)
    i1 = jnp.min(jnp.where(logits == m1, lane, LANES), axis=-1, keepdims=True)
    rest = jnp.where(lane == i1, neg, logits)
    m2 = jnp.max(rest, axis=-1, keepdims=True)
    i2 = jnp.min(jnp.where(rest == m2, lane, LANES), axis=-1, keepdims=True)
    e2 = jnp.exp(m2 - m1)
    denom = 1.0 + e2
    pick1 = lane == i1
    pick2 = lane == i2
    onehot = jnp.where(pick1 | pick2, 1.0, 0.0)
    r = lax.broadcasted_iota(jnp.int32, (tm, tm), 0)
    c = lax.broadcasted_iota(jnp.int32, (tm, tm), 1)
    strict_lower = jnp.where(r > c, 1.0, 0.0).astype(BF16)
    before = _dot(strict_lower, onehot.astype(BF16)) + carry_ref[...]
    rank1 = jnp.sum(jnp.where(pick1, before, 0.0), axis=-1, keepdims=True)
    rank2 = jnp.sum(jnp.where(pick2, before, 0.0), axis=-1, keepdims=True)
    record = jnp.zeros_like(logits)
    for slot, val in ((META_E1, i1.astype(F32)), (META_E2, i2.astype(F32)),
                      (META_R1, rank1), (META_R2, rank2),
                      (META_W1, 1.0 / denom), (META_W2, e2 / denom)):
        record = jnp.where(lane == slot, val, record)
    meta_ref[...] = record
    plan_ref[...] = record.T[0:SUBLANES, :]
    carry_ref[...] += jnp.sum(onehot, axis=0, keepdims=True)
    cnt_ref[...] = carry_ref[...]


def _dispatch_kernel(ends_ref, sizes_ref, h_ref, p1_ref, p2_ref, xs_ref, zero_ref, row_sem, zero_sem):
    tm = h_ref.shape[0]

    @pl.when(pl.program_id(0) == 0)
    def _():
        zero_ref[...] = jnp.zeros_like(zero_ref)
        n_tiles = xs_ref.shape[0] // TM_EXPERT
        total = ends_ref[N_EXPERTS - 1]

        def zero_tile(row):
            start = pl.multiple_of(row, TM_EXPERT)
            return pltpu.make_async_copy(zero_ref, xs_ref.at[pl.ds(start, TM_EXPERT)], zero_sem)

        fills = [(sizes_ref[e] > 0, ends_ref[e] - TM_EXPERT) for e in range(N_EXPERTS)]
        fills += [(j * TM_EXPERT >= total, j * TM_EXPERT) for j in range(n_tiles - N_EXPERTS, n_tiles)]
        for needed, row in fills:
            @pl.when(needed)
            def _():
                zero_tile(row).start()
        for needed, row in fills:
            @pl.when(needed)
            def _():
                zero_tile(row).wait()

    def scatter_row(r, carry):
        _row_copy(h_ref, r, xs_ref, p1_ref[0, 0, r], row_sem).start(priority=0)
        _row_copy(h_ref, r, xs_ref, p2_ref[0, 0, r], row_sem).start(priority=1)
        return carry

    lax.fori_loop(0, tm, scatter_row, 0, unroll=True)
    for _ in range(2):
        pltpu.make_async_copy(h_ref, xs_ref.at[pl.ds(0, tm)], row_sem).wait()


def _dispatch(h, pos1, pos2, ends, sizes, n_rows, tm=512):
    t = h.shape[0]
    idx = lambda a: a.reshape(t // tm, 1, tm)
    smem_block = pl.BlockSpec((1, 1, tm), lambda i, *_: (i, 0, 0), memory_space=pltpu.SMEM)
    grid_spec = pltpu.PrefetchScalarGridSpec(
        num_scalar_prefetch=2,
        grid=(t // tm,),
        in_specs=[pl.BlockSpec((tm, D_MODEL), lambda i, *_: (i, 0)), smem_block, smem_block],
        out_specs=pl.BlockSpec(memory_space=pl.ANY),
        scratch_shapes=[pltpu.VMEM((TM_EXPERT, D_MODEL), F32),
                        pltpu.SemaphoreType.DMA(()), pltpu.SemaphoreType.DMA(())],
    )
    return pl.pallas_call(
        _dispatch_kernel,
        grid_spec=grid_spec,
        out_shape=jax.ShapeDtypeStruct((n_rows, D_MODEL), F32),
        compiler_params=_cparams(1),
        name="moe_dispatch",
    )(ends, sizes, h, idx(pos1), idx(pos2))


def _expert_kernel(tile_expert_ref, tile_src_ref, x_ref, g_ref, wg_ref, wu_ref, wd_ref, y_ref):
    j = pl.program_id(0)

    @pl.when(tile_src_ref[j] != j)
    def _():
        y_ref[...] = jnp.zeros_like(y_ref)

    @pl.when(tile_src_ref[j] == j)
    def _():
        xb = _rms(x_ref[...], g_ref[...]).astype(BF16)
        half = wd_ref.shape[0] // 2
        y = None
        for c0 in (0, half):
            g = _dot(xb, wg_ref[:, c0:c0 + half])
            u = _dot(xb, wu_ref[:, c0:c0 + half])
            a = (g * jax.nn.sigmoid(g) * u).astype(BF16)
            part = _dot(a, wd_ref[c0:c0 + half, :])
            y = part if y is None else y + part
        y_ref[...] = y


def _experts(xs, g, tile_expert, tile_src, we_gu, we_down):
    n_rows = xs.shape[0]
    d_e = we_down.shape[1]
    tile = lambda j, te, ts: (ts[j], 0)
    grid_spec = pltpu.PrefetchScalarGridSpec(
        num_scalar_prefetch=2,
        grid=(n_rows // TM_EXPERT,),
        in_specs=[
            pl.BlockSpec((TM_EXPERT, D_MODEL), tile),
            pl.BlockSpec((1, D_MODEL), lambda j, te, ts: (0, 0)),
            pl.BlockSpec((None, D_MODEL, d_e), lambda j, te, ts: (te[j], 0, 0)),
            pl.BlockSpec((None, D_MODEL, d_e), lambda j, te, ts: (te[j], 0, 1)),
            pl.BlockSpec((None, d_e, D_MODEL), lambda j, te, ts: (te[j], 0, 0)),
        ],
        out_specs=pl.BlockSpec((TM_EXPERT, D_MODEL), lambda j, te, ts: (j, 0)),
    )
    return pl.pallas_call(
        _expert_kernel,
        grid_spec=grid_spec,
        out_shape=jax.ShapeDtypeStruct((n_rows, D_MODEL), F32),
        compiler_params=pltpu.CompilerParams(
            dimension_semantics=("arbitrary",), vmem_limit_bytes=VMEM_LIMIT_EXPERT),
        name="moe_experts",
    )(tile_expert, tile_src, xs, g, we_gu, we_gu, we_down)


def _ple_tail(x, p_ref, g_ref, wg_ref, wp_ref, gf_ref, out_ref, final_norm):
    h = _rms(x, g_ref[...]).astype(BF16)
    gate = jax.nn.sigmoid(_dot(h, wg_ref[...]))
    y = x + gate * _dot(p_ref[...].astype(BF16), wp_ref[...])
    if final_norm:
        y = _rms(y, gf_ref[...])
    out_ref[...] = y


def _combine_ple_kernel(x_ref, meta_ref, p1_ref, p2_ref, p1_next_ref, p2_next_ref, y_ref,
                        p_ref, g_ref, wg_ref, wp_ref, gf_ref, out_ref, rows_ref, sems, *, final_norm):
    i = pl.program_id(0)
    n = pl.num_programs(0)
    tm = x_ref.shape[0]
    slot = i % 2

    def gather(p1, p2, s):
        def gather_row(r, carry):
            _row_copy(y_ref, p1[0, 0, r], rows_ref.at[s, 0], r, sems.at[s]).start(priority=0)
            _row_copy(y_ref, p2[0, 0, r], rows_ref.at[s, 1], r, sems.at[s]).start(priority=1)
            return carry
        lax.fori_loop(0, tm, gather_row, 0, unroll=True)

    @pl.when(i == 0)
    def _():
        gather(p1_ref, p2_ref, 0)

    @pl.when(i + 1 < n)
    def _():
        gather(p1_next_ref, p2_next_ref, 1 - slot)

    for k in range(2):
        pltpu.make_async_copy(y_ref.at[pl.ds(0, tm)], rows_ref.at[slot, k], sems.at[slot]).wait()
    meta = meta_ref[...]
    x = (x_ref[...] + meta[:, META_W1:META_W1 + 1] * rows_ref[slot, 0]
         + meta[:, META_W2:META_W2 + 1] * rows_ref[slot, 1])
    _ple_tail(x, p_ref, g_ref, wg_ref, wp_ref, gf_ref, out_ref, final_norm)


def _combine_ple(x, meta, pos1, pos2, y, p, g, w_gate, w_proj, g_final, final_norm, tm=512):
    t = x.shape[0]
    nb = t // tm
    p_all, layer = p
    row = lambda i: (i, 0)
    idx = lambda a: a.reshape(nb, 1, tm)
    cur = pl.BlockSpec((1, 1, tm), lambda i: (i, 0, 0), memory_space=pltpu.SMEM)
    nxt = pl.BlockSpec((1, 1, tm), lambda i: (jnp.minimum(i + 1, nb - 1), 0, 0),
                       memory_space=pltpu.SMEM)
    return pl.pallas_call(
        functools.partial(_combine_ple_kernel, final_norm=final_norm),
        grid=(nb,),
        in_specs=[
            pl.BlockSpec((tm, D_MODEL), row),
            pl.BlockSpec((tm, LANES), row),
            cur, cur, nxt, nxt,
            pl.BlockSpec(memory_space=pl.ANY),
            pl.BlockSpec((None, tm, p_all.shape[2]), lambda i: (layer, i, 0)),
            _resident((1, D_MODEL)),
            _resident(w_gate.shape),
            _resident(w_proj.shape),
            _resident((1, D_MODEL)),
        ],
        out_specs=pl.BlockSpec((tm, D_MODEL), row),
        out_shape=jax.ShapeDtypeStruct((t, D_MODEL), F32),
        scratch_shapes=[pltpu.VMEM((2, 2, tm, D_MODEL), F32), pltpu.SemaphoreType.DMA((2,))],
        compiler_params=_cparams(1),
        name="moe_combine_ple",
    )(x, meta, idx(pos1), idx(pos2), idx(pos1), idx(pos2), y, p_all, g, w_gate, w_proj, g_final)


def _moe_plan(plan, counts):
    cnt = counts[0, :N_EXPERTS].astype(jnp.int32)
    sizes = (cnt + TM_EXPERT - 1) // TM_EXPERT * TM_EXPERT
    ends = jnp.cumsum(sizes)
    starts = ends - sizes

    def start_of(e):
        out = jnp.zeros_like(e)
        for j in range(N_EXPERTS):
            out = jnp.where(e == j, starts[j], out)
        return out

    pos1 = start_of(plan[META_E1].astype(jnp.int32)) + plan[META_R1].astype(jnp.int32)
    pos2 = start_of(plan[META_E2].astype(jnp.int32)) + plan[META_R2].astype(jnp.int32)
    n_rows = 2 * plan.shape[1] + N_EXPERTS * TM_EXPERT
    n_valid = ends[-1] // TM_EXPERT
    tile_src = jnp.minimum(jnp.arange(n_rows // TM_EXPERT, dtype=jnp.int32), n_valid - 1)
    tile_expert = jnp.sum((tile_src * TM_EXPERT)[:, None] >= ends[None, :], axis=1).astype(jnp.int32)
    return pos1, pos2, ends.astype(jnp.int32), sizes, tile_expert, tile_src, n_rows


def _rope_tables(seq):
    inv_freq = ROPE_THETA ** (-jnp.arange(0, HEAD_DIM, 2, dtype=F32) / HEAD_DIM)
    ang = jnp.arange(seq, dtype=F32)[:, None] * inv_freq[None, :]
    cos, sin = jnp.cos(ang), jnp.sin(ang)
    cos_t = jnp.concatenate([cos] * (LANES // (HEAD_DIM // 2)), axis=-1)
    sin_t = jnp.concatenate([-sin, sin] * (LANES // HEAD_DIM), axis=-1)
    return cos_t, sin_t


def kernel(x, p, g_mix, w_in, lam, g_subln, w_attn_out, conv_w, conv_b, conv_ln_g, conv_ln_b, w_conv_out, w_o, g_ffn, w_ff_gu, w_ff_down, w_router, b_router, we_gu, we_down, g_ple, w_ple_gate, w_ple_proj, g_final):
    batch, seq, _ = x.shape
    depth = w_in.shape[0]
    t = batch * seq
    cos_t, sin_t = _rope_tables(seq)
    xs = x.reshape(t, D_MODEL)
    row2 = lambda a: a.reshape(1, -1)
    expert_bf16 = {}
    w_in_bf16 = w_in[0].astype(BF16)
    for i in range(depth):
        later = {"attn_out": (w_attn_out, i), "conv_out": (w_conv_out, i), "w_o": (w_o, i),
                 "ple_gate": (w_ple_gate, i), "ple_proj": (w_ple_proj, i)}
        if i % 2 == 0:
            later.update(ff_gu=(w_ff_gu, i // 2), ff_down=(w_ff_down, i // 2))
        if i + 1 < depth:
            later["w_in_next"] = (w_in, i + 1)
        (q, k, v, z, gates), narrowed = _in_proj(xs, row2(g_mix[i]), w_in_bf16, cos_t, sin_t, seq,
                                                 tuple(later.values()))
        wb = dict(zip(later.keys(), narrowed))
        w_in_bf16 = wb.get("w_in_next")
        lam_init = 0.8 - 0.6 * math.exp(-0.3 * i)
        lf = lam[i].astype(F32)
        lam_full = (jnp.exp(jnp.sum(lf[0] * lf[1])) - jnp.exp(jnp.sum(lf[2] * lf[3]))
                    + lam_init).reshape(1)
        o = _attention(q, k, v, lam_full, row2(g_subln[i]), batch, seq, 1.0 - lam_init)
        mix_args = (xs, o, z, gates, conv_w[i], row2(conv_b[i]), row2(conv_ln_g[i]), row2(conv_ln_b[i]),
                    wb["attn_out"], wb["conv_out"], wb["w_o"], seq)
        ple_args = ((p.reshape(depth, t, -1), i), row2(g_ple[i]), wb["ple_gate"], wb["ple_proj"],
                    row2(g_final))
        final_norm = i == depth - 1
        if i % 2 == 0:
            xs = _mix(*mix_args)
            m_next = i // 2
            to_cast = ()
            if i + 1 < depth:
                to_cast = ((we_gu.reshape(we_gu.shape[0], -1, we_gu.shape[-1]), m_next),
                           (we_down.reshape(we_down.shape[0], -1, we_down.shape[-1]), m_next))
            xs, cast = _ffn_ple(xs, row2(g_ffn[i]), wb["ff_gu"], wb["ff_down"], *ple_args, final_norm, to_cast)
            if cast:
                expert_bf16[m_next] = (cast[0].reshape(we_gu.shape[1:]), cast[1].reshape(we_down.shape[1:]))
        else:
            m = i // 2
            wr = jnp.pad(w_router[m], ((0, 0), (0, LANES - N_EXPERTS)))
            wr_hi = wr.astype(BF16)
            wr_lo = (wr - wr_hi.astype(F32)).astype(BF16)
            br = jnp.pad(b_router[m], (0, LANES - N_EXPERTS)).reshape(1, LANES)
            xs, meta, plan, counts = _mix(*mix_args, router=(row2(g_ffn[i]), wr_hi, wr_lo, br))
            pos1, pos2, ends, sizes, tile_expert, tile_src, n_rows = _moe_plan(plan, counts)
            routed = _dispatch(xs, pos1, pos2, ends, sizes, n_rows)
            if m not in expert_bf16:
                expert_bf16[m] = (we_gu[m].astype(BF16), we_down[m].astype(BF16))
            y = _experts(routed, row2(g_ffn[i]), tile_expert, tile_src, *expert_bf16[m])
            xs = _combine_ple(xs, meta, pos1, pos2, y, *ple_args, final_norm=final_norm)
    return xs.reshape(batch, seq, D_MODEL)
```

```python
import functools
import math

import jax
import jax.numpy as jnp
from jax import lax
from jax.experimental import pallas as pl
from jax.experimental.pallas import tpu as pltpu

D_MODEL = 1024
N_HEADS = 4
HEAD_DIM = 64
V_DIM = 2 * HEAD_DIM
ATTN_QK = N_HEADS * 2 * HEAD_DIM
ATTN_V = N_HEADS * V_DIM
ROPE_THETA = 10000.0
CONV_CH = 512
CONV_K = 31
CONV_PAD = (CONV_K - 1) // 2
N_EXPERTS = 8
EPS = 1e-6

LANES = 128
SUBLANES = 8
VMEM_LIMIT = 56 * 1024 * 1024
VMEM_LIMIT_EXPERT = 60 * 1024 * 1024

F32 = jnp.float32
BF16 = jnp.bfloat16


def _cparams(n_axes):
    return pltpu.CompilerParams(
        dimension_semantics=("arbitrary",) * n_axes, vmem_limit_bytes=VMEM_LIMIT)


def _resident(shape):
    nd = len(shape)
    return pl.BlockSpec(shape, lambda *_: (0,) * nd, pipeline_mode=pl.Buffered(1))


def _rms(x, g):
    return x * lax.rsqrt(jnp.mean(x * x, axis=-1, keepdims=True) + EPS) * g


def _dot(a, b):
    return jnp.dot(a, b, preferred_element_type=F32)


BF16_ROW_TILE = 2 * SUBLANES


def _cast_specs(to_cast, steps):
    in_specs, out_specs, shapes = [], [], []
    for w, layer in to_cast:
        _, rows, cols = w.shape
        n_blocks = max(n for n in range(1, steps + 1)
                       if steps % n == 0 and rows % (n * BF16_ROW_TILE) == 0)
        every = steps // n_blocks
        in_specs.append(pl.BlockSpec((None, rows // n_blocks, cols),
                                     lambda i, every=every, layer=layer: (layer, i // every, 0)))
        out_specs.append(pl.BlockSpec((rows // n_blocks, cols), lambda i, every=every: (i // every, 0)))
        shapes.append(jax.ShapeDtypeStruct((rows, cols), BF16))
    return in_specs, out_specs, shapes


def _cast_slabs(src_refs, dst_refs):
    for src, dst in zip(src_refs, dst_refs):
        dst[...] = src[...].astype(BF16)


def _rope(t, cos, sin_signed):
    lane = lax.broadcasted_iota(jnp.int32, t.shape, 1)
    first_half = (lane & (HEAD_DIM // 2)) == 0
    width = t.shape[-1]
    partner = jnp.where(first_half,
                        pltpu.roll(t, width - HEAD_DIM // 2, axis=1),
                        pltpu.roll(t, HEAD_DIM // 2, axis=1))
    reps = width // LANES
    c = jnp.concatenate([cos] * reps, axis=-1)
    s = jnp.concatenate([sin_signed] * reps, axis=-1)
    return t * c + partner * s


_GATE_COLS = 512


def _in_proj_kernel(x_ref, g_ref, w_ref, cos_ref, sin_ref, *rest, n_cast):
    cast_in, (q_ref, k_ref, v_ref, z_ref, gate_ref), cast_out = (
        rest[:n_cast], rest[n_cast:n_cast + 5], rest[n_cast + 5:])
    _cast_slabs(cast_in, cast_out)
    h = _rms(x_ref[...], g_ref[...]).astype(BF16)
    cos = cos_ref[...]
    sin = sin_ref[...]

    def proj(c0, n):
        return _dot(h, w_ref[:, c0:c0 + n])

    q = _rope(proj(0, ATTN_QK), cos, sin) * (HEAD_DIM ** -0.5)
    q_ref[...] = q.astype(BF16)
    k_ref[...] = _rope(proj(ATTN_QK, ATTN_QK), cos, sin).T.astype(BF16)
    v_ref[...] = proj(2 * ATTN_QK, ATTN_V).astype(BF16)
    u0 = 2 * ATTN_QK + ATTN_V
    a = proj(u0, CONV_CH)
    g = proj(u0 + CONV_CH, CONV_CH)
    z_ref[...] = a * jax.nn.sigmoid(g)
    g0 = u0 + 2 * CONV_CH
    for c in range(0, 2 * D_MODEL, _GATE_COLS):
        gate_ref[:, c:c + _GATE_COLS] = jax.nn.sigmoid(proj(g0 + c, _GATE_COLS)).astype(BF16)


def _in_proj(x, g, w_bf, cos_t, sin_t, seq, to_cast=(), tm=1024):
    t = x.shape[0]
    n_in = w_bf.shape[1]
    seq_blocks = seq // tm
    row = lambda i: (i, 0)
    tab = lambda i: (i % seq_blocks, 0)
    cast_in_specs, cast_out_specs, cast_shapes = _cast_specs(to_cast, t // tm)
    outs = pl.pallas_call(
        functools.partial(_in_proj_kernel, n_cast=len(to_cast)),
        grid=(t // tm,),
        in_specs=[
            pl.BlockSpec((tm, D_MODEL), row),
            _resident((1, D_MODEL)),
            _resident((D_MODEL, n_in)),
            pl.BlockSpec((tm, LANES), tab),
            pl.BlockSpec((tm, LANES), tab),
            *cast_in_specs,
        ],
        out_specs=[
            pl.BlockSpec((tm, ATTN_QK), row),
            pl.BlockSpec((ATTN_QK, tm), lambda i: (0, i)),
            pl.BlockSpec((tm, ATTN_V), row),
            pl.BlockSpec((tm, CONV_CH), row),
            pl.BlockSpec((tm, 2 * D_MODEL), row),
            *cast_out_specs,
        ],
        out_shape=[
            jax.ShapeDtypeStruct((t, ATTN_QK), BF16),
            jax.ShapeDtypeStruct((ATTN_QK, t), BF16),
            jax.ShapeDtypeStruct((t, ATTN_V), BF16),
            jax.ShapeDtypeStruct((t, CONV_CH), F32),
            jax.ShapeDtypeStruct((t, 2 * D_MODEL), BF16),
            *cast_shapes,
        ],
        compiler_params=_cparams(1),
        name="in_proj",
    )(x, g, w_bf, cos_t, sin_t, *[w for w, _ in to_cast])
    return outs[:5], outs[5:]


_ATTN_SUB = 256


_ATTN_HEADS = 2


def _attn_kernel(lam_ref, q_ref, kt_ref, v_ref, g_ref, o_ref, v1_ref, *, out_scale):
    for hh in range(_ATTN_HEADS):
        head = slice(hh * V_DIM, (hh + 1) * V_DIM)
        v1_ref[hh, :, 0:V_DIM] = v_ref[:, head]
        v1_ref[hh, :, V_DIM:2 * V_DIM] = jnp.ones((v_ref.shape[0], V_DIM), BF16)
        kt = kt_ref[head, :]
        v1 = v1_ref[hh]

        def softmax_pv(qc):
            s = _dot(qc, kt)
            p = jnp.exp(s - jnp.max(s, axis=-1, keepdims=True)).astype(BF16)
            ov = _dot(p, v1)
            return ov[:, 0:V_DIM] / ov[:, V_DIM:V_DIM + 1]

        for r0 in range(0, q_ref.shape[0], _ATTN_SUB):
            q = q_ref[r0:r0 + _ATTN_SUB, head]
            lane = lax.broadcasted_iota(jnp.int32, q.shape, 1)
            zero = jnp.zeros_like(q)
            o0 = softmax_pv(jnp.where(lane < HEAD_DIM, q, zero))
            o1 = softmax_pv(jnp.where(lane >= HEAD_DIM, q, zero))
            o = o0 - lam_ref[0] * o1
            o_ref[r0:r0 + _ATTN_SUB, head] = (_rms(o, g_ref[...]) * out_scale).astype(BF16)


def _attention(q, kt, v, lam_full, g_subln, batch, seq, out_scale):
    t = q.shape[0]
    width = _ATTN_HEADS * V_DIM
    block = pl.BlockSpec((seq, width), lambda b, h, lam: (b, h))
    grid_spec = pltpu.PrefetchScalarGridSpec(
        num_scalar_prefetch=1,
        grid=(batch, N_HEADS // _ATTN_HEADS),
        in_specs=[block, pl.BlockSpec((width, seq), lambda b, h, lam: (h, b)), block,
                  pl.BlockSpec((1, V_DIM), lambda b, h, lam: (0, 0))],
        out_specs=block,
        scratch_shapes=[pltpu.VMEM((_ATTN_HEADS, seq, 2 * V_DIM), BF16)],
    )
    return pl.pallas_call(
        functools.partial(_attn_kernel, out_scale=out_scale),
        grid_spec=grid_spec,
        out_shape=jax.ShapeDtypeStruct((t, ATTN_V), BF16),
        compiler_params=_cparams(2),
        name="diff_attn",
    )(lam_full, q, kt, v, g_subln)


_CONV_HALO = 16
_CONV_ROWS = 128


def _conv_block(zp_ref, w_ref, r0, c0):
    first = _CONV_HALO - CONV_PAD
    window = _CONV_ROWS + SUBLANES
    n_aligned = (CONV_K + first + SUBLANES - 1) // SUBLANES
    acc = None
    for r in reversed(range(SUBLANES)):
        y_r = None
        for a in range(n_aligned):
            tap = SUBLANES * a + r - first
            if 0 <= tap < CONV_K:
                lo = r0 + SUBLANES * a
                term = zp_ref[lo:lo + window, c0:c0 + LANES] * w_ref[tap:tap + 1, c0:c0 + LANES]
                y_r = term if y_r is None else y_r + term
        if acc is None:
            acc = y_r
        else:
            shifted = pltpu.roll(acc, window - 1, axis=0)
            acc = shifted if y_r is None else y_r + shifted
    return acc[0:_CONV_ROWS, :]


def _conv_tail(y, b_ref, lg_ref, lb_ref):
    y = y + b_ref[...]
    mu = jnp.mean(y, axis=-1, keepdims=True)
    yc = y - mu
    yn = yc * lax.rsqrt(jnp.mean(yc * yc, axis=-1, keepdims=True) + EPS) * lg_ref[...] + lb_ref[...]
    return (yn * jax.nn.sigmoid(yn)).astype(BF16)


_MIX_SUB = 512


def _mixed_residual(x_ref, o_ref, zprev_ref, z_ref, znext_ref, gate_ref, cw_ref, cb_ref, lg_ref, lb_ref,
                    wa_ref, wc_ref, wo_ref, out_ref, zp_ref, *, tiles_per_seq):
    tm = x_ref.shape[0]
    pos = pl.program_id(0) % tiles_per_seq
    zp_ref[0:_CONV_HALO, :] = jnp.where(pos > 0, zprev_ref[0], 0.0)
    zp_ref[_CONV_HALO:_CONV_HALO + tm, :] = z_ref[...]
    zp_ref[_CONV_HALO + tm:_CONV_HALO + tm + _CONV_HALO, :] = jnp.where(
        pos < tiles_per_seq - 1, znext_ref[0], 0.0)
    for r0 in range(0, tm, _MIX_SUB):
        rows = slice(r0, r0 + _MIX_SUB)
        conv = jnp.concatenate(
            [jnp.concatenate([_conv_block(zp_ref, cw_ref, r, c0) for c0 in range(0, CONV_CH, LANES)], axis=-1)
             for r in range(r0, r0 + _MIX_SUB, _CONV_ROWS)], axis=0)
        c = _conv_tail(conv, cb_ref, lg_ref, lb_ref)
        attn_d = _dot(o_ref[rows, :], wa_ref[...])
        conv_d = _dot(c, wc_ref[...])
        ga = gate_ref[rows, 0:D_MODEL].astype(F32)
        gc = gate_ref[rows, D_MODEL:2 * D_MODEL].astype(F32)
        mix = (ga * attn_d + gc * conv_d).astype(BF16)
        out_ref[rows, :] = x_ref[rows, :] + _dot(mix, wo_ref[...])


def _mix_kernel(*refs, tiles_per_seq):
    _mixed_residual(*refs, tiles_per_seq=tiles_per_seq)


def _mix_route_kernel(*refs, tiles_per_seq):
    mix_refs, (g_ref, wr_hi_ref, wr_lo_ref, br_ref) = refs[:13], refs[13:17]
    out_ref, meta_ref, plan_ref, cnt_ref, zp_ref, carry_ref = refs[17:]
    _mixed_residual(*mix_refs, out_ref, zp_ref, tiles_per_seq=tiles_per_seq)
    _route(_rms(out_ref[...], g_ref[...]), wr_hi_ref, wr_lo_ref, br_ref, meta_ref, plan_ref, cnt_ref,
           carry_ref)


def _mix(x, o, z, gates, conv_w, conv_b, ln_g, ln_b, wa, wc, wo, seq, router=None, tm=512):
    t = x.shape[0]
    row = lambda i: (i, 0)
    halo_per_tile = tm // _CONV_HALO
    n_halo = t // _CONV_HALO
    z_halo = z.reshape(n_halo, _CONV_HALO, CONV_CH)
    in_specs = [
        pl.BlockSpec((tm, D_MODEL), row),
        pl.BlockSpec((tm, ATTN_V), row),
        pl.BlockSpec((1, _CONV_HALO, CONV_CH), lambda i: (jnp.maximum(i * halo_per_tile - 1, 0), 0, 0)),
        pl.BlockSpec((tm, CONV_CH), row),
        pl.BlockSpec((1, _CONV_HALO, CONV_CH),
                     lambda i: (jnp.minimum((i + 1) * halo_per_tile, n_halo - 1), 0, 0)),
        pl.BlockSpec((tm, 2 * D_MODEL), row),
        _resident(conv_w.shape), _resident(conv_b.shape), _resident(ln_g.shape), _resident(ln_b.shape),
        _resident(wa.shape), _resident(wc.shape), _resident(wo.shape),
    ]
    args = (x, o, z_halo, z, z_halo, gates, conv_w, conv_b, ln_g, ln_b, wa, wc, wo)
    x_spec = pl.BlockSpec((tm, D_MODEL), row)
    x_shape = jax.ShapeDtypeStruct((t, D_MODEL), F32)
    zp_scratch = pltpu.VMEM((tm + 2 * _CONV_HALO, CONV_CH), F32)
    tiles_per_seq = seq // tm
    if router is None:
        return pl.pallas_call(
            functools.partial(_mix_kernel, tiles_per_seq=tiles_per_seq),
            grid=(t // tm,), in_specs=in_specs, out_specs=x_spec, out_shape=x_shape,
            scratch_shapes=[zp_scratch], compiler_params=_cparams(1), name="branch_mix",
        )(*args)
    return pl.pallas_call(
        functools.partial(_mix_route_kernel, tiles_per_seq=tiles_per_seq),
        grid=(t // tm,),
        in_specs=in_specs + [_resident(a.shape) for a in router],
        out_specs=[x_spec, pl.BlockSpec((tm, LANES), row), pl.BlockSpec((SUBLANES, tm), lambda i: (0, i)),
                   pl.BlockSpec((1, LANES), lambda i: (0, 0))],
        out_shape=[x_shape, jax.ShapeDtypeStruct((t, LANES), F32), jax.ShapeDtypeStruct((SUBLANES, t), F32),
                   jax.ShapeDtypeStruct((1, LANES), F32)],
        scratch_shapes=[zp_scratch, pltpu.VMEM((1, LANES), F32)],
        compiler_params=_cparams(1),
        name="branch_mix_route",
    )(*args, *router)


def _swiglu(h, wg_ref, wu_ref, wd_ref):
    g = _dot(h, wg_ref[...])
    u = _dot(h, wu_ref[...])
    a = (g * jax.nn.sigmoid(g) * u).astype(BF16)
    return _dot(a, wd_ref[...])


def _ffn_kernel(x_ref, g_ref, wg_ref, wu_ref, wd_ref, p_ref, gp_ref, wpg_ref, wpp_ref, gf_ref, *rest,
                n_cast, final_norm):
    cast_in, out_ref, cast_out = rest[:n_cast], rest[n_cast], rest[n_cast + 1:]
    x = x_ref[...]
    h = _rms(x, g_ref[...]).astype(BF16)
    x = x + _swiglu(h, wg_ref, wu_ref, wd_ref)
    _ple_tail(x, p_ref, gp_ref, wpg_ref, wpp_ref, gf_ref, out_ref, final_norm)
    _cast_slabs(cast_in, cast_out)


def _ffn_ple(x, g, w_gu, w_down, p, g_ple, w_gate, w_proj, g_final, final_norm, to_cast=(), tm=256):
    t = x.shape[0]
    d_ff = w_down.shape[0]
    steps = t // tm
    row = lambda i: (i, 0)
    cast_in_specs, cast_out_specs, cast_shapes = _cast_specs(to_cast, steps)
    outs = pl.pallas_call(
        functools.partial(_ffn_kernel, n_cast=len(to_cast), final_norm=final_norm),
        grid=(steps,),
        in_specs=[
            pl.BlockSpec((tm, D_MODEL), row),
            _resident((1, D_MODEL)),
            pl.BlockSpec((D_MODEL, d_ff), lambda i: (0, 0), pipeline_mode=pl.Buffered(1)),
            pl.BlockSpec((D_MODEL, d_ff), lambda i: (0, 1), pipeline_mode=pl.Buffered(1)),
            _resident((d_ff, D_MODEL)),
            pl.BlockSpec((None, tm, p[0].shape[2]), lambda i, layer=p[1]: (layer, i, 0)),
            _resident((1, D_MODEL)),
            _resident(w_gate.shape),
            _resident(w_proj.shape),
            _resident((1, D_MODEL)),
            *cast_in_specs,
        ],
        out_specs=[pl.BlockSpec((tm, D_MODEL), row), *cast_out_specs],
        out_shape=[jax.ShapeDtypeStruct((t, D_MODEL), F32), *cast_shapes],
        compiler_params=_cparams(1),
        name="dense_ffn_ple",
    )(x, g, w_gu, w_gu, w_down, p[0], g_ple, w_gate, w_proj, g_final, *[w for w, _ in to_cast])
    return outs[0], outs[1:]


TM_EXPERT = 256
META_E1, META_E2, META_R1, META_R2, META_W1, META_W2 = range(6)


def _row_copy(src_ref, src_row, dst_ref, dst_row, sem):
    return pltpu.make_async_copy(src_ref.at[pl.ds(src_row, 1)], dst_ref.at[pl.ds(dst_row, 1)], sem)


def _route(hf, wr_hi_ref, wr_lo_ref, br_ref, meta_ref, plan_ref, cnt_ref, carry_ref):
    @pl.when(pl.program_id(0) == 0)
    def _():
        carry_ref[...] = jnp.zeros_like(carry_ref)

    h_hi = hf.astype(BF16)
    h_lo = (hf - h_hi.astype(F32)).astype(BF16)
    logits = (_dot(h_hi, wr_hi_ref[...]) + _dot(h_lo, wr_hi_ref[...])
              + _dot(h_hi, wr_lo_ref[...])) + br_ref[...]
    tm = logits.shape[0]
    lane = lax.broadcasted_iota(jnp.int32, logits.shape, 1)
    neg = jnp.full_like(logits, -jnp.inf)
    logits = jnp.where(lane < N_EXPERTS, logits, neg)
    m1 = jnp.max(logits, axis=-1, keepdims=True)
    i1 = jnp.min(jnp.where(logits == m1, lane, LANES), axis=-1, keepdims=True)
    rest = jnp.where(lane == i1, neg, logits)
    m2 = jnp.max(rest, axis=-1, keepdims=True)
    i2 = jnp.min(jnp.where(rest == m2, lane, LANES), axis=-1, keepdims=True)
    e2 = jnp.exp(m2 - m1)
    denom = 1.0 + e2
    pick1 = lane == i1
    pick2 = lane == i2
    onehot = jnp.where(pick1 | pick2, 1.0, 0.0)
    r = lax.broadcasted_iota(jnp.int32, (tm, tm), 0)
    c = lax.broadcasted_iota(jnp.int32, (tm, tm), 1)
    strict_lower = jnp.where(r > c, 1.0, 0.0).astype(BF16)
    before = _dot(strict_lower, onehot.astype(BF16)) + carry_ref[...]
    rank1 = jnp.sum(jnp.where(pick1, before, 0.0), axis=-1, keepdims=True)
    rank2 = jnp.sum(jnp.where(pick2, before, 0.0), axis=-1, keepdims=True)
    record = jnp.zeros_like(logits)
    for slot, val in ((META_E1, i1.astype(F32)), (META_E2, i2.astype(F32)),
                      (META_R1, rank1), (META_R2, rank2),
                      (META_W1, 1.0 / denom), (META_W2, e2 / denom)):
        record = jnp.where(lane == slot, val, record)
    meta_ref[...] = record
    plan_ref[...] = record.T[0:SUBLANES, :]
    carry_ref[...] += jnp.sum(onehot, axis=0, keepdims=True)
    cnt_ref[...] = carry_ref[...]


def _dispatch_kernel(ends_ref, sizes_ref, x_ref, p1_ref, p2_ref, xs_ref, zero_ref, row_sem, zero_sem):
    tm = x_ref.shape[0]

    @pl.when(pl.program_id(0) == 0)
    def _():
        zero_ref[...] = jnp.zeros_like(zero_ref)
        n_tiles = xs_ref.shape[0] // TM_EXPERT
        total = ends_ref[N_EXPERTS - 1]

        def zero_tile(row):
            start = pl.multiple_of(row, TM_EXPERT)
            return pltpu.make_async_copy(zero_ref, xs_ref.at[pl.ds(start, TM_EXPERT)], zero_sem)

        fills = [(sizes_ref[e] > 0, ends_ref[e] - TM_EXPERT) for e in range(N_EXPERTS)]
        fills += [(j * TM_EXPERT >= total, j * TM_EXPERT) for j in range(n_tiles - N_EXPERTS, n_tiles)]
        for needed, row in fills:
            @pl.when(needed)
            def _():
                zero_tile(row).start()
        for needed, row in fills:
            @pl.when(needed)
            def _():
                zero_tile(row).wait()

    def scatter_row(r, carry):
        _row_copy(x_ref, r, xs_ref, p1_ref[0, 0, r], row_sem).start(priority=0)
        _row_copy(x_ref, r, xs_ref, p2_ref[0, 0, r], row_sem).start(priority=1)
        return carry

    lax.fori_loop(0, tm, scatter_row, 0, unroll=True)
    for _ in range(2):
        pltpu.make_async_copy(x_ref, xs_ref.at[pl.ds(0, tm)], row_sem).wait()


def _dispatch(x, pos1, pos2, ends, sizes, n_rows, tm=512):
    t = x.shape[0]
    idx = lambda a: a.reshape(t // tm, 1, tm)
    smem_block = pl.BlockSpec((1, 1, tm), lambda i, *_: (i, 0, 0), memory_space=pltpu.SMEM)
    grid_spec = pltpu.PrefetchScalarGridSpec(
        num_scalar_prefetch=2,
        grid=(t // tm,),
        in_specs=[pl.BlockSpec((tm, D_MODEL), lambda i, *_: (i, 0)), smem_block, smem_block],
        out_specs=pl.BlockSpec(memory_space=pl.ANY),
        scratch_shapes=[pltpu.VMEM((TM_EXPERT, D_MODEL), F32),
                        pltpu.SemaphoreType.DMA(()), pltpu.SemaphoreType.DMA(())],
    )
    return pl.pallas_call(
        _dispatch_kernel,
        grid_spec=grid_spec,
        out_shape=jax.ShapeDtypeStruct((n_rows, D_MODEL), F32),
        compiler_params=_cparams(1),
        name="moe_dispatch",
    )(ends, sizes, x, idx(pos1), idx(pos2))


def _expert_kernel(tile_expert_ref, tile_src_ref, x_ref, g_ref, wg_ref, wu_ref, wd_ref, y_ref):
    j = pl.program_id(0)

    @pl.when(tile_src_ref[j] != j)
    def _():
        y_ref[...] = jnp.zeros_like(y_ref)

    @pl.when(tile_src_ref[j] == j)
    def _():
        xb = _rms(x_ref[...], g_ref[...]).astype(BF16)
        half = wd_ref.shape[0] // 2
        y = None
        for c0 in (0, half):
            g = _dot(xb, wg_ref[:, c0:c0 + half])
            u = _dot(xb, wu_ref[:, c0:c0 + half])
            a = (g * jax.nn.sigmoid(g) * u).astype(BF16)
            part = _dot(a, wd_ref[c0:c0 + half, :])
            y = part if y is None else y + part
        y_ref[...] = y


def _experts(xs, g, tile_expert, tile_src, we_gu, we_down):
    n_rows = xs.shape[0]
    d_e = we_down.shape[1]
    tile = lambda j, te, ts: (ts[j], 0)
    grid_spec = pltpu.PrefetchScalarGridSpec(
        num_scalar_prefetch=2,
        grid=(n_rows // TM_EXPERT,),
        in_specs=[
            pl.BlockSpec((TM_EXPERT, D_MODEL), tile),
            pl.BlockSpec((1, D_MODEL), lambda j, te, ts: (0, 0)),
            pl.BlockSpec((None, D_MODEL, d_e), lambda j, te, ts: (te[j], 0, 0)),
            pl.BlockSpec((None, D_MODEL, d_e), lambda j, te, ts: (te[j], 0, 1)),
            pl.BlockSpec((None, d_e, D_MODEL), lambda j, te, ts: (te[j], 0, 0)),
        ],
        out_specs=pl.BlockSpec((TM_EXPERT, D_MODEL), lambda j, te, ts: (j, 0)),
    )
    return pl.pallas_call(
        _expert_kernel,
        grid_spec=grid_spec,
        out_shape=jax.ShapeDtypeStruct((n_rows, D_MODEL), F32),
        compiler_params=pltpu.CompilerParams(
            dimension_semantics=("arbitrary",), vmem_limit_bytes=VMEM_LIMIT_EXPERT),
        name="moe_experts",
    )(tile_expert, tile_src, xs, g, we_gu, we_gu, we_down)


def _ple_tail(x, p_ref, g_ref, wg_ref, wp_ref, gf_ref, out_ref, final_norm):
    h = _rms(x, g_ref[...]).astype(BF16)
    gate = jax.nn.sigmoid(_dot(h, wg_ref[...]))
    y = x + gate * _dot(p_ref[...].astype(BF16), wp_ref[...])
    if final_norm:
        y = _rms(y, gf_ref[...])
    out_ref[...] = y


def _combine_ple_kernel(x_ref, meta_ref, p1_ref, p2_ref, p1_next_ref, p2_next_ref, y_ref,
                        p_ref, g_ref, wg_ref, wp_ref, gf_ref, out_ref, rows_ref, sems, *, final_norm):
    i = pl.program_id(0)
    n = pl.num_programs(0)
    tm = x_ref.shape[0]
    slot = i % 2

    def gather(p1, p2, s):
        def gather_row(r, carry):
            _row_copy(y_ref, p1[0, 0, r], rows_ref.at[s, 0], r, sems.at[s]).start(priority=0)
            _row_copy(y_ref, p2[0, 0, r], rows_ref.at[s, 1], r, sems.at[s]).start(priority=1)
            return carry
        lax.fori_loop(0, tm, gather_row, 0, unroll=True)

    @pl.when(i == 0)
    def _():
        gather(p1_ref, p2_ref, 0)

    @pl.when(i + 1 < n)
    def _():
        gather(p1_next_ref, p2_next_ref, 1 - slot)

    for k in range(2):
        pltpu.make_async_copy(y_ref.at[pl.ds(0, tm)], rows_ref.at[slot, k], sems.at[slot]).wait()
    meta = meta_ref[...]
    x = (x_ref[...] + meta[:, META_W1:META_W1 + 1] * rows_ref[slot, 0]
         + meta[:, META_W2:META_W2 + 1] * rows_ref[slot, 1])
    _ple_tail(x, p_ref, g_ref, wg_ref, wp_ref, gf_ref, out_ref, final_norm)


def _combine_ple(x, meta, pos1, pos2, y, p, g, w_gate, w_proj, g_final, final_norm, tm=512):
    t = x.shape[0]
    nb = t // tm
    p_all, layer = p
    row = lambda i: (i, 0)
    idx = lambda a: a.reshape(nb, 1, tm)
    cur = pl.BlockSpec((1, 1, tm), lambda i: (i, 0, 0), memory_space=pltpu.SMEM)
    nxt = pl.BlockSpec((1, 1, tm), lambda i: (jnp.minimum(i + 1, nb - 1), 0, 0),
                       memory_space=pltpu.SMEM)
    return pl.pallas_call(
        functools.partial(_combine_ple_kernel, final_norm=final_norm),
        grid=(nb,),
        in_specs=[
            pl.BlockSpec((tm, D_MODEL), row),
            pl.BlockSpec((tm, LANES), row),
            cur, cur, nxt, nxt,
            pl.BlockSpec(memory_space=pl.ANY),
            pl.BlockSpec((None, tm, p_all.shape[2]), lambda i: (layer, i, 0)),
            _resident((1, D_MODEL)),
            _resident(w_gate.shape),
            _resident(w_proj.shape),
            _resident((1, D_MODEL)),
        ],
        out_specs=pl.BlockSpec((tm, D_MODEL), row),
        out_shape=jax.ShapeDtypeStruct((t, D_MODEL), F32),
        scratch_shapes=[pltpu.VMEM((2, 2, tm, D_MODEL), F32), pltpu.SemaphoreType.DMA((2,))],
        compiler_params=_cparams(1),
        name="moe_combine_ple",
    )(x, meta, idx(pos1), idx(pos2), idx(pos1), idx(pos2), y, p_all, g, w_gate, w_proj, g_final)


def _moe_plan(plan, counts):
    cnt = counts[0, :N_EXPERTS].astype(jnp.int32)
    sizes = (cnt + TM_EXPERT - 1) // TM_EXPERT * TM_EXPERT
    ends = jnp.cumsum(sizes)
    starts = ends - sizes

    def start_of(e):
        out = jnp.zeros_like(e)
        for j in range(N_EXPERTS):
            out = jnp.where(e == j, starts[j], out)
        return out

    pos1 = start_of(plan[META_E1].astype(jnp.int32)) + plan[META_R1].astype(jnp.int32)
    pos2 = start_of(plan[META_E2].astype(jnp.int32)) + plan[META_R2].astype(jnp.int32)
    n_rows = 2 * plan.shape[1] + N_EXPERTS * TM_EXPERT
    n_valid = ends[-1] // TM_EXPERT
    tile_src = jnp.minimum(jnp.arange(n_rows // TM_EXPERT, dtype=jnp.int32), n_valid - 1)
    tile_expert = jnp.sum((tile_src * TM_EXPERT)[:, None] >= ends[None, :], axis=1).astype(jnp.int32)
    return pos1, pos2, ends.astype(jnp.int32), sizes, tile_expert, tile_src, n_rows


def _rope_tables(seq):
    inv_freq = ROPE_THETA ** (-jnp.arange(0, HEAD_DIM, 2, dtype=F32) / HEAD_DIM)
    ang = jnp.arange(seq, dtype=F32)[:, None] * inv_freq[None, :]
    cos, sin = jnp.cos(ang), jnp.sin(ang)
    cos_t = jnp.concatenate([cos] * (LANES // (HEAD_DIM // 2)), axis=-1)
    sin_t = jnp.concatenate([-sin, sin] * (LANES // HEAD_DIM), axis=-1)
    return cos_t, sin_t


def kernel(x, p, g_mix, w_in, lam, g_subln, w_attn_out, conv_w, conv_b, conv_ln_g, conv_ln_b, w_conv_out, w_o, g_ffn, w_ff_gu, w_ff_down, w_router, b_router, we_gu, we_down, g_ple, w_ple_gate, w_ple_proj, g_final):
    batch, seq, _ = x.shape
    depth = w_in.shape[0]
    t = batch * seq
    cos_t, sin_t = _rope_tables(seq)
    xs = x.reshape(t, D_MODEL)
    row2 = lambda a: a.reshape(1, -1)
    expert_bf16 = {}
    w_in_bf16 = w_in[0].astype(BF16)
    for i in range(depth):
        later = {"attn_out": (w_attn_out, i), "conv_out": (w_conv_out, i), "w_o": (w_o, i),
                 "ple_gate": (w_ple_gate, i), "ple_proj": (w_ple_proj, i)}
        if i % 2 == 0:
            later.update(ff_gu=(w_ff_gu, i // 2), ff_down=(w_ff_down, i // 2))
        if i + 1 < depth:
            later["w_in_next"] = (w_in, i + 1)
        (q, k, v, z, gates), narrowed = _in_proj(xs, row2(g_mix[i]), w_in_bf16, cos_t, sin_t, seq,
                                                 tuple(later.values()))
        wb = dict(zip(later.keys(), narrowed))
        w_in_bf16 = wb.get("w_in_next")
        lam_init = 0.8 - 0.6 * math.exp(-0.3 * i)
        lf = lam[i].astype(F32)
        lam_full = (jnp.exp(jnp.sum(lf[0] * lf[1])) - jnp.exp(jnp.sum(lf[2] * lf[3]))
                    + lam_init).reshape(1)
        o = _attention(q, k, v, lam_full, row2(g_subln[i]), batch, seq, 1.0 - lam_init)
        mix_args = (xs, o, z, gates, conv_w[i], row2(conv_b[i]), row2(conv_ln_g[i]), row2(conv_ln_b[i]),
                    wb["attn_out"], wb["conv_out"], wb["w_o"], seq)
        ple_args = ((p.reshape(depth, t, -1), i), row2(g_ple[i]), wb["ple_gate"], wb["ple_proj"],
                    row2(g_final))
        final_norm = i == depth - 1
        if i % 2 == 0:
            xs = _mix(*mix_args)
            m_next = i // 2
            to_cast = ()
            if i + 1 < depth:
                to_cast = ((we_gu.reshape(we_gu.shape[0], -1, we_gu.shape[-1]), m_next),
                           (we_down.reshape(we_down.shape[0], -1, we_down.shape[-1]), m_next))
            xs, cast = _ffn_ple(xs, row2(g_ffn[i]), wb["ff_gu"], wb["ff_down"], *ple_args, final_norm, to_cast)
            if cast:
                expert_bf16[m_next] = (cast[0].reshape(we_gu.shape[1:]), cast[1].reshape(we_down.shape[1:]))
        else:
            m = i // 2
            wr = jnp.pad(w_router[m], ((0, 0), (0, LANES - N_EXPERTS)))
            wr_hi = wr.astype(BF16)
            wr_lo = (wr - wr_hi.astype(F32)).astype(BF16)
            br = jnp.pad(b_router[m], (0, LANES - N_EXPERTS)).reshape(1, LANES)
            xs, meta, plan, counts = _mix(*mix_args, router=(row2(g_ffn[i]), wr_hi, wr_lo, br))
            pos1, pos2, ends, sizes, tile_expert, tile_src, n_rows = _moe_plan(plan, counts)
            routed = _dispatch(xs, pos1, pos2, ends, sizes, n_rows)
            if m not in expert_bf16:
                expert_bf16[m] = (we_gu[m].astype(BF16), we_down[m].astype(BF16))
            y = _experts(routed, row2(g_ffn[i]), tile_expert, tile_src, *expert_bf16[m])
            xs = _combine_ple(xs, meta, pos1, pos2, y, *ple_args, final_norm=final_norm)
    return xs.reshape(batch, seq, D_MODEL)
```

```python
import functools
import math

import jax
import jax.numpy as jnp
from jax import lax
from jax.experimental import pallas as pl
from jax.experimental.pallas import tpu as pltpu

D_MODEL = 1024
N_HEADS = 4
HEAD_DIM = 64
V_DIM = 2 * HEAD_DIM
ATTN_QK = N_HEADS * 2 * HEAD_DIM
ATTN_V = N_HEADS * V_DIM
ROPE_THETA = 10000.0
CONV_CH = 512
CONV_K = 31
CONV_PAD = (CONV_K - 1) // 2
N_EXPERTS = 8
EPS = 1e-6

LANES = 128
SUBLANES = 8
VMEM_LIMIT = 56 * 1024 * 1024
VMEM_LIMIT_EXPERT = 60 * 1024 * 1024

F32 = jnp.float32
BF16 = jnp.bfloat16


def _cparams(n_axes):
    return pltpu.CompilerParams(
        dimension_semantics=("arbitrary",) * n_axes, vmem_limit_bytes=VMEM_LIMIT)


def _resident(shape):
    nd = len(shape)
    return pl.BlockSpec(shape, lambda *_: (0,) * nd, pipeline_mode=pl.Buffered(1))


def _rms(x, g):
    return x * lax.rsqrt(jnp.mean(x * x, axis=-1, keepdims=True) + EPS) * g


def _dot(a, b):
    return jnp.dot(a, b, preferred_element_type=F32)


BF16_ROW_TILE = 2 * SUBLANES


def _cast_specs(to_cast, steps):
    in_specs, out_specs, shapes = [], [], []
    for w, layer in to_cast:
        _, rows, cols = w.shape
        n_blocks = max(n for n in range(1, steps + 1)
                       if steps % n == 0 and rows % (n * BF16_ROW_TILE) == 0)
        every = steps // n_blocks
        in_specs.append(pl.BlockSpec((None, rows // n_blocks, cols),
                                     lambda i, every=every, layer=layer: (layer, i // every, 0)))
        out_specs.append(pl.BlockSpec((rows // n_blocks, cols), lambda i, every=every: (i // every, 0)))
        shapes.append(jax.ShapeDtypeStruct((rows, cols), BF16))
    return in_specs, out_specs, shapes


def _cast_slabs(src_refs, dst_refs):
    for src, dst in zip(src_refs, dst_refs):
        dst[...] = src[...].astype(BF16)


def _rope(t, cos, sin_signed):
    lane = lax.broadcasted_iota(jnp.int32, t.shape, 1)
    first_half = (lane & (HEAD_DIM // 2)) == 0
    width = t.shape[-1]
    partner = jnp.where(first_half,
                        pltpu.roll(t, width - HEAD_DIM // 2, axis=1),
                        pltpu.roll(t, HEAD_DIM // 2, axis=1))
    reps = width // LANES
    c = jnp.concatenate([cos] * reps, axis=-1)
    s = jnp.concatenate([sin_signed] * reps, axis=-1)
    return t * c + partner * s


_GATE_COLS = 512


def _in_proj_kernel(x_ref, g_ref, w_ref, cos_ref, sin_ref, *rest, n_cast):
    cast_in, (q_ref, k_ref, v_ref, z_ref, gate_ref), cast_out = (
        rest[:n_cast], rest[n_cast:n_cast + 5], rest[n_cast + 5:])
    _cast_slabs(cast_in, cast_out)
    h = _rms(x_ref[...], g_ref[...]).astype(BF16)
    cos = cos_ref[...]
    sin = sin_ref[...]

    def proj(c0, n):
        return _dot(h, w_ref[:, c0:c0 + n])

    q = _rope(proj(0, ATTN_QK), cos, sin) * (HEAD_DIM ** -0.5)
    q_ref[...] = q.astype(BF16)
    k_ref[...] = _rope(proj(ATTN_QK, ATTN_QK), cos, sin).T.astype(BF16)
    v_ref[...] = proj(2 * ATTN_QK, ATTN_V).astype(BF16)
    u0 = 2 * ATTN_QK + ATTN_V
    a = proj(u0, CONV_CH)
    g = proj(u0 + CONV_CH, CONV_CH)
    z_ref[...] = a * jax.nn.sigmoid(g)
    g0 = u0 + 2 * CONV_CH
    for c in range(0, 2 * D_MODEL, _GATE_COLS):
        gate_ref[:, c:c + _GATE_COLS] = jax.nn.sigmoid(proj(g0 + c, _GATE_COLS)).astype(BF16)


def _in_proj(x, g, w_bf, cos_t, sin_t, seq, to_cast=(), tm=1024):
    t = x.shape[0]
    n_in = w_bf.shape[1]
    seq_blocks = seq // tm
    row = lambda i: (i, 0)
    tab = lambda i: (i % seq_blocks, 0)
    cast_in_specs, cast_out_specs, cast_shapes = _cast_specs(to_cast, t // tm)
    outs = pl.pallas_call(
        functools.partial(_in_proj_kernel, n_cast=len(to_cast)),
        grid=(t // tm,),
        in_specs=[
            pl.BlockSpec((tm, D_MODEL), row),
            _resident((1, D_MODEL)),
            _resident((D_MODEL, n_in)),
            pl.BlockSpec((tm, LANES), tab),
            pl.BlockSpec((tm, LANES), tab),
            *cast_in_specs,
        ],
        out_specs=[
            pl.BlockSpec((tm, ATTN_QK), row),
            pl.BlockSpec((ATTN_QK, tm), lambda i: (0, i)),
            pl.BlockSpec((tm, ATTN_V), row),
            pl.BlockSpec((tm, CONV_CH), row),
            pl.BlockSpec((tm, 2 * D_MODEL), row),
            *cast_out_specs,
        ],
        out_shape=[
            jax.ShapeDtypeStruct((t, ATTN_QK), BF16),
            jax.ShapeDtypeStruct((ATTN_QK, t), BF16),
            jax.ShapeDtypeStruct((t, ATTN_V), BF16),
            jax.ShapeDtypeStruct((t, CONV_CH), F32),
            jax.ShapeDtypeStruct((t, 2 * D_MODEL), BF16),
            *cast_shapes,
        ],
        compiler_params=_cparams(1),
        name="in_proj",
    )(x, g, w_bf, cos_t, sin_t, *[w for w, _ in to_cast])
    return outs[:5], outs[5:]


_ATTN_SUB = 256


_ATTN_HEADS = 2


def _attn_kernel(lam_ref, q_ref, kt_ref, v_ref, g_ref, o_ref, v1_ref, *, out_scale):
    for hh in range(_ATTN_HEADS):
        head = slice(hh * V_DIM, (hh + 1) * V_DIM)
        v1_ref[hh, :, 0:V_DIM] = v_ref[:, head]
        v1_ref[hh, :, V_DIM:2 * V_DIM] = jnp.ones((v_ref.shape[0], V_DIM), BF16)
        kt = kt_ref[head, :]
        v1 = v1_ref[hh]

        def softmax_pv(qc):
            s = _dot(qc, kt)
            p = jnp.exp(s - jnp.max(s, axis=-1, keepdims=True)).astype(BF16)
            ov = _dot(p, v1)
            return ov[:, 0:V_DIM] / ov[:, V_DIM:V_DIM + 1]

        for r0 in range(0, q_ref.shape[0], _ATTN_SUB):
            q = q_ref[r0:r0 + _ATTN_SUB, head]
            lane = lax.broadcasted_iota(jnp.int32, q.shape, 1)
            zero = jnp.zeros_like(q)
            o0 = softmax_pv(jnp.where(lane < HEAD_DIM, q, zero))
            o1 = softmax_pv(jnp.where(lane >= HEAD_DIM, q, zero))
            o = o0 - lam_ref[0] * o1
            o_ref[r0:r0 + _ATTN_SUB, head] = (_rms(o, g_ref[...]) * out_scale).astype(BF16)


def _attention(q, kt, v, lam_full, g_subln, batch, seq, out_scale):
    t = q.shape[0]
    width = _ATTN_HEADS * V_DIM
    block = pl.BlockSpec((seq, width), lambda b, h, lam: (b, h))
    grid_spec = pltpu.PrefetchScalarGridSpec(
        num_scalar_prefetch=1,
        grid=(batch, N_HEADS // _ATTN_HEADS),
        in_specs=[block, pl.BlockSpec((width, seq), lambda b, h, lam: (h, b)), block,
                  pl.BlockSpec((1, V_DIM), lambda b, h, lam: (0, 0))],
        out_specs=block,
        scratch_shapes=[pltpu.VMEM((_ATTN_HEADS, seq, 2 * V_DIM), BF16)],
    )
    return pl.pallas_call(
        functools.partial(_attn_kernel, out_scale=out_scale),
        grid_spec=grid_spec,
        out_shape=jax.ShapeDtypeStruct((t, ATTN_V), BF16),
        compiler_params=_cparams(2),
        name="diff_attn",
    )(lam_full, q, kt, v, g_subln)


_CONV_HALO = 16
_CONV_ROWS = 128


def _conv_block(zp_ref, w_ref, r0, c0):
    first = _CONV_HALO - CONV_PAD
    window = _CONV_ROWS + SUBLANES
    n_aligned = (CONV_K + first + SUBLANES - 1) // SUBLANES
    acc = None
    for r in reversed(range(SUBLANES)):
        y_r = None
        for a in range(n_aligned):
            tap = SUBLANES * a + r - first
            if 0 <= tap < CONV_K:
                lo = r0 + SUBLANES * a
                term = zp_ref[lo:lo + window, c0:c0 + LANES] * w_ref[tap:tap + 1, c0:c0 + LANES]
                y_r = term if y_r is None else y_r + term
        if acc is None:
            acc = y_r
        else:
            shifted = pltpu.roll(acc, window - 1, axis=0)
            acc = shifted if y_r is None else y_r + shifted
    return acc[0:_CONV_ROWS, :]


def _conv_tail(y, b_ref, lg_ref, lb_ref):
    y = y + b_ref[...]
    mu = jnp.mean(y, axis=-1, keepdims=True)
    yc = y - mu
    yn = yc * lax.rsqrt(jnp.mean(yc * yc, axis=-1, keepdims=True) + EPS) * lg_ref[...] + lb_ref[...]
    return (yn * jax.nn.sigmoid(yn)).astype(BF16)


_MIX_SUB = 512


def _mixed_residual(x_ref, o_ref, zprev_ref, z_ref, znext_ref, gate_ref, cw_ref, cb_ref, lg_ref, lb_ref,
                    wa_ref, wc_ref, wo_ref, out_ref, zp_ref, *, tiles_per_seq):
    tm = x_ref.shape[0]
    pos = pl.program_id(0) % tiles_per_seq
    zp_ref[0:_CONV_HALO, :] = jnp.where(pos > 0, zprev_ref[0], 0.0)
    zp_ref[_CONV_HALO:_CONV_HALO + tm, :] = z_ref[...]
    zp_ref[_CONV_HALO + tm:_CONV_HALO + tm + _CONV_HALO, :] = jnp.where(
        pos < tiles_per_seq - 1, znext_ref[0], 0.0)
    for r0 in range(0, tm, _MIX_SUB):
        rows = slice(r0, r0 + _MIX_SUB)
        conv = jnp.concatenate(
            [jnp.concatenate([_conv_block(zp_ref, cw_ref, r, c0) for c0 in range(0, CONV_CH, LANES)], axis=-1)
             for r in range(r0, r0 + _MIX_SUB, _CONV_ROWS)], axis=0)
        c = _conv_tail(conv, cb_ref, lg_ref, lb_ref)
        attn_d = _dot(o_ref[rows, :], wa_ref[...])
        conv_d = _dot(c, wc_ref[...])
        ga = gate_ref[rows, 0:D_MODEL].astype(F32)
        gc = gate_ref[rows, D_MODEL:2 * D_MODEL].astype(F32)
        mix = (ga * attn_d + gc * conv_d).astype(BF16)
        out_ref[rows, :] = x_ref[rows, :] + _dot(mix, wo_ref[...])


def _mix_kernel(*refs, tiles_per_seq):
    _mixed_residual(*refs, tiles_per_seq=tiles_per_seq)


def _mix_route_kernel(*refs, tiles_per_seq):
    mix_refs, (g_ref, wr_hi_ref, wr_lo_ref, br_ref) = refs[:13], refs[13:17]
    out_ref, meta_ref, plan_ref, cnt_ref, zp_ref, carry_ref = refs[17:]
    _mixed_residual(*mix_refs, out_ref, zp_ref, tiles_per_seq=tiles_per_seq)
    _route(_rms(out_ref[...], g_ref[...]), wr_hi_ref, wr_lo_ref, br_ref, meta_ref, plan_ref, cnt_ref,
           carry_ref)


def _mix(x, o, z, gates, conv_w, conv_b, ln_g, ln_b, wa, wc, wo, seq, router=None, tm=1024):
    t = x.shape[0]
    row = lambda i: (i, 0)
    halo_per_tile = tm // _CONV_HALO
    n_halo = t // _CONV_HALO
    z_halo = z.reshape(n_halo, _CONV_HALO, CONV_CH)
    in_specs = [
        pl.BlockSpec((tm, D_MODEL), row),
        pl.BlockSpec((tm, ATTN_V), row),
        pl.BlockSpec((1, _CONV_HALO, CONV_CH), lambda i: (jnp.maximum(i * halo_per_tile - 1, 0), 0, 0)),
        pl.BlockSpec((tm, CONV_CH), row),
        pl.BlockSpec((1, _CONV_HALO, CONV_CH),
                     lambda i: (jnp.minimum((i + 1) * halo_per_tile, n_halo - 1), 0, 0)),
        pl.BlockSpec((tm, 2 * D_MODEL), row),
        _resident(conv_w.shape), _resident(conv_b.shape), _resident(ln_g.shape), _resident(ln_b.shape),
        _resident(wa.shape), _resident(wc.shape), _resident(wo.shape),
    ]
    args = (x, o, z_halo, z, z_halo, gates, conv_w, conv_b, ln_g, ln_b, wa, wc, wo)
    x_spec = pl.BlockSpec((tm, D_MODEL), row)
    x_shape = jax.ShapeDtypeStruct((t, D_MODEL), F32)
    zp_scratch = pltpu.VMEM((tm + 2 * _CONV_HALO, CONV_CH), F32)
    tiles_per_seq = seq // tm
    if router is None:
        return pl.pallas_call(
            functools.partial(_mix_kernel, tiles_per_seq=tiles_per_seq),
            grid=(t // tm,), in_specs=in_specs, out_specs=x_spec, out_shape=x_shape,
            scratch_shapes=[zp_scratch], compiler_params=_cparams(1), name="branch_mix",
        )(*args)
    return pl.pallas_call(
        functools.partial(_mix_route_kernel, tiles_per_seq=tiles_per_seq),
        grid=(t // tm,),
        in_specs=in_specs + [_resident(a.shape) for a in router],
        out_specs=[x_spec, pl.BlockSpec((tm, LANES), row), pl.BlockSpec((SUBLANES, tm), lambda i: (0, i)),
                   pl.BlockSpec((1, LANES), lambda i: (0, 0))],
        out_shape=[x_shape, jax.ShapeDtypeStruct((t, LANES), F32), jax.ShapeDtypeStruct((SUBLANES, t), F32),
                   jax.ShapeDtypeStruct((1, LANES), F32)],
        scratch_shapes=[zp_scratch, pltpu.VMEM((1, LANES), F32)],
        compiler_params=_cparams(1),
        name="branch_mix_route",
    )(*args, *router)


def _swiglu(h, wg_ref, wu_ref, wd_ref):
    g = _dot(h, wg_ref[...])
    u = _dot(h, wu_ref[...])
    a = (g * jax.nn.sigmoid(g) * u).astype(BF16)
    return _dot(a, wd_ref[...])


def _ffn_kernel(x_ref, g_ref, wg_ref, wu_ref, wd_ref, p_ref, gp_ref, wpg_ref, wpp_ref, gf_ref, *rest,
                n_cast, final_norm):
    cast_in, out_ref, cast_out = rest[:n_cast], rest[n_cast], rest[n_cast + 1:]
    x = x_ref[...]
    h = _rms(x, g_ref[...]).astype(BF16)
    x = x + _swiglu(h, wg_ref, wu_ref, wd_ref)
    _ple_tail(x, p_ref, gp_ref, wpg_ref, wpp_ref, gf_ref, out_ref, final_norm)
    _cast_slabs(cast_in, cast_out)


def _ffn_ple(x, g, w_gu, w_down, p, g_ple, w_gate, w_proj, g_final, final_norm, to_cast=(), tm=256):
    t = x.shape[0]
    d_ff = w_down.shape[0]
    steps = t // tm
    row = lambda i: (i, 0)
    cast_in_specs, cast_out_specs, cast_shapes = _cast_specs(to_cast, steps)
    outs = pl.pallas_call(
        functools.partial(_ffn_kernel, n_cast=len(to_cast), final_norm=final_norm),
        grid=(steps,),
        in_specs=[
            pl.BlockSpec((tm, D_MODEL), row),
            _resident((1, D_MODEL)),
            pl.BlockSpec((D_MODEL, d_ff), lambda i: (0, 0), pipeline_mode=pl.Buffered(1)),
            pl.BlockSpec((D_MODEL, d_ff), lambda i: (0, 1), pipeline_mode=pl.Buffered(1)),
            _resident((d_ff, D_MODEL)),
            pl.BlockSpec((None, tm, p[0].shape[2]), lambda i, layer=p[1]: (layer, i, 0)),
            _resident((1, D_MODEL)),
            _resident(w_gate.shape),
            _resident(w_proj.shape),
            _resident((1, D_MODEL)),
            *cast_in_specs,
        ],
        out_specs=[pl.BlockSpec((tm, D_MODEL), row), *cast_out_specs],
        out_shape=[jax.ShapeDtypeStruct((t, D_MODEL), F32), *cast_shapes],
        compiler_params=_cparams(1),
        name="dense_ffn_ple",
    )(x, g, w_gu, w_gu, w_down, p[0], g_ple, w_gate, w_proj, g_final, *[w for w, _ in to_cast])
    return outs[0], outs[1:]


TM_EXPERT = 256
META_E1, META_E2, META_R1, META_R2, META_W1, META_W2 = range(6)


def _row_copy(src_ref, src_row, dst_ref, dst_row, sem):
    return pltpu.make_async_copy(src_ref.at[pl.ds(src_row, 1)], dst_ref.at[pl.ds(dst_row, 1)], sem)


def _route(hf, wr_hi_ref, wr_lo_ref, br_ref, meta_ref, plan_ref, cnt_ref, carry_ref):
    @pl.when(pl.program_id(0) == 0)
    def _():
        carry_ref[...] = jnp.zeros_like(carry_ref)

    h_hi = hf.astype(BF16)
    h_lo = (hf - h_hi.astype(F32)).astype(BF16)
    logits = (_dot(h_hi, wr_hi_ref[...]) + _dot(h_lo, wr_hi_ref[...])
              + _dot(h_hi, wr_lo_ref[...])) + br_ref[...]
    tm = logits.shape[0]
    lane = lax.broadcasted_iota(jnp.int32, logits.shape, 1)
    neg = jnp.full_like(logits, -jnp.inf)
    logits = jnp.where(lane < N_EXPERTS, logits, neg)
    m1 = jnp.max(logits, axis=-1, keepdims=True)
    i1 = jnp.min(jnp.where(logits == m1, lane, LANES), axis=-1, keepdims=True)
    rest = jnp.where(lane == i1, neg, logits)
    m2 = jnp.max(rest, axis=-1, keepdims=True)
    i2 = jnp.min(jnp.where(rest == m2, lane, LANES), axis=-1, keepdims=True)
    e2 = jnp.exp(m2 - m1)
    denom = 1.0 + e2
    pick1 = lane == i1
    pick2 = lane == i2
    onehot = jnp.where(pick1 | pick2, 1.0, 0.0)
    r = lax.broadcasted_iota(jnp.int32, (tm, tm), 0)
    c = lax.broadcasted_iota(jnp.int32, (tm, tm), 1)
    strict_lower = jnp.where(r > c, 1.0, 0.0).astype(BF16)
    before = _dot(strict_lower, onehot.astype(BF16)) + carry_ref[...]
    rank1 = jnp.sum(jnp.where(pick1, before, 0.0), axis=-1, keepdims=True)
    rank2 = jnp.sum(jnp.where(pick2, before, 0.0), axis=-1, keepdims=True)
    record = jnp.zeros_like(logits)
    for slot, val in ((META_E1, i1.astype(F32)), (META_E2, i2.astype(F32)),
                      (META_R1, rank1), (META_R2, rank2),
                      (META_W1, 1.0 / denom), (META_W2, e2 / denom)):
        record = jnp.where(lane == slot, val, record)
    meta_ref[...] = record
    plan_ref[...] = record.T[0:SUBLANES, :]
    carry_ref[...] += jnp.sum(onehot, axis=0, keepdims=True)
    cnt_ref[...] = carry_ref[...]


def _dispatch_kernel(ends_ref, sizes_ref, x_ref, p1_ref, p2_ref, xs_ref, zero_ref, row_sem, zero_sem):
    tm = x_ref.shape[0]

    @pl.when(pl.program_id(0) == 0)
    def _():
        zero_ref[...] = jnp.zeros_like(zero_ref)
        n_tiles = xs_ref.shape[0] // TM_EXPERT
        total = ends_ref[N_EXPERTS - 1]

        def zero_tile(row):
            start = pl.multiple_of(row, TM_EXPERT)
            return pltpu.make_async_copy(zero_ref, xs_ref.at[pl.ds(start, TM_EXPERT)], zero_sem)

        fills = [(sizes_ref[e] > 0, ends_ref[e] - TM_EXPERT) for e in range(N_EXPERTS)]
        fills += [(j * TM_EXPERT >= total, j * TM_EXPERT) for j in range(n_tiles - N_EXPERTS, n_tiles)]
        for needed, row in fills:
            @pl.when(needed)
            def _():
                zero_tile(row).start()
        for needed, row in fills:
            @pl.when(needed)
            def _():
                zero_tile(row).wait()

    def scatter_row(r, carry):
        _row_copy(x_ref, r, xs_ref, p1_ref[0, 0, r], row_sem).start(priority=0)
        _row_copy(x_ref, r, xs_ref, p2_ref[0, 0, r], row_sem).start(priority=1)
        return carry

    lax.fori_loop(0, tm, scatter_row, 0, unroll=True)
    for _ in range(2):
        pltpu.make_async_copy(x_ref, xs_ref.at[pl.ds(0, tm)], row_sem).wait()


def _dispatch(x, pos1, pos2, ends, sizes, n_rows, tm=512):
    t = x.shape[0]
    idx = lambda a: a.reshape(t // tm, 1, tm)
    smem_block = pl.BlockSpec((1, 1, tm), lambda i, *_: (i, 0, 0), memory_space=pltpu.SMEM)
    grid_spec = pltpu.PrefetchScalarGridSpec(
        num_scalar_prefetch=2,
        grid=(t // tm,),
        in_specs=[pl.BlockSpec((tm, D_MODEL), lambda i, *_: (i, 0)), smem_block, smem_block],
        out_specs=pl.BlockSpec(memory_space=pl.ANY),
        scratch_shapes=[pltpu.VMEM((TM_EXPERT, D_MODEL), F32),
                        pltpu.SemaphoreType.DMA(()), pltpu.SemaphoreType.DMA(())],
    )
    return pl.pallas_call(
        _dispatch_kernel,
        grid_spec=grid_spec,
        out_shape=jax.ShapeDtypeStruct((n_rows, D_MODEL), F32),
        compiler_params=_cparams(1),
        name="moe_dispatch",
    )(ends, sizes, x, idx(pos1), idx(pos2))


def _expert_kernel(tile_expert_ref, tile_src_ref, x_ref, g_ref, wg_ref, wu_ref, wd_ref, y_ref):
    j = pl.program_id(0)

    @pl.when(tile_src_ref[j] != j)
    def _():
        y_ref[...] = jnp.zeros_like(y_ref)

    @pl.when(tile_src_ref[j] == j)
    def _():
        xb = _rms(x_ref[...], g_ref[...]).astype(BF16)
        half = wd_ref.shape[0] // 2
        y = None
        for c0 in (0, half):
            g = _dot(xb, wg_ref[:, c0:c0 + half])
            u = _dot(xb, wu_ref[:, c0:c0 + half])
            a = (g * jax.nn.sigmoid(g) * u).astype(BF16)
            part = _dot(a, wd_ref[c0:c0 + half, :])
            y = part if y is None else y + part
        y_ref[...] = y


def _experts(xs, g, tile_expert, tile_src, we_gu, we_down):
    n_rows = xs.shape[0]
    d_e = we_down.shape[1]
    tile = lambda j, te, ts: (ts[j], 0)
    grid_spec = pltpu.PrefetchScalarGridSpec(
        num_scalar_prefetch=2,
        grid=(n_rows // TM_EXPERT,),
        in_specs=[
            pl.BlockSpec((TM_EXPERT, D_MODEL), tile),
            pl.BlockSpec((1, D_MODEL), lambda j, te, ts: (0, 0)),
            pl.BlockSpec((None, D_MODEL, d_e), lambda j, te, ts: (te[j], 0, 0)),
            pl.BlockSpec((None, D_MODEL, d_e), lambda j, te, ts: (te[j], 0, 1)),
            pl.BlockSpec((None, d_e, D_MODEL), lambda j, te, ts: (te[j], 0, 0)),
        ],
        out_specs=pl.BlockSpec((TM_EXPERT, D_MODEL), lambda j, te, ts: (j, 0)),
    )
    return pl.pallas_call(
        _expert_kernel,
        grid_spec=grid_spec,
        out_shape=jax.ShapeDtypeStruct((n_rows, D_MODEL), F32),
        compiler_params=pltpu.CompilerParams(
            dimension_semantics=("arbitrary",), vmem_limit_bytes=VMEM_LIMIT_EXPERT),
        name="moe_experts",
    )(tile_expert, tile_src, xs, g, we_gu, we_gu, we_down)


def _ple_tail(x, p_ref, g_ref, wg_ref, wp_ref, gf_ref, out_ref, final_norm):
    h = _rms(x, g_ref[...]).astype(BF16)
    gate = jax.nn.sigmoid(_dot(h, wg_ref[...]))
    y = x + gate * _dot(p_ref[...].astype(BF16), wp_ref[...])
    if final_norm:
        y = _rms(y, gf_ref[...])
    out_ref[...] = y


def _combine_ple_kernel(x_ref, meta_ref, p1_ref, p2_ref, p1_next_ref, p2_next_ref, y_ref,
                        p_ref, g_ref, wg_ref, wp_ref, gf_ref, out_ref, rows_ref, sems, *, final_norm):
    i = pl.program_id(0)
    n = pl.num_programs(0)
    tm = x_ref.shape[0]
    slot = i % 2

    def gather(p1, p2, s):
        def gather_row(r, carry):
            _row_copy(y_ref, p1[0, 0, r], rows_ref.at[s, 0], r, sems.at[s]).start(priority=0)
            _row_copy(y_ref, p2[0, 0, r], rows_ref.at[s, 1], r, sems.at[s]).start(priority=1)
            return carry
        lax.fori_loop(0, tm, gather_row, 0, unroll=True)

    @pl.when(i == 0)
    def _():
        gather(p1_ref, p2_ref, 0)

    @pl.when(i + 1 < n)
    def _():
        gather(p1_next_ref, p2_next_ref, 1 - slot)

    for k in range(2):
        pltpu.make_async_copy(y_ref.at[pl.ds(0, tm)], rows_ref.at[slot, k], sems.at[slot]).wait()
    meta = meta_ref[...]
    x = (x_ref[...] + meta[:, META_W1:META_W1 + 1] * rows_ref[slot, 0]
         + meta[:, META_W2:META_W2 + 1] * rows_ref[slot, 1])
    _ple_tail(x, p_ref, g_ref, wg_ref, wp_ref, gf_ref, out_ref, final_norm)


def _combine_ple(x, meta, pos1, pos2, y, p, g, w_gate, w_proj, g_final, final_norm, tm=512):
    t = x.shape[0]
    nb = t // tm
    p_all, layer = p
    row = lambda i: (i, 0)
    idx = lambda a: a.reshape(nb, 1, tm)
    cur = pl.BlockSpec((1, 1, tm), lambda i: (i, 0, 0), memory_space=pltpu.SMEM)
    nxt = pl.BlockSpec((1, 1, tm), lambda i: (jnp.minimum(i + 1, nb - 1), 0, 0),
                       memory_space=pltpu.SMEM)
    return pl.pallas_call(
        functools.partial(_combine_ple_kernel, final_norm=final_norm),
        grid=(nb,),
        in_specs=[
            pl.BlockSpec((tm, D_MODEL), row),
            pl.BlockSpec((tm, LANES), row),
            cur, cur, nxt, nxt,
            pl.BlockSpec(memory_space=pl.ANY),
            pl.BlockSpec((None, tm, p_all.shape[2]), lambda i: (layer, i, 0)),
            _resident((1, D_MODEL)),
            _resident(w_gate.shape),
            _resident(w_proj.shape),
            _resident((1, D_MODEL)),
        ],
        out_specs=pl.BlockSpec((tm, D_MODEL), row),
        out_shape=jax.ShapeDtypeStruct((t, D_MODEL), F32),
        scratch_shapes=[pltpu.VMEM((2, 2, tm, D_MODEL), F32), pltpu.SemaphoreType.DMA((2,))],
        compiler_params=_cparams(1),
        name="moe_combine_ple",
    )(x, meta, idx(pos1), idx(pos2), idx(pos1), idx(pos2), y, p_all, g, w_gate, w_proj, g_final)


def _moe_plan(plan, counts):
    cnt = counts[0, :N_EXPERTS].astype(jnp.int32)
    sizes = (cnt + TM_EXPERT - 1) // TM_EXPERT * TM_EXPERT
    ends = jnp.cumsum(sizes)
    starts = ends - sizes

    def start_of(e):
        out = jnp.zeros_like(e)
        for j in range(N_EXPERTS):
            out = jnp.where(e == j, starts[j], out)
        return out

    pos1 = start_of(plan[META_E1].astype(jnp.int32)) + plan[META_R1].astype(jnp.int32)
    pos2 = start_of(plan[META_E2].astype(jnp.int32)) + plan[META_R2].astype(jnp.int32)
    n_rows = 2 * plan.shape[1] + N_EXPERTS * TM_EXPERT
    n_valid = ends[-1] // TM_EXPERT
    tile_src = jnp.minimum(jnp.arange(n_rows // TM_EXPERT, dtype=jnp.int32), n_valid - 1)
    tile_expert = jnp.sum((tile_src * TM_EXPERT)[:, None] >= ends[None, :], axis=1).astype(jnp.int32)
    return pos1, pos2, ends.astype(jnp.int32), sizes, tile_expert, tile_src, n_rows


def _rope_tables(seq):
    inv_freq = ROPE_THETA ** (-jnp.arange(0, HEAD_DIM, 2, dtype=F32) / HEAD_DIM)
    ang = jnp.arange(seq, dtype=F32)[:, None] * inv_freq[None, :]
    cos, sin = jnp.cos(ang), jnp.sin(ang)
    cos_t = jnp.concatenate([cos] * (LANES // (HEAD_DIM // 2)), axis=-1)
    sin_t = jnp.concatenate([-sin, sin] * (LANES // HEAD_DIM), axis=-1)
    return cos_t, sin_t


def kernel(x, p, g_mix, w_in, lam, g_subln, w_attn_out, conv_w, conv_b, conv_ln_g, conv_ln_b, w_conv_out, w_o, g_ffn, w_ff_gu, w_ff_down, w_router, b_router, we_gu, we_down, g_ple, w_ple_gate, w_ple_proj, g_final):
    batch, seq, _ = x.shape
    depth = w_in.shape[0]
    t = batch * seq
    cos_t, sin_t = _rope_tables(seq)
    xs = x.reshape(t, D_MODEL)
    row2 = lambda a: a.reshape(1, -1)
    expert_bf16 = {}
    w_in_bf16 = w_in[0].astype(BF16)
    for i in range(depth):
        later = {"attn_out": (w_attn_out, i), "conv_out": (w_conv_out, i), "w_o": (w_o, i),
                 "ple_gate": (w_ple_gate, i), "ple_proj": (w_ple_proj, i)}
        if i % 2 == 0:
            later.update(ff_gu=(w_ff_gu, i // 2), ff_down=(w_ff_down, i // 2))
        if i + 1 < depth:
            later["w_in_next"] = (w_in, i + 1)
        (q, k, v, z, gates), narrowed = _in_proj(xs, row2(g_mix[i]), w_in_bf16, cos_t, sin_t, seq,
                                                 tuple(later.values()))
        wb = dict(zip(later.keys(), narrowed))
        w_in_bf16 = wb.get("w_in_next")
        lam_init = 0.8 - 0.6 * math.exp(-0.3 * i)
        lf = lam[i].astype(F32)
        lam_full = (jnp.exp(jnp.sum(lf[0] * lf[1])) - jnp.exp(jnp.sum(lf[2] * lf[3]))
                    + lam_init).reshape(1)
        o = _attention(q, k, v, lam_full, row2(g_subln[i]), batch, seq, 1.0 - lam_init)
        mix_args = (xs, o, z, gates, conv_w[i], row2(conv_b[i]), row2(conv_ln_g[i]), row2(conv_ln_b[i]),
                    wb["attn_out"], wb["conv_out"], wb["w_o"], seq)
        ple_args = ((p.reshape(depth, t, -1), i), row2(g_ple[i]), wb["ple_gate"], wb["ple_proj"],
                    row2(g_final))
        final_norm = i == depth - 1
        if i % 2 == 0:
            xs = _mix(*mix_args)
            m_next = i // 2
            to_cast = ()
            if i + 1 < depth:
                to_cast = ((we_gu.reshape(we_gu.shape[0], -1, we_gu.shape[-1]), m_next),
                           (we_down.reshape(we_down.shape[0], -1, we_down.shape[-1]), m_next))
            xs, cast = _ffn_ple(xs, row2(g_ffn[i]), wb["ff_gu"], wb["ff_down"], *ple_args, final_norm, to_cast)
            if cast:
                expert_bf16[m_next] = (cast[0].reshape(we_gu.shape[1:]), cast[1].reshape(we_down.shape[1:]))
        else:
            m = i // 2
            wr = jnp.pad(w_router[m], ((0, 0), (0, LANES - N_EXPERTS)))
            wr_hi = wr.astype(BF16)
            wr_lo = (wr - wr_hi.astype(F32)).astype(BF16)
            br = jnp.pad(b_router[m], (0, LANES - N_EXPERTS)).reshape(1, LANES)
            xs, meta, plan, counts = _mix(*mix_args, router=(row2(g_ffn[i]), wr_hi, wr_lo, br))
            pos1, pos2, ends, sizes, tile_expert, tile_src, n_rows = _moe_plan(plan, counts)
            routed = _dispatch(xs, pos1, pos2, ends, sizes, n_rows)
            if m not in expert_bf16:
                expert_bf16[m] = (we_gu[m].astype(BF16), we_down[m].astype(BF16))
            y = _experts(routed, row2(g_ffn[i]), tile_expert, tile_src, *expert_bf16[m])
            xs = _combine_ple(xs, meta, pos1, pos2, y, *ple_args, final_norm=final_norm)
    return xs.reshape(batch, seq, D_MODEL)
```
